```python
import math
import jax, jax.numpy as jnp
from jax import lax
import numpy as np

D_MODEL = 2048
BATCH = 4
SEQ = 2048
DEPTH = 2
DEC_BATCH = 128
DEC_SEQ = 8
PAST_LEN = 16384
PAGE_SIZE = 128

POOL_WIDTH = D_MODEL
POOL_WINDOWS = (2, 4, 8, 16)
POOL_GROUPS = len(POOL_WINDOWS)
POOL_GROUP_DIM = POOL_WIDTH // POOL_GROUPS
POOL_BUF = max(POOL_WINDOWS) - 1
SSM_INNER = 2 * D_MODEL
SSM_HEAD_DIM = 64
SSM_HEADS = SSM_INNER // SSM_HEAD_DIM
SSM_STATE = 128
SSM_GROUPS = 8
SSM_CONV = 4
SSM_CHUNK = 128
SSM_CONV_DIM = SSM_INNER + 2 * SSM_GROUPS * SSM_STATE
MEM_LEN = 256
MEM_HEADS = 4
MEM_HEAD_DIM = D_MODEL // MEM_HEADS
FFN_DIM = 256 * ((8 * D_MODEL // 3 + 255) // 256)
FFN_CONV = 3
IN_WIDTHS = (POOL_WIDTH, SSM_INNER, SSM_CONV_DIM, SSM_HEADS, D_MODEL, D_MODEL)
IN_SPLITS = tuple(int(s) for s in np.cumsum(IN_WIDTHS)[:-1])
N_IN = sum(IN_WIDTHS)
EPS = 1e-6

kernel_name = "pool_ssd_gated_hybrid_decode_step"


def _rmsnorm(x, g):
    xf = x.astype(jnp.float32)
    y = xf * lax.rsqrt(jnp.mean(xf * xf, axis=-1, keepdims=True) + EPS)
    return (y * g.astype(jnp.float32)).astype(x.dtype)


def _causal_dwconv(u, prev, w, b):
    k = w.shape[0]
    l = u.shape[1]
    ext = jnp.concatenate([prev.astype(u.dtype), u], axis=1)
    y = ext[:, :l] * w[0]
    for j in range(1, k):
        y = y + ext[:, j:j + l] * w[j]
    return y + b, ext[:, l:]


def _pool_mixer(u, prev, pos, w_group, scale):
    b, l, _ = u.shape
    ext = jnp.concatenate([prev.astype(u.dtype), u], axis=1)
    cs = jnp.cumsum(ext.astype(jnp.float32), axis=1)
    cs = jnp.concatenate([jnp.zeros((b, 1, POOL_WIDTH), jnp.float32), cs], axis=1)
    end = cs[:, POOL_BUF + 1:]
    parts = []
    for k, w in enumerate(POOL_WINDOWS):
        ch = slice(k * POOL_GROUP_DIM, (k + 1) * POOL_GROUP_DIM)
        start = cs[:, POOL_BUF + 1 - w:POOL_BUF + 1 - w + l, ch]
        count = jnp.minimum(pos + 1, w).astype(jnp.float32)[None, :, None]
        parts.append((end[:, :, ch] - start) / count)
    mean = jnp.stack(parts, axis=2)
    diff = mean - u.astype(jnp.float32).reshape(b, l, POOL_GROUPS, POOL_GROUP_DIM)
    mixed = jnp.einsum('blgc,gcd->blgd', diff.astype(u.dtype), w_group).reshape(b, l, POOL_WIDTH)
    return mixed * scale, ext[:, l:]


def _ssd_scan(x, dt, a, bm, cm, h0):
    b, l = x.shape[0], x.shape[1]
    q = SSM_CHUNK if l % SSM_CHUNK == 0 else l
    c = l // q
    r = SSM_HEADS // SSM_GROUPS
    xd = (x * dt[..., None]).reshape(b, c, q, SSM_GROUPS, r, SSM_HEAD_DIM)
    la_cs = jnp.cumsum((dt * a).reshape(b, c, q, SSM_GROUPS, r), axis=2)
    bm = bm.reshape(b, c, q, SSM_GROUPS, SSM_STATE)
    cm = cm.reshape(b, c, q, SSM_GROUPS, SSM_STATE)
    causal = jnp.tril(jnp.ones((q, q), bool))[None, None, :, :, None, None]
    seg = la_cs[:, :, :, None] - la_cs[:, :, None, :]
    decay = jnp.exp(jnp.where(causal, seg, -jnp.inf))
    cb = jnp.einsum('bctgn,bcsgn->bctsg', cm, bm)
    y_diag = jnp.einsum('bctsgr,bcsgrp->bctgrp', cb[..., None] * decay, xd)
    to_end = jnp.exp(la_cs[:, :, -1:] - la_cs)
    chunk_states = jnp.einsum('bcsgn,bcsgr,bcsgrp->bcgrpn', bm, to_end, xd)
    chunk_decay = jnp.exp(la_cs[:, :, -1])

    def step(h, inp):
        s, d = inp
        return h * d[..., None, None] + s, h

    h_last, h_in = lax.scan(step, h0.reshape(b, SSM_GROUPS, r, SSM_HEAD_DIM, SSM_STATE),
                            (jnp.moveaxis(chunk_states, 1, 0), jnp.moveaxis(chunk_decay, 1, 0)))
    h_in = jnp.moveaxis(h_in, 0, 1)
    y_off = jnp.einsum('bctgn,bcgrpn,bctgr->bctgrp', cm, h_in, jnp.exp(la_cs))
    y = (y_diag + y_off).reshape(b, l, SSM_HEADS, SSM_HEAD_DIM)
    return y, h_last.reshape(b, SSM_HEADS, SSM_HEAD_DIM, SSM_STATE)


def _mixer(h, pos, pool_prev, conv_prev, ssm_prev, w_in, w_pool_group, pool_scale, w_pool_out,
           ssm_conv_w, ssm_conv_b, ssm_dt_bias, ssm_a_log, ssm_d, ssm_norm, w_ssm_out, w_out):
    f32 = jnp.float32
    b, l, _ = h.shape
    u, z, xbc, dt_raw, g_pool, g_ssm = jnp.split(h @ w_in, IN_SPLITS, axis=-1)
    pooled, pool_new = _pool_mixer(u, pool_prev, pos, w_pool_group, pool_scale)
    out_pool = pooled @ w_pool_out
    xbc, conv_new = _causal_dwconv(xbc, conv_prev, ssm_conv_w, ssm_conv_b)
    xbc = jax.nn.silu(xbc)
    xs, bm, cm = jnp.split(xbc, (SSM_INNER, SSM_INNER + SSM_GROUPS * SSM_STATE), axis=-1)
    dt = jax.nn.softplus(dt_raw.astype(f32) + ssm_dt_bias.astype(f32))
    a = -jnp.exp(ssm_a_log.astype(f32))
    xh = xs.astype(f32).reshape(b, l, SSM_HEADS, SSM_HEAD_DIM)
    y, ssm_new = _ssd_scan(xh, dt, a,
                           bm.astype(f32).reshape(b, l, SSM_GROUPS, SSM_STATE),
                           cm.astype(f32).reshape(b, l, SSM_GROUPS, SSM_STATE),
                           ssm_prev.astype(f32))
    y = y + ssm_d.astype(f32)[:, None] * xh
    y = y.reshape(b, l, SSM_INNER) * jax.nn.silu(z.astype(f32))
    yg = y.reshape(b, l, SSM_GROUPS, SSM_INNER // SSM_GROUPS)
    yg = yg * lax.rsqrt(jnp.mean(yg * yg, axis=-1, keepdims=True) + EPS)
    y = (yg.reshape(b, l, SSM_INNER) * ssm_norm.astype(f32)).astype(h.dtype)
    out_ssm = y @ w_ssm_out
    merged = jax.nn.sigmoid(g_pool) * out_pool + jax.nn.sigmoid(g_ssm) * out_ssm
    return merged @ w_out, pool_new, conv_new, ssm_new.astype(ssm_prev.dtype)


def _mem_kv(mem, norm_g, w_k, w_v):
    b, m, _ = mem.shape
    mn = _rmsnorm(mem, norm_g)
    k = (mn @ w_k).reshape(b, m, MEM_HEADS, MEM_HEAD_DIM)
    v = (mn @ w_v).reshape(b, m, MEM_HEADS, MEM_HEAD_DIM)
    return k, v


def _cross_attn(h, k, v, w_q, w_o):
    b, l, _ = h.shape
    q = (h @ w_q).reshape(b, l, MEM_HEADS, MEM_HEAD_DIM)
    s = jnp.einsum('blhd,bmhd->bhlm', q, k.astype(q.dtype)).astype(jnp.float32) / math.sqrt(MEM_HEAD_DIM)
    p = jax.nn.softmax(s, axis=-1).astype(h.dtype)
    o = jnp.einsum('bhlm,bmhd->blhd', p, v.astype(h.dtype)).reshape(b, l, D_MODEL)
    return o @ w_o


def _conv_ffn(h, prev, w_up, conv_w, conv_b, w_down):
    up, new = _causal_dwconv(h @ w_up, prev, conv_w, conv_b)
    g, v = jnp.split(up, 2, axis=-1)
    return (jax.nn.silu(g) * v) @ w_down, new


def _layer(x, pos, pool_prev, conv_prev, ssm_prev, ffn_prev, mem_k, mem_v, lw):
    (norm_mix, w_in, w_pool_group, pool_scale, w_pool_out, ssm_conv_w, ssm_conv_b, ssm_dt_bias,
     ssm_a_log, ssm_d, ssm_norm, w_ssm_out, w_out, norm_mem_q, w_mem_q, w_mem_o,
     norm_ffn, w_ffn_up, ffn_conv_w, ffn_conv_b, w_ffn_down) = lw
    mix, pool_new, conv_new, ssm_new = _mixer(
        _rmsnorm(x, norm_mix), pos, pool_prev, conv_prev, ssm_prev, w_in, w_pool_group, pool_scale,
        w_pool_out, ssm_conv_w, ssm_conv_b, ssm_dt_bias, ssm_a_log, ssm_d, ssm_norm, w_ssm_out, w_out)
    x = x + mix
    x = x + _cross_attn(_rmsnorm(x, norm_mem_q), mem_k, mem_v, w_mem_q, w_mem_o)
    ffn, ffn_new = _conv_ffn(_rmsnorm(x, norm_ffn), ffn_prev, w_ffn_up, ffn_conv_w, ffn_conv_b, w_ffn_down)
    x = x + ffn
    return x, pool_new, conv_new, ssm_new, ffn_new


def setup_inputs(seed: int = 0) -> dict:
    key = jax.random.key(seed)
    ks = jax.random.split(key, 40)
    f32 = jnp.float32

    def nrm(i, shape, scale):
        return scale * jax.random.normal(ks[i], shape, f32)

    def gain(i, shape):
        return 1.0 + 0.02 * jax.random.normal(ks[i], shape, f32)

    dt0 = jnp.exp(jax.random.uniform(ks[30], (DEPTH, SSM_HEADS), f32) * (math.log(0.1) - math.log(0.001)) + math.log(0.001))
    return {
        "x_prompt": nrm(0, (BATCH, SEQ, D_MODEL), 1.0),
        "x_sample": nrm(1, (DEC_BATCH, DEC_SEQ, D_MODEL), 1.0),
        "state_pool": nrm(2, (DEPTH, DEC_BATCH, POOL_BUF, POOL_WIDTH), 1.0),
        "state_ssm_conv": nrm(3, (DEPTH, DEC_BATCH, SSM_CONV - 1, SSM_CONV_DIM), 1.0),
        "state_ssm": nrm(4, (DEPTH, DEC_BATCH, SSM_HEADS, SSM_HEAD_DIM, SSM_STATE), 0.1),
        "state_ffn_conv": nrm(5, (DEPTH, DEC_BATCH, FFN_CONV - 1, 2 * FFN_DIM), 1.0),
        "cache_mem_k": nrm(6, (DEPTH, DEC_BATCH, MEM_LEN, MEM_HEADS, MEM_HEAD_DIM), 1.0),
        "cache_mem_v": nrm(7, (DEPTH, DEC_BATCH, MEM_LEN, MEM_HEADS, MEM_HEAD_DIM), 1.0),
        "mem_prompt": nrm(8, (BATCH, MEM_LEN, D_MODEL), 1.0),
        "norm_mix": gain(9, (DEPTH, D_MODEL)),
        "w_in": nrm(10, (DEPTH, D_MODEL, N_IN), D_MODEL ** -0.5),
        "w_pool_group": nrm(11, (DEPTH, POOL_GROUPS, POOL_GROUP_DIM, POOL_GROUP_DIM), POOL_GROUP_DIM ** -0.5),
        "pool_scale": 1.0 + 0.1 * jax.random.normal(ks[12], (DEPTH, POOL_WIDTH), f32),
        "w_pool_out": nrm(13, (DEPTH, POOL_WIDTH, D_MODEL), POOL_WIDTH ** -0.5),
        "ssm_conv_w": nrm(14, (DEPTH, SSM_CONV, SSM_CONV_DIM), SSM_CONV ** -0.5),
        "ssm_conv_b": nrm(15, (DEPTH, SSM_CONV_DIM), 0.01),
        "ssm_dt_bias": dt0 + jnp.log(-jnp.expm1(-dt0)),
        "ssm_a_log": jnp.log(jax.random.uniform(ks[16], (DEPTH, SSM_HEADS), f32, 1.0, 16.0)),
        "ssm_d": gain(17, (DEPTH, SSM_HEADS)),
        "ssm_norm": gain(18, (DEPTH, SSM_INNER)),
        "w_ssm_out": nrm(19, (DEPTH, SSM_INNER, D_MODEL), SSM_INNER ** -0.5),
        "w_out": nrm(20, (DEPTH, D_MODEL, D_MODEL), D_MODEL ** -0.5),
        "norm_mem_q": gain(21, (DEPTH, D_MODEL)),
        "w_mem_q": nrm(22, (DEPTH, D_MODEL, D_MODEL), D_MODEL ** -0.5),
        "w_mem_o": nrm(23, (DEPTH, D_MODEL, D_MODEL), D_MODEL ** -0.5),
        "norm_mem_kv": gain(24, (DEPTH, D_MODEL)),
        "w_mem_k": nrm(25, (DEPTH, D_MODEL, D_MODEL), D_MODEL ** -0.5),
        "w_mem_v": nrm(26, (DEPTH, D_MODEL, D_MODEL), D_MODEL ** -0.5),
        "norm_ffn": gain(27, (DEPTH, D_MODEL)),
        "w_ffn_up": nrm(28, (DEPTH, D_MODEL, 2 * FFN_DIM), D_MODEL ** -0.5),
        "ffn_conv_w": nrm(29, (DEPTH, FFN_CONV, 2 * FFN_DIM), FFN_CONV ** -0.5),
        "ffn_conv_b": nrm(31, (DEPTH, 2 * FFN_DIM), 0.01),
        "w_ffn_down": nrm(32, (DEPTH, FFN_DIM, D_MODEL), FFN_DIM ** -0.5),
        "norm_final": gain(33, (D_MODEL,)),
    }


def reference(x_prompt, x_sample, state_pool, state_ssm_conv, state_ssm, state_ffn_conv, cache_mem_k, cache_mem_v,
              mem_prompt, norm_mix, w_in, w_pool_group, pool_scale, w_pool_out, ssm_conv_w, ssm_conv_b, ssm_dt_bias,
              ssm_a_log, ssm_d, ssm_norm, w_ssm_out, w_out, norm_mem_q, w_mem_q, w_mem_o, norm_mem_kv, w_mem_k,
              w_mem_v, norm_ffn, w_ffn_up, ffn_conv_w, ffn_conv_b, w_ffn_down, norm_final):
    bp, lp, _ = x_prompt.shape
    ls = x_sample.shape[1]
    dtp = x_prompt.dtype
    pos_p = jnp.arange(lp, dtype=jnp.int32)
    pos_s = PAST_LEN + jnp.arange(ls, dtype=jnp.int32)
    shared = (norm_mix, w_in, w_pool_group, pool_scale, w_pool_out, ssm_conv_w, ssm_conv_b, ssm_dt_bias,
              ssm_a_log, ssm_d, ssm_norm, w_ssm_out, w_out, norm_mem_q, w_mem_q, w_mem_o,
              norm_ffn, w_ffn_up, ffn_conv_w, ffn_conv_b, w_ffn_down)
    zero_pool = jnp.zeros((bp, POOL_BUF, POOL_WIDTH), dtp)
    zero_conv = jnp.zeros((bp, SSM_CONV - 1, SSM_CONV_DIM), dtp)
    zero_ssm = jnp.zeros((bp, SSM_HEADS, SSM_HEAD_DIM, SSM_STATE), dtp)
    zero_ffn = jnp.zeros((bp, FFN_CONV - 1, 2 * FFN_DIM), dtp)
    yp, ys = x_prompt, x_sample
    pool_p, pool_s, conv_p, conv_s, ssm_p, ssm_s, ffn_p, ffn_s, mk_p, mv_p = ([] for _ in range(10))
    for i in range(DEPTH):
        lw = tuple(w[i] for w in shared)
        k_i, v_i = _mem_kv(mem_prompt, norm_mem_kv[i], w_mem_k[i], w_mem_v[i])
        yp, a0, a1, a2, a3 = _layer(yp, pos_p, zero_pool, zero_conv, zero_ssm, zero_ffn, k_i, v_i, lw)
        ys, b0, b1, b2, b3 = _layer(ys, pos_s, state_pool[i], state_ssm_conv[i], state_ssm[i], state_ffn_conv[i],
                                    cache_mem_k[i], cache_mem_v[i], lw)
        pool_p.append(a0); conv_p.append(a1); ssm_p.append(a2); ffn_p.append(a3)
        pool_s.append(b0); conv_s.append(b1); ssm_s.append(b2); ffn_s.append(b3)
        mk_p.append(k_i); mv_p.append(v_i)
    y_prompt = _rmsnorm(yp, norm_final)
    y_sample = _rmsnorm(ys, norm_final)
    return (y_prompt, y_sample,
            jnp.stack(pool_p), jnp.stack(pool_s),
            jnp.stack(conv_p), jnp.stack(conv_s),
            jnp.stack(ssm_p), jnp.stack(ssm_s),
            jnp.stack(ffn_p), jnp.stack(ffn_s),
            jnp.stack(mk_p), jnp.stack(mv_p))
```

```python
import functools
import math

import jax
import jax.numpy as jnp
from jax import lax
from jax.experimental import pallas as pl
from jax.experimental.pallas import tpu as pltpu

F32 = jnp.float32
BF16 = jnp.bfloat16
EPS = 1e-6

LANES = 128
SUBLANES = 8
VMEM_LIMIT = 56 * 1024 * 1024

PAST_LEN = 16384
POOL_WINDOWS = (2, 4, 8, 16)
POOL_HALO = 16
CONV_HALO = SUBLANES


def _cparams(*sem):
    return pltpu.CompilerParams(dimension_semantics=sem, vmem_limit_bytes=VMEM_LIMIT)


def _silu(x):
    return x * jax.nn.sigmoid(x)


def _norm_matmul_kernel(x_ref, g_ref, w_ref, *rest, has_aux):
    if has_aux:
        wa_ref, o_ref, oa_ref, xn_ref = rest
    else:
        o_ref, xn_ref = rest

    @pl.when(pl.program_id(1) == 0)
    def _():
        x = x_ref[...]
        ms = jnp.mean(x * x, axis=-1, keepdims=True)
        xn = (x * lax.rsqrt(ms + EPS) * g_ref[...]).astype(BF16)
        xn_ref[...] = xn
        if has_aux:
            oa_ref[...] = jnp.dot(xn, wa_ref[...], preferred_element_type=F32)

    o_ref[...] = jnp.dot(xn_ref[...], w_ref[...], preferred_element_type=F32).astype(o_ref.dtype)


def norm_matmul(x, g, w, w_aux=None, *, tm, tn, out_dtype=F32):
    t, k = x.shape
    n = w.shape[1]
    assert t % tm == 0 and n % tn == 0
    has_aux = w_aux is not None
    in_specs = [
        pl.BlockSpec((tm, k), lambda i, j: (i, 0)),
        pl.BlockSpec((1, k), lambda i, j: (0, 0)),
        pl.BlockSpec((k, tn), lambda i, j: (0, j)),
    ]
    args = [x, g.reshape(1, k), w]
    out_shape = [jax.ShapeDtypeStruct((t, n), out_dtype)]
    out_specs = [pl.BlockSpec((tm, tn), lambda i, j: (i, j))]
    if has_aux:
        na = w_aux.shape[1]
        in_specs.append(pl.BlockSpec((k, na), lambda i, j: (0, 0)))
        args.append(w_aux)
        out_shape.append(jax.ShapeDtypeStruct((t, na), F32))
        out_specs.append(pl.BlockSpec((tm, na), lambda i, j: (i, 0)))
    res = pl.pallas_call(
        functools.partial(_norm_matmul_kernel, has_aux=has_aux),
        grid=(t // tm, n // tn),
        in_specs=in_specs,
        out_specs=out_specs,
        out_shape=out_shape,
        scratch_shapes=[pltpu.VMEM((tm, k), BF16)],
        compiler_params=_cparams("parallel", "arbitrary"),
    )(*args)
    return res if has_aux else res[0]


def _matmul_kernel(a_ref, w_ref, *rest, epilogue):
    *extra, o_ref = rest
    acc = jnp.dot(a_ref[...].astype(BF16), w_ref[...], preferred_element_type=F32)
    o_ref[...] = epilogue(acc, *[e[...] for e in extra]).astype(o_ref.dtype)


def _ep_residual(acc, r):
    return r + acc


def _ep_gate_merge(acc, gated_pool, g_ssm):
    return gated_pool + jax.nn.sigmoid(g_ssm) * acc


def matmul(a, w, extras, epilogue, *, tm, tn, out_dtype):
    t, k = a.shape
    n = w.shape[1]
    assert t % tm == 0 and n % tn == 0
    in_specs = [
        pl.BlockSpec((tm, k), lambda i, j: (i, 0)),
        pl.BlockSpec((k, tn), lambda i, j: (0, j)),
    ]
    args = [a, w]
    for arr, off in extras:
        assert off % tn == 0
        ob = off // tn
        in_specs.append(pl.BlockSpec((tm, tn), lambda i, j, ob=ob: (i, j + ob)))
        args.append(arr)
    return pl.pallas_call(
        functools.partial(_matmul_kernel, epilogue=epilogue),
        grid=(t // tm, n // tn),
        in_specs=in_specs,
        out_specs=pl.BlockSpec((tm, tn), lambda i, j: (i, j)),
        out_shape=jax.ShapeDtypeStruct((t, n), out_dtype),
        compiler_params=_cparams("parallel", "arbitrary"),
    )(*args)


def _rmsnorm_kernel(x_ref, g_ref, o_ref):
    x = x_ref[...]
    ms = jnp.mean(x * x, axis=-1, keepdims=True)
    o_ref[...] = x * lax.rsqrt(ms + EPS) * g_ref[...]


def rmsnorm(x, g, *, tm):
    t, k = x.shape
    return pl.pallas_call(
        _rmsnorm_kernel,
        grid=(t // tm,),
        in_specs=[pl.BlockSpec((tm, k), lambda i: (i, 0)), pl.BlockSpec((1, k), lambda i: (0, 0))],
        out_specs=pl.BlockSpec((tm, k), lambda i: (i, 0)),
        out_shape=jax.ShapeDtypeStruct((t, k), F32),
        compiler_params=_cparams("parallel"),
    )(x, g.reshape(1, k))


def _pool_kernel(cur_ref, prev_ref, gate_ref, wg_ref, scale_ref, wo_ref, o_ref, ext_ref, pooled_ref,
                 *, nb, L, from_state, pos0):
    w_ch = cur_ref.shape[-1]
    gdim = w_ch // len(POOL_WINDOWS)
    if from_state:
        ext_ref[:, 1:POOL_HALO, :] = prev_ref[...]
        pos_start = pos0
    else:
        i = pl.program_id(1)
        ext_ref[:, 0:POOL_HALO, :] = jnp.where(i == 0, 0.0, prev_ref[...])
        pos_start = pos0 + i * L
    ext_ref[:, POOL_HALO:POOL_HALO + L, :] = cur_ref[...]

    pos = pos_start + lax.broadcasted_iota(jnp.int32, (1, L, gdim), 1)
    for k, win in enumerate(POOL_WINDOWS):
        cs = slice(k * gdim, (k + 1) * gdim)
        cur = ext_ref[:, POOL_HALO:POOL_HALO + L, cs]
        acc = cur
        for j in range(1, win):
            acc = acc + ext_ref[:, POOL_HALO - j:POOL_HALO - j + L, cs]
        count = jnp.minimum(pos + 1, win).astype(F32)
        diff = (acc / count - cur).reshape(nb * L, gdim).astype(BF16)
        mixed = jnp.dot(diff, wg_ref[k], preferred_element_type=F32)
        pooled_ref[:, cs] = (mixed * scale_ref[:, cs]).astype(BF16)
    out_pool = jnp.dot(pooled_ref[...], wo_ref[...], preferred_element_type=F32)
    gate = jax.nn.sigmoid(gate_ref[...].reshape(nb * L, w_ch))
    o_ref[...] = (gate * out_pool).reshape(nb, L, w_ch)


def pool_branch(main3, prev_state, wg, scale, wo, *, u_col, gate_col, nb, L, pos0):
    b, s, _ = main3.shape
    w_ch = wo.shape[0]
    ub, gb = u_col // w_ch, gate_col // w_ch
    from_state = prev_state is not None
    if from_state:
        assert L == s
        prev = prev_state
        prev_spec = pl.BlockSpec((nb, POOL_HALO - 1, w_ch), lambda n, i: (n, 0, 0))
    else:
        assert nb == 1 and L % POOL_HALO == 0
        prev = main3
        r = L // POOL_HALO
        prev_spec = pl.BlockSpec((1, POOL_HALO, w_ch), lambda n, i: (n, jnp.maximum(i * r - 1, 0), ub))
    return pl.pallas_call(
        functools.partial(_pool_kernel, nb=nb, L=L, from_state=from_state, pos0=pos0),
        grid=(b // nb, s // L),
        in_specs=[
            pl.BlockSpec((nb, L, w_ch), lambda n, i: (n, i, ub)),
            prev_spec,
            pl.BlockSpec((nb, L, w_ch), lambda n, i: (n, i, gb)),
            pl.BlockSpec(wg.shape, lambda n, i: (0, 0, 0)),
            pl.BlockSpec((1, w_ch), lambda n, i: (0, 0)),
            pl.BlockSpec(wo.shape, lambda n, i: (0, 0)),
        ],
        out_specs=pl.BlockSpec((nb, L, w_ch), lambda n, i: (n, i, 0)),
        out_shape=jax.ShapeDtypeStruct((b, s, w_ch), F32),
        scratch_shapes=[pltpu.VMEM((nb, POOL_HALO + L, w_ch), F32), pltpu.VMEM((nb * L, w_ch), BF16)],
        compiler_params=_cparams("parallel", "arbitrary"),
    )(main3, prev, main3, wg, scale.reshape(1, w_ch), wo)


def _cumsum_rows(x):
    q = x.shape[0]
    row = lax.broadcasted_iota(jnp.int32, x.shape, 0)
    k = 1
    while k < q:
        x = x + jnp.where(row >= k, pltpu.roll(x, k, axis=0), 0.0)
        k *= 2
    return x


def _expand_heads(v, e_ref):
    hi = v.astype(BF16)
    r1 = v - hi.astype(F32)
    mid = r1.astype(BF16)
    lo = (r1 - mid.astype(F32)).astype(BF16)
    e = e_ref[...]
    out = jnp.dot(hi, e, preferred_element_type=F32)
    out = out + jnp.dot(mid, e, preferred_element_type=F32)
    return out + jnp.dot(lo, e, preferred_element_type=F32)


def _ssd_kernel(*refs, q, n_groups, heads_per_group, head_dim, n_state, from_state, conv_k):
    if from_state:
        (xbc_ref, prev_ref, z_ref, dt_ref, h0_ref, cw_ref, cb_ref, dtb_ref, alog_ref, dexp_ref, norm_ref, e_ref,
         y_ref, hout_ref, ext_ref, act_ref, h_ref, yacc_ref) = refs
    else:
        (xbc_ref, prev_ref, z_ref, dt_ref, cw_ref, cb_ref, dtb_ref, alog_ref, dexp_ref, norm_ref, e_ref,
         y_ref, hout_ref, ext_ref, act_ref, h_ref, yacc_ref) = refs
    c = pl.program_id(1)
    gw = heads_per_group * head_dim
    inner = n_groups * gw
    hist = conv_k - 1

    if from_state:
        h_ref[...] = h0_ref[0]
        ext_ref[CONV_HALO - hist:CONV_HALO, :] = prev_ref[0]
    else:
        @pl.when(c == 0)
        def _():
            h_ref[...] = jnp.zeros_like(h_ref)
        ext_ref[0:CONV_HALO, :] = jnp.where(c == 0, 0.0, prev_ref[0])
    ext_ref[CONV_HALO:CONV_HALO + q, :] = xbc_ref[0]

    acc = ext_ref[CONV_HALO - hist:CONV_HALO - hist + q, :] * cw_ref[0:1, :]
    for j in range(1, conv_k):
        acc = acc + ext_ref[CONV_HALO - hist + j:CONV_HALO - hist + j + q, :] * cw_ref[j:j + 1, :]
    act_ref[...] = _silu(acc + cb_ref[...])

    dt = jax.nn.softplus(dt_ref[0] + dtb_ref[...])
    la_cs = _cumsum_rows(dt * (-jnp.exp(alog_ref[...])))
    la_cs_t = la_cs.T
    dt_x = _expand_heads(dt, e_ref)
    la_x = _expand_heads(la_cs, e_ref)
    from_start_x = jnp.exp(la_x)
    to_end_x = jnp.exp(la_x[q - 1:q, :] - la_x)
    chunk_decay = jnp.exp(la_cs[q - 1:q, :])

    xs = act_ref[:, 0:inner]
    xd = xs * dt_x
    xdw = xd * to_end_x
    yacc_ref[...] = dexp_ref[...] * xs

    tri = lax.broadcasted_iota(jnp.int32, (q, q), 0) >= lax.broadcasted_iota(jnp.int32, (q, q), 1)
    assert LANES % head_dim == 0
    hpl = LANES // head_dim
    lane_head = lax.broadcasted_iota(jnp.int32, (q, LANES), 1) // head_dim
    for g in range(n_groups):
        gs = slice(g * gw, (g + 1) * gw)
        bm = act_ref[:, inner + g * n_state:inner + (g + 1) * n_state].astype(BF16)
        cm = act_ref[:, inner + (n_groups + g) * n_state:inner + (n_groups + g + 1) * n_state].astype(BF16)
        cb = lax.dot_general(cm, bm, (((1,), (1,)), ((), ())), preferred_element_type=F32)
        h_g = h_ref[g]
        y_off = lax.dot_general(cm, h_g.astype(BF16), (((1,), (1,)), ((), ())),
                                preferred_element_type=F32)
        yacc_ref[:, gs] += y_off * from_start_x[:, gs]
        for lt in range(gw // LANES):
            ls = slice(g * gw + lt * LANES, g * gw + (lt + 1) * LANES)
            xd_t = xd[:, ls]
            y_t = None
            for k in range(hpl):
                hd = (g * gw + lt * LANES) // head_dim + k
                seg = la_cs[:, hd:hd + 1] - la_cs_t[hd:hd + 1, :]
                m = (cb * jnp.exp(jnp.where(tri, seg, -jnp.inf))).astype(BF16)
                rhs = jnp.where(lane_head == k, xd_t, 0.0).astype(BF16)
                part = jnp.dot(m, rhs, preferred_element_type=F32)
                y_t = part if y_t is None else y_t + part
            yacc_ref[:, ls] += y_t
        s_new = lax.dot_general(xdw[:, gs].astype(BF16), bm, (((0,), (0,)), ((), ())),
                                preferred_element_type=F32)
        for r in range(heads_per_group):
            hd = g * heads_per_group + r
            rows = slice(r * head_dim, (r + 1) * head_dim)
            h_ref[g, rows, :] = h_g[rows, :] * chunk_decay[:, hd:hd + 1] + s_new[rows, :]

    y = yacc_ref[...] * _silu(z_ref[0])
    for g in range(n_groups):
        gs = slice(g * gw, (g + 1) * gw)
        yg = y[:, gs]
        ms = jnp.mean(yg * yg, axis=-1, keepdims=True)
        y_ref[0, :, gs] = (yg * lax.rsqrt(ms + EPS) * norm_ref[:, gs]).astype(y_ref.dtype)

    hout_ref[0] = h_ref[...]


def ssd_branch(main3, dt3, conv_state, ssm_state, cw, cb, dt_bias, a_log, d_exp, norm, expand,
               *, xbc_col, z_col, q, n_groups, heads_per_group, head_dim, n_state, out_dtype):
    b, s, _ = main3.shape
    gw = heads_per_group * head_dim
    inner = n_groups * gw
    conv_dim = inner + 2 * n_groups * n_state
    conv_k = cw.shape[0]
    from_state = ssm_state is not None
    xb, zb = xbc_col // conv_dim, z_col // inner
    assert xbc_col % conv_dim == 0 and z_col % inner == 0

    def const(shape):
        nd = len(shape)
        return pl.BlockSpec(shape, lambda n, c: (0,) * nd)

    in_specs = [pl.BlockSpec((1, q, conv_dim), lambda n, c: (n, c, xb))]
    args = [main3]
    if from_state:
        assert q == s
        in_specs.append(pl.BlockSpec((1, conv_k - 1, conv_dim), lambda n, c: (n, 0, 0)))
        args.append(conv_state)
    else:
        assert q % CONV_HALO == 0
        r = q // CONV_HALO
        in_specs.append(pl.BlockSpec((1, CONV_HALO, conv_dim), lambda n, c: (n, jnp.maximum(c * r - 1, 0), xb)))
        args.append(main3)
    in_specs += [pl.BlockSpec((1, q, inner), lambda n, c: (n, c, zb)),
                 pl.BlockSpec((1, q, LANES), lambda n, c: (n, c, 0))]
    args += [main3, dt3]
    if from_state:
        in_specs.append(pl.BlockSpec((1, n_groups, gw, n_state), lambda n, c: (n, 0, 0, 0)))
        args.append(ssm_state)
    small = [cw, cb.reshape(1, conv_dim), dt_bias, a_log, d_exp, norm.reshape(1, inner), expand]
    in_specs += [const(a.shape) for a in small]
    args += small
    return pl.pallas_call(
        functools.partial(_ssd_kernel, q=q, n_groups=n_groups, heads_per_group=heads_per_group,
                          head_dim=head_dim, n_state=n_state, from_state=from_state, conv_k=conv_k),
        grid=(b, s // q),
        in_specs=in_specs,
        out_specs=[pl.BlockSpec((1, q, inner), lambda n, c: (n, c, 0)),
                   pl.BlockSpec((1, n_groups, gw, n_state), lambda n, c: (n, 0, 0, 0))],
        out_shape=[jax.ShapeDtypeStruct((b, s, inner), out_dtype),
                   jax.ShapeDtypeStruct((b, n_groups, gw, n_state), F32)],
        scratch_shapes=[pltpu.VMEM((CONV_HALO + q, conv_dim), F32),
                        pltpu.VMEM((q, conv_dim), F32),
                        pltpu.VMEM((n_groups, gw, n_state), F32),
                        pltpu.VMEM((q, inner), F32)],
        compiler_params=_cparams("parallel", "arbitrary"),
    )(*args)


def _attn_kernel(q_ref, k_ref, v_ref, o_ref, *, n_heads):
    d = q_ref.shape[-1] // n_heads
    inv = 1.0 / math.sqrt(d)
    for h in range(n_heads):
        hs = slice(h * d, (h + 1) * d)
        qh = q_ref[0, :, hs].astype(BF16)
        kh = k_ref[0, :, hs].astype(BF16)
        vh = v_ref[0, :, hs].astype(BF16)
        s = lax.dot_general(qh, kh, (((1,), (1,)), ((), ())), preferred_element_type=F32) * inv
        e = jnp.exp(s - jnp.max(s, axis=-1, keepdims=True))
        p = (e / jnp.sum(e, axis=-1, keepdims=True)).astype(BF16)
        o_ref[0, :, hs] = jnp.dot(p, vh, preferred_element_type=F32).astype(o_ref.dtype)


def cross_attention(q3, k3, v3, *, tq, n_heads, out_dtype):
    b, s, dm = q3.shape
    m = k3.shape[1]
    return pl.pallas_call(
        functools.partial(_attn_kernel, n_heads=n_heads),
        grid=(b, s // tq),
        in_specs=[pl.BlockSpec((1, tq, dm), lambda n, i: (n, i, 0)),
                  pl.BlockSpec((1, m, dm), lambda n, i: (n, 0, 0)),
                  pl.BlockSpec((1, m, dm), lambda n, i: (n, 0, 0))],
        out_specs=pl.BlockSpec((1, tq, dm), lambda n, i: (n, i, 0)),
        out_shape=jax.ShapeDtypeStruct((b, s, dm), out_dtype),
        compiler_params=_cparams("parallel", "arbitrary"),
    )(q3, k3, v3)


def _ffn_mid_kernel(g_ref, v_ref, gp_ref, vp_ref, gw_ref, vw_ref, gb_ref, vb_ref, o_ref, gext_ref, vext_ref,
                    *, L, from_state, conv_k):
    hist = conv_k - 1
    i = pl.program_id(1)

    def conv(cur_ref, prev_ref, w_ref, b_ref, ext_ref):
        if from_state:
            ext_ref[:, CONV_HALO - hist:CONV_HALO, :] = prev_ref[...]
        else:
            ext_ref[:, 0:CONV_HALO, :] = jnp.where(i == 0, 0.0, prev_ref[...])
        ext_ref[:, CONV_HALO:CONV_HALO + L, :] = cur_ref[...]
        acc = ext_ref[:, CONV_HALO - hist:CONV_HALO - hist + L, :] * w_ref[0:1, :]
        for j in range(1, conv_k):
            acc = acc + ext_ref[:, CONV_HALO - hist + j:CONV_HALO - hist + j + L, :] * w_ref[j:j + 1, :]
        return acc + b_ref[...]

    g = conv(g_ref, gp_ref, gw_ref, gb_ref, gext_ref)
    v = conv(v_ref, vp_ref, vw_ref, vb_ref, vext_ref)
    o_ref[...] = (_silu(g) * v).astype(o_ref.dtype)


def ffn_mid(up3, prev_state, cw, cb, *, nb, L, tc, out_dtype):
    b, s, two_f = up3.shape
    f = two_f // 2
    conv_k = cw.shape[0]
    assert f % tc == 0
    nj = f // tc
    from_state = prev_state is not None
    if from_state:
        assert L == s
        prev = prev_state
        gp_spec = pl.BlockSpec((nb, conv_k - 1, tc), lambda n, i, j: (n, 0, j))
        vp_spec = pl.BlockSpec((nb, conv_k - 1, tc), lambda n, i, j: (n, 0, j + nj))
    else:
        assert L % CONV_HALO == 0
        prev = up3
        r = L // CONV_HALO
        gp_spec = pl.BlockSpec((nb, CONV_HALO, tc), lambda n, i, j: (n, jnp.maximum(i * r - 1, 0), j))
        vp_spec = pl.BlockSpec((nb, CONV_HALO, tc), lambda n, i, j: (n, jnp.maximum(i * r - 1, 0), j + nj))
    cb2 = cb.reshape(1, two_f)
    return pl.pallas_call(
        functools.partial(_ffn_mid_kernel, L=L, from_state=from_state, conv_k=conv_k),
        grid=(b // nb, s // L, nj),
        in_specs=[
            pl.BlockSpec((nb, L, tc), lambda n, i, j: (n, i, j)),
            pl.BlockSpec((nb, L, tc), lambda n, i, j: (n, i, j + nj)),
            gp_spec, vp_spec,
            pl.BlockSpec((conv_k, tc), lambda n, i, j: (0, j)),
            pl.BlockSpec((conv_k, tc), lambda n, i, j: (0, j + nj)),
            pl.BlockSpec((1, tc), lambda n, i, j: (0, j)),
            pl.BlockSpec((1, tc), lambda n, i, j: (0, j + nj)),
        ],
        out_specs=pl.BlockSpec((nb, L, tc), lambda n, i, j: (n, i, j)),
        out_shape=jax.ShapeDtypeStruct((b, s, f), out_dtype),
        scratch_shapes=[pltpu.VMEM((nb, CONV_HALO + L, tc), F32), pltpu.VMEM((nb, CONV_HALO + L, tc), F32)],
        compiler_params=_cparams("parallel", "arbitrary", "arbitrary"),
    )(up3, up3, prev, prev, cw, cw, cb2, cb2)


def _layer(x3, lw, dims, states, mem_kv, *, pos0, tiles):
    b, s, d = x3.shape
    t = b * s
    x2 = x3.reshape(t, d)
    G, R, P, N = dims["groups"], dims["heads_per_group"], dims["head_dim"], dims["n_state"]
    inner = G * R * P
    conv_dim = inner + 2 * G * N
    z_col, u_col, xbc_col, gp_col, gs_col = 0, inner, inner + d, inner + d + conv_dim, inner + 2 * d + conv_dim
    n_main = gs_col + d
    tm = tiles["tm"]

    main, dt = norm_matmul(x2, lw["norm_mix"], lw["w_main"], lw["w_dt"], tm=tm, tn=tiles["tn"])
    main3 = main.reshape(b, s, n_main)
    dt3 = dt.reshape(b, s, LANES)

    if states is None:
        pool_prev = conv_prev = ssm_prev = ffn_prev = None
    else:
        pool_prev, conv_prev, ssm_prev, ffn_prev = states
        ssm_prev = ssm_prev.reshape(b, G, R * P, N)

    gated_pool = pool_branch(main3, pool_prev, lw["w_pool_group"], lw["pool_scale"], lw["w_pool_out"],
                             u_col=u_col, gate_col=gp_col, nb=tiles["pool_nb"], L=tiles["pool_L"], pos0=pos0)
    y, ssm_new = ssd_branch(main3, dt3, conv_prev, ssm_prev, lw["ssm_conv_w"], lw["ssm_conv_b"],
                            lw["dt_bias"], lw["a_log"], lw["d_exp"], lw["ssm_norm"], lw["expand"],
                            xbc_col=xbc_col, z_col=z_col, q=tiles["ssd_q"], n_groups=G, heads_per_group=R,
                            head_dim=P, n_state=N, out_dtype=tiles["act_dtype"])
    merged = matmul(y.reshape(t, inner), lw["w_ssm_out"],
                    [(gated_pool.reshape(t, d), 0), (main, gs_col)], _ep_gate_merge,
                    tm=tm, tn=tiles["tn_d"], out_dtype=BF16)
    x2 = matmul(merged, lw["w_out"], [(x2, 0)], _ep_residual, tm=tm, tn=tiles["tn_d"], out_dtype=F32)

    qm = norm_matmul(x2, lw["norm_mem_q"], lw["w_mem_q"], tm=tm, tn=tiles["tn_d"], out_dtype=tiles["act_dtype"])
    k3, v3 = mem_kv
    o = cross_attention(qm.reshape(b, s, d), k3, v3, tq=tiles["attn_tq"], n_heads=dims["mem_heads"],
                        out_dtype=tiles["act_dtype"])
    x2 = matmul(o.reshape(t, d), lw["w_mem_o"], [(x2, 0)], _ep_residual, tm=tm, tn=tiles["tn_d"], out_dtype=F32)

    up = norm_matmul(x2, lw["norm_ffn"], lw["w_ffn_up"], tm=tm, tn=tiles["tn"])
    up3 = up.reshape(b, s, up.shape[1])
    act = ffn_mid(up3, ffn_prev, lw["ffn_conv_w"], lw["ffn_conv_b"], nb=tiles["ffn_nb"], L=tiles["ffn_L"],
                  tc=tiles["ffn_tc"], out_dtype=tiles["act_dtype"])
    x2 = matmul(act.reshape(t, act.shape[2]), lw["w_ffn_down"], [(x2, 0)], _ep_residual,
                tm=tm, tn=tiles["tn_d"], out_dtype=F32)

    u3 = main3[:, :, u_col:u_col + d]
    xbc3 = main3[:, :, xbc_col:xbc_col + conv_dim]
    hp, hc, hf = POOL_HALO - 1, lw["ssm_conv_w"].shape[0] - 1, lw["ffn_conv_w"].shape[0] - 1
    if states is None:
        pool_new, conv_new, ffn_new = u3[:, s - hp:], xbc3[:, s - hc:], up3[:, s - hf:]
    else:
        pool_new = jnp.concatenate([pool_prev, u3], axis=1)[:, s:]
        conv_new = jnp.concatenate([conv_prev, xbc3], axis=1)[:, s:]
        ffn_new = jnp.concatenate([ffn_prev, up3], axis=1)[:, s:]
    return x2.reshape(b, s, d), pool_new, conv_new, ssm_new.reshape(b, G * R, P, N), ffn_new


def kernel(x_prompt, x_sample, state_pool, state_ssm_conv, state_ssm, state_ffn_conv, cache_mem_k, cache_mem_v,
           mem_prompt, norm_mix, w_in, w_pool_group, pool_scale, w_pool_out, ssm_conv_w, ssm_conv_b, ssm_dt_bias,
           ssm_a_log, ssm_d, ssm_norm, w_ssm_out, w_out, norm_mem_q, w_mem_q, w_mem_o, norm_mem_kv, w_mem_k,
           w_mem_v, norm_ffn, w_ffn_up, ffn_conv_w, ffn_conv_b, w_ffn_down, norm_final):
    depth = w_in.shape[0]
    bp, sp, d = x_prompt.shape
    bs, ss, _ = x_sample.shape
    n_heads = ssm_d.shape[1]
    inner = w_ssm_out.shape[1]
    head_dim = inner // n_heads
    n_state = state_ssm.shape[-1]
    conv_dim = ssm_conv_w.shape[2]
    n_groups = (conv_dim - inner) // (2 * n_state)
    mem_heads = cache_mem_k.shape[3]
    mem_len = mem_prompt.shape[1]
    dims = dict(groups=n_groups, heads_per_group=n_heads // n_groups, head_dim=head_dim, n_state=n_state,
                mem_heads=mem_heads)
    assert n_heads <= LANES

    c_u, c_z, c_x, c_dt, c_gp = d, d + inner, d + inner + conv_dim, d + inner + conv_dim + n_heads, 2 * d + inner + conv_dim + n_heads
    head_of_col = jnp.arange(inner, dtype=jnp.int32) // head_dim
    expand = (jnp.arange(LANES, dtype=jnp.int32)[:, None] == head_of_col[None, :]).astype(BF16)

    def pad_heads(v):
        return jnp.pad(v, (0, LANES - n_heads)).reshape(1, LANES)

    tiles_p = dict(tm=512, tn=512, tn_d=512, pool_nb=1, pool_L=512, ssd_q=128, attn_tq=512, ffn_nb=1, ffn_L=512,
                   ffn_tc=512, act_dtype=BF16)
    tiles_s = dict(tm=512, tn=512, tn_d=512, pool_nb=32, pool_L=ss, ssd_q=ss, attn_tq=ss, ffn_nb=64, ffn_L=ss,
                   ffn_tc=512, act_dtype=F32)

    yp, ys = x_prompt, x_sample
    outs = [[] for _ in range(10)]
    for i in range(depth):
        wi = w_in[i]
        lw = dict(
            norm_mix=norm_mix[i],
            w_main=jnp.concatenate([wi[:, c_u:c_z], wi[:, :c_u], wi[:, c_z:c_x], wi[:, c_dt:]], axis=1).astype(BF16),
            w_dt=jnp.pad(wi[:, c_x:c_dt], ((0, 0), (0, LANES - n_heads))).astype(BF16),
            w_pool_group=w_pool_group[i].astype(BF16), pool_scale=pool_scale[i],
            w_pool_out=w_pool_out[i].astype(BF16),
            ssm_conv_w=ssm_conv_w[i], ssm_conv_b=ssm_conv_b[i],
            dt_bias=pad_heads(ssm_dt_bias[i]), a_log=pad_heads(ssm_a_log[i]),
            d_exp=jnp.repeat(ssm_d[i], head_dim).reshape(1, inner), ssm_norm=ssm_norm[i], expand=expand,
            w_ssm_out=w_ssm_out[i].astype(BF16), w_out=w_out[i].astype(BF16),
            norm_mem_q=norm_mem_q[i], w_mem_q=w_mem_q[i].astype(BF16), w_mem_o=w_mem_o[i].astype(BF16),
            norm_ffn=norm_ffn[i], w_ffn_up=w_ffn_up[i].astype(BF16),
            ffn_conv_w=ffn_conv_w[i], ffn_conv_b=ffn_conv_b[i], w_ffn_down=w_ffn_down[i].astype(BF16),
        )
        mem2 = mem_prompt.reshape(bp * mem_len, d)
        k_i = norm_matmul(mem2, norm_mem_kv[i], w_mem_k[i].astype(BF16), tm=512, tn=512)
        v_i = norm_matmul(mem2, norm_mem_kv[i], w_mem_v[i].astype(BF16), tm=512, tn=512)
        kv_p = (k_i.reshape(bp, mem_len, d), v_i.reshape(bp, mem_len, d))
        kv_s = (cache_mem_k[i].reshape(bs, mem_len, d), cache_mem_v[i].reshape(bs, mem_len, d))

        yp, a0, a1, a2, a3 = _layer(yp, lw, dims, None, kv_p, pos0=0, tiles=tiles_p)
        ys, b0, b1, b2, b3 = _layer(ys, lw, dims, (state_pool[i], state_ssm_conv[i], state_ssm[i], state_ffn_conv[i]),
                                    kv_s, pos0=PAST_LEN, tiles=tiles_s)
        for lst, val in zip(outs, (a0, b0, a1, b1, a2, b2, a3, b3,
                                   k_i.reshape(bp, mem_len, mem_heads, d // mem_heads),
                                   v_i.reshape(bp, mem_len, mem_heads, d // mem_heads))):
            lst.append(val)

    y_prompt = rmsnorm(yp.reshape(bp * sp, d), norm_final, tm=512).reshape(bp, sp, d)
    y_sample = rmsnorm(ys.reshape(bs * ss, d), norm_final, tm=512).reshape(bs, ss, d)
    return (y_prompt, y_sample) + tuple(jnp.stack(lst) for lst in outs)
```

```python
import functools
import math

import jax
import jax.numpy as jnp
from jax import lax
from jax.experimental import pallas as pl
from jax.experimental.pallas import tpu as pltpu

F32 = jnp.float32
BF16 = jnp.bfloat16
EPS = 1e-6

LANES = 128
SUBLANES = 8
VMEM_LIMIT = 56 * 1024 * 1024

PAST_LEN = 16384
POOL_WINDOWS = (2, 4, 8, 16)
POOL_HALO = 16
CONV_HALO = SUBLANES


def _cparams(*sem):
    return pltpu.CompilerParams(dimension_semantics=sem, vmem_limit_bytes=VMEM_LIMIT)


def _silu(x):
    h = 0.5 * x
    return h + h * jnp.tanh(h)


def _causal_conv(ext, w_ref, rows, axis):
    conv_k = w_ref.shape[0]
    tail = (slice(None),) * axis + (slice(CONV_HALO, CONV_HALO + rows),)
    acc = ext[tail] * w_ref[conv_k - 1:conv_k, :]
    for j in range(conv_k - 1):
        acc = acc + pltpu.roll(ext, conv_k - 1 - j, axis=axis)[tail] * w_ref[j:j + 1, :]
    return acc


def _norm_matmul_kernel(x_ref, g_ref, w_ref, *rest, has_aux):
    if has_aux:
        wa_ref, o_ref, oa_ref, xn_ref = rest
    else:
        o_ref, xn_ref = rest

    @pl.when(pl.program_id(1) == 0)
    def _():
        x = x_ref[...]
        ms = jnp.mean(x * x, axis=-1, keepdims=True)
        xn = (x * lax.rsqrt(ms + EPS) * g_ref[...]).astype(BF16)
        xn_ref[...] = xn
        if has_aux:
            oa_ref[...] = jnp.dot(xn, wa_ref[...], preferred_element_type=F32)

    o_ref[...] = jnp.dot(xn_ref[...], w_ref[...], preferred_element_type=F32).astype(o_ref.dtype)


def norm_matmul(x, g, w, w_aux=None, *, tm, tn, out_dtype=F32, name):
    t, k = x.shape
    n = w.shape[1]
    assert t % tm == 0 and n % tn == 0
    has_aux = w_aux is not None
    in_specs = [
        pl.BlockSpec((tm, k), lambda i, j: (i, 0)),
        pl.BlockSpec((1, k), lambda i, j: (0, 0)),
        pl.BlockSpec((k, tn), lambda i, j: (0, j)),
    ]
    args = [x, g.reshape(1, k), w]
    out_shape = [jax.ShapeDtypeStruct((t, n), out_dtype)]
    out_specs = [pl.BlockSpec((tm, tn), lambda i, j: (i, j))]
    if has_aux:
        na = w_aux.shape[1]
        in_specs.append(pl.BlockSpec((k, na), lambda i, j: (0, 0)))
        args.append(w_aux)
        out_shape.append(jax.ShapeDtypeStruct((t, na), F32))
        out_specs.append(pl.BlockSpec((tm, na), lambda i, j: (i, 0)))
    res = pl.pallas_call(
        functools.partial(_norm_matmul_kernel, has_aux=has_aux),
        grid=(t // tm, n // tn),
        in_specs=in_specs,
        out_specs=out_specs,
        out_shape=out_shape,
        scratch_shapes=[pltpu.VMEM((tm, k), BF16)],
        compiler_params=_cparams("parallel", "arbitrary"),
        name=name,
    )(*args)
    return res if has_aux else res[0]


def _matmul_kernel(a_ref, w_ref, *rest, epilogue):
    *extra, o_ref = rest
    acc = jnp.dot(a_ref[...].astype(BF16), w_ref[...], preferred_element_type=F32)
    o_ref[...] = epilogue(acc, *[e[...] for e in extra]).astype(o_ref.dtype)


def _ep_residual(acc, r):
    return r + acc


def _ep_gate_merge(acc, gated_pool, g_ssm):
    return gated_pool + jax.nn.sigmoid(g_ssm) * acc


def matmul(a, w, extras, epilogue, *, tm, tn, out_dtype, name):
    t, k = a.shape
    n = w.shape[1]
    assert t % tm == 0 and n % tn == 0
    in_specs = [
        pl.BlockSpec((tm, k), lambda i, j: (i, 0)),
        pl.BlockSpec((k, tn), lambda i, j: (0, j)),
    ]
    args = [a, w]
    for arr, off in extras:
        assert off % tn == 0
        ob = off // tn
        in_specs.append(pl.BlockSpec((tm, tn), lambda i, j, ob=ob: (i, j + ob)))
        args.append(arr)
    return pl.pallas_call(
        functools.partial(_matmul_kernel, epilogue=epilogue),
        grid=(t // tm, n // tn),
        in_specs=in_specs,
        out_specs=pl.BlockSpec((tm, tn), lambda i, j: (i, j)),
        out_shape=jax.ShapeDtypeStruct((t, n), out_dtype),
        compiler_params=_cparams("parallel", "arbitrary"),
        name=name,
    )(*args)


def _rmsnorm_kernel(x_ref, g_ref, o_ref):
    x = x_ref[...]
    ms = jnp.mean(x * x, axis=-1, keepdims=True)
    o_ref[...] = x * lax.rsqrt(ms + EPS) * g_ref[...]


def rmsnorm(x, g, *, tm):
    t, k = x.shape
    return pl.pallas_call(
        _rmsnorm_kernel,
        grid=(t // tm,),
        in_specs=[pl.BlockSpec((tm, k), lambda i: (i, 0)), pl.BlockSpec((1, k), lambda i: (0, 0))],
        out_specs=pl.BlockSpec((tm, k), lambda i: (i, 0)),
        out_shape=jax.ShapeDtypeStruct((t, k), F32),
        compiler_params=_cparams("parallel"),
    )(x, g.reshape(1, k))


def _pool_kernel(cur_ref, prev_ref, gate_ref, wg_ref, scale_ref, wo_ref, o_ref, ext_ref, pooled_ref,
                 *, nb, L, from_state, pos0):
    w_ch = cur_ref.shape[-1]
    gdim = w_ch // len(POOL_WINDOWS)
    if from_state:
        ext_ref[:, 1:POOL_HALO, :] = prev_ref[0]
        pos_start = pos0
    else:
        i = pl.program_id(1)
        ext_ref[:, 0:POOL_HALO, :] = jnp.where(i == 0, 0.0, prev_ref[...])
        pos_start = pos0 + i * L
    ext_ref[:, POOL_HALO:POOL_HALO + L, :] = cur_ref[...]

    pos = pos_start + lax.broadcasted_iota(jnp.int32, (1, L, gdim), 1)
    for k, win in enumerate(POOL_WINDOWS):
        cs = slice(k * gdim, (k + 1) * gdim)
        cur = ext_ref[:, POOL_HALO:POOL_HALO + L, cs]
        acc = cur
        for j in range(1, win):
            acc = acc + ext_ref[:, POOL_HALO - j:POOL_HALO - j + L, cs]
        count = jnp.minimum(pos + 1, win).astype(F32)
        diff = (acc / count - cur).reshape(nb * L, gdim).astype(BF16)
        mixed = jnp.dot(diff, wg_ref[k], preferred_element_type=F32)
        pooled_ref[:, cs] = (mixed * scale_ref[:, cs]).astype(BF16)
    out_pool = jnp.dot(pooled_ref[...], wo_ref[...], preferred_element_type=F32)
    gate = jax.nn.sigmoid(gate_ref[...].reshape(nb * L, w_ch))
    o_ref[...] = (gate * out_pool).reshape(nb, L, w_ch)


def pool_branch(main3, prev_state, wg, scale, wo, *, layer, u_col, gate_col, nb, L, pos0):
    b, s, _ = main3.shape
    w_ch = wo.shape[0]
    ub, gb = u_col // w_ch, gate_col // w_ch
    from_state = prev_state is not None
    if from_state:
        assert L == s
        prev = prev_state
        prev_spec = pl.BlockSpec((1, nb, POOL_HALO - 1, w_ch), lambda n, i: (layer, n, 0, 0))
    else:
        assert nb == 1 and L % POOL_HALO == 0
        prev = main3
        r = L // POOL_HALO
        prev_spec = pl.BlockSpec((1, POOL_HALO, w_ch), lambda n, i: (n, jnp.maximum(i * r - 1, 0), ub))
    return pl.pallas_call(
        functools.partial(_pool_kernel, nb=nb, L=L, from_state=from_state, pos0=pos0),
        grid=(b // nb, s // L),
        in_specs=[
            pl.BlockSpec((nb, L, w_ch), lambda n, i: (n, i, ub)),
            prev_spec,
            pl.BlockSpec((nb, L, w_ch), lambda n, i: (n, i, gb)),
            pl.BlockSpec(wg.shape, lambda n, i: (0, 0, 0)),
            pl.BlockSpec((1, w_ch), lambda n, i: (0, 0)),
            pl.BlockSpec(wo.shape, lambda n, i: (0, 0)),
        ],
        out_specs=pl.BlockSpec((nb, L, w_ch), lambda n, i: (n, i, 0)),
        out_shape=jax.ShapeDtypeStruct((b, s, w_ch), F32),
        scratch_shapes=[pltpu.VMEM((nb, POOL_HALO + L, w_ch), F32), pltpu.VMEM((nb * L, w_ch), BF16)],
        compiler_params=_cparams("parallel", "arbitrary"),
        name="pool_state" if from_state else "pool_scan",
    )(main3, prev, main3, wg, scale.reshape(1, w_ch), wo)


def _cumsum_rows(x):
    q = x.shape[0]
    row = lax.broadcasted_iota(jnp.int32, x.shape, 0)
    k = 1
    while k < q:
        x = x + jnp.where(row >= k, pltpu.roll(x, k, axis=0), 0.0)
        k *= 2
    return x


def _expand_heads(vals, e_ref):
    q = vals[0].shape[0]
    pieces = []
    for v in vals:
        hi = v.astype(BF16).astype(F32)
        r1 = v - hi
        mid = r1.astype(BF16).astype(F32)
        pieces += [hi, mid, r1 - mid]
    lhs = jnp.concatenate(pieces, axis=0).astype(BF16)
    out = jnp.dot(lhs, e_ref[...], preferred_element_type=F32)
    return [out[(3 * i) * q:(3 * i + 1) * q] + out[(3 * i + 1) * q:(3 * i + 2) * q] + out[(3 * i + 2) * q:(3 * i + 3) * q]
            for i in range(len(vals))]


def _ssd_kernel(*refs, q, n_groups, heads_per_group, head_dim, n_state, from_state, conv_k):
    if from_state:
        (xbc_ref, prev_ref, z_ref, dt_ref, h0_ref, cw_ref, cb_ref, dtb_ref, alog_ref, dexp_ref, norm_ref, e_ref,
         *_, y_ref, hout_ref, ext_ref, act_ref, h_ref, yacc_ref) = refs
    else:
        (xbc_ref, prev_ref, z_ref, dt_ref, cw_ref, cb_ref, dtb_ref, alog_ref, dexp_ref, norm_ref, e_ref,
         y_ref, hout_ref, ext_ref, act_ref, h_ref, yacc_ref) = refs
    c = pl.program_id(1)
    gw = heads_per_group * head_dim
    inner = n_groups * gw
    hist = conv_k - 1

    if from_state:
        h_in, h_out = h0_ref.at[0, 0], hout_ref.at[0, 0]
        ext_ref[0:CONV_HALO, :] = jnp.zeros((CONV_HALO, ext_ref.shape[1]), F32)
        ext_ref[CONV_HALO - hist:CONV_HALO, :] = prev_ref[0, 0]
    else:
        h_in = h_out = h_ref

        @pl.when(c == 0)
        def _():
            h_ref[...] = jnp.zeros_like(h_ref)
        ext_ref[0:CONV_HALO, :] = jnp.where(c == 0, 0.0, prev_ref[0])
    ext_ref[CONV_HALO:CONV_HALO + q, :] = xbc_ref[0]

    act_ref[...] = _silu(_causal_conv(ext_ref[...], cw_ref, q, 0) + cb_ref[...])

    dt = jax.nn.softplus(dt_ref[0] + dtb_ref[...])
    la_cs = _cumsum_rows(dt * (-jnp.exp(alog_ref[...])))
    la_cs_t = la_cs.T
    dt_x, la_x = _expand_heads([dt, la_cs], e_ref)
    from_start_x = jnp.exp(la_x)
    to_end_x = jnp.exp(la_x[q - 1:q, :] - la_x)
    chunk_decay = jnp.exp(la_cs[q - 1:q, :])

    xs = act_ref[:, 0:inner]
    xd = xs * dt_x
    xdw = xd * to_end_x
    yacc_ref[...] = dexp_ref[...] * xs

    tri = lax.broadcasted_iota(jnp.int32, (q, q), 0) >= lax.broadcasted_iota(jnp.int32, (q, q), 1)
    assert LANES % head_dim == 0
    hpl = LANES // head_dim
    lane_head = lax.broadcasted_iota(jnp.int32, (q, LANES), 1) // head_dim
    bms, cbs = [], []
    for g in range(n_groups):
        gs = slice(g * gw, (g + 1) * gw)
        bm = act_ref[:, inner + g * n_state:inner + (g + 1) * n_state].astype(BF16)
        cm = act_ref[:, inner + (n_groups + g) * n_state:inner + (n_groups + g + 1) * n_state].astype(BF16)
        bms.append(bm)
        cbs.append(lax.dot_general(cm, bm, (((1,), (1,)), ((), ())), preferred_element_type=F32))
        y_off = lax.dot_general(cm, h_in[g].astype(BF16), (((1,), (1,)), ((), ())),
                                preferred_element_type=F32)
        yacc_ref[:, gs] += y_off * from_start_x[:, gs]
    for g in range(n_groups):
        cb = cbs[g]
        for lt in range(gw // LANES):
            ls = slice(g * gw + lt * LANES, g * gw + (lt + 1) * LANES)
            xd_t = xd[:, ls]
            y_t = None
            for k in range(hpl):
                hd = (g * gw + lt * LANES) // head_dim + k
                seg = la_cs[:, hd:hd + 1] - la_cs_t[hd:hd + 1, :]
                m = (cb * jnp.exp(jnp.where(tri, seg, -jnp.inf))).astype(BF16)
                rhs = jnp.where(lane_head == k, xd_t, 0.0).astype(BF16)
                part = jnp.dot(m, rhs, preferred_element_type=F32)
                y_t = part if y_t is None else y_t + part
            yacc_ref[:, ls] += y_t
    for g in range(n_groups):
        gs = slice(g * gw, (g + 1) * gw)
        s_new = lax.dot_general(xdw[:, gs].astype(BF16), bms[g], (((0,), (0,)), ((), ())),
                                preferred_element_type=F32)
        for r in range(heads_per_group):
            hd = g * heads_per_group + r
            rows = slice(r * head_dim, (r + 1) * head_dim)
            h_out[g, rows, :] = h_in[g, rows, :] * chunk_decay[:, hd:hd + 1] + s_new[rows, :]

    y = yacc_ref[...] * _silu(z_ref[0])
    for g in range(n_groups):
        gs = slice(g * gw, (g + 1) * gw)
        yg = y[:, gs]
        ms = jnp.mean(yg * yg, axis=-1, keepdims=True)
        y_ref[0, :, gs] = (yg * lax.rsqrt(ms + EPS) * norm_ref[:, gs]).astype(y_ref.dtype)

    if not from_state:
        @pl.when(c == pl.num_programs(1) - 1)
        def _():
            hout_ref[0] = h_ref[...]


def ssd_branch(main3, dt3, conv_state, ssm_state, state_out, cw, cb, dt_bias, a_log, d_exp, norm, expand,
               *, layer, xbc_col, z_col, q, n_groups, heads_per_group, head_dim, n_state, out_dtype):
    b, s, _ = main3.shape
    gw = heads_per_group * head_dim
    inner = n_groups * gw
    conv_dim = inner + 2 * n_groups * n_state
    conv_k = cw.shape[0]
    from_state = ssm_state is not None
    xb, zb = xbc_col // conv_dim, z_col // inner
    assert xbc_col % conv_dim == 0 and z_col % inner == 0

    def const(shape):
        nd = len(shape)
        return pl.BlockSpec(shape, lambda n, c: (0,) * nd)

    in_specs = [pl.BlockSpec((1, q, conv_dim), lambda n, c: (n, c, xb))]
    args = [main3]
    if from_state:
        assert q == s
        in_specs.append(pl.BlockSpec((1, 1, conv_k - 1, conv_dim), lambda n, c: (layer, n, 0, 0)))
        args.append(conv_state)
    else:
        assert q % CONV_HALO == 0
        r = q // CONV_HALO
        in_specs.append(pl.BlockSpec((1, CONV_HALO, conv_dim), lambda n, c: (n, jnp.maximum(c * r - 1, 0), xb)))
        args.append(main3)
    in_specs += [pl.BlockSpec((1, q, inner), lambda n, c: (n, c, zb)),
                 pl.BlockSpec((1, q, LANES), lambda n, c: (n, c, 0))]
    args += [main3, dt3]
    if from_state:
        in_specs.append(pl.BlockSpec((1, 1, n_groups, gw, n_state), lambda n, c: (layer, n, 0, 0, 0)))
        args.append(ssm_state)
    small = [cw, cb.reshape(1, conv_dim), dt_bias, a_log, d_exp, norm.reshape(1, inner), expand]
    in_specs += [const(a.shape) for a in small]
    args += small
    aliases = {}
    if from_state:
        state_spec = pl.BlockSpec((1, 1, n_groups, gw, n_state), lambda n, c: (layer, n, 0, 0, 0))
        state_shape = jax.ShapeDtypeStruct(ssm_state.shape, F32)
        if state_out is not None:
            aliases = {len(args): 1}
            in_specs.append(pl.BlockSpec(memory_space=pl.ANY))
            args.append(state_out)
    else:
        state_spec = pl.BlockSpec((1, n_groups, gw, n_state), lambda n, c: (n, 0, 0, 0))
        state_shape = jax.ShapeDtypeStruct((b, n_groups, gw, n_state), F32)
    return pl.pallas_call(
        functools.partial(_ssd_kernel, q=q, n_groups=n_groups, heads_per_group=heads_per_group,
                          head_dim=head_dim, n_state=n_state, from_state=from_state, conv_k=conv_k),
        grid=(b, s // q),
        in_specs=in_specs,
        out_specs=[pl.BlockSpec((1, q, inner), lambda n, c: (n, c, 0)), state_spec],
        out_shape=[jax.ShapeDtypeStruct((b, s, inner), out_dtype), state_shape],
        input_output_aliases=aliases,
        scratch_shapes=[pltpu.VMEM((CONV_HALO + q, conv_dim), F32),
                        pltpu.VMEM((q, conv_dim), F32),
                        pltpu.VMEM((1, SUBLANES, LANES) if from_state else (n_groups, gw, n_state), F32),
                        pltpu.VMEM((q, inner), F32)],
        compiler_params=_cparams("parallel", "arbitrary"),
        name="ssd_state" if from_state else "ssd_scan",
    )(*args)


def _attn_kernel(q_ref, k_ref, v_ref, o_ref, *, n_heads, nb):
    d = q_ref.shape[-1] // n_heads
    inv = 1.0 / math.sqrt(d)
    units = [(n, slice(h * d, (h + 1) * d)) for n in range(nb) for h in range(n_heads)]
    scores = []
    for n, hs in units:
        qh = q_ref[n, :, hs].astype(BF16)
        kh = k_ref[0, n, :, hs].astype(BF16)
        scores.append(lax.dot_general(qh, kh, (((1,), (1,)), ((), ())), preferred_element_type=F32) * inv)
    probs = []
    for s in scores:
        e = jnp.exp(s - jnp.max(s, axis=-1, keepdims=True))
        probs.append((e / jnp.sum(e, axis=-1, keepdims=True)).astype(BF16))
    for (n, hs), p in zip(units, probs):
        vh = v_ref[0, n, :, hs].astype(BF16)
        o_ref[n, :, hs] = jnp.dot(p, vh, preferred_element_type=F32).astype(o_ref.dtype)


def _attn_few_kernel(q_ref, k_ref, v_ref, o_ref, *, n_heads, nb):
    tq = q_ref.shape[1]
    d = q_ref.shape[-1] // n_heads
    inv = 1.0 / math.sqrt(d)
    pad = jnp.zeros((LANES - tq, d), F32)
    units = [(n, slice(h * d, (h + 1) * d)) for n in range(nb) for h in range(n_heads)]
    scores_t = []
    for n, hs in units:
        qh = jnp.concatenate([q_ref[n, :, hs].astype(F32), pad], axis=0).astype(BF16)
        kh = k_ref[0, n, :, hs].astype(BF16)
        scores_t.append(lax.dot_general(kh, qh, (((1,), (1,)), ((), ())), preferred_element_type=F32) * inv)
    probs = []
    for st in scores_t:
        e = jnp.exp(st - jnp.max(st, axis=0, keepdims=True))
        pt = e / jnp.sum(e, axis=0, keepdims=True)
        probs.append(pt.T[0:tq, :].astype(BF16))
    for (n, hs), p in zip(units, probs):
        vh = v_ref[0, n, :, hs].astype(BF16)
        o_ref[n, :, hs] = jnp.dot(p, vh, preferred_element_type=F32).astype(o_ref.dtype)


def cross_attention(q3, k4, v4, *, layer, nb, tq, n_heads, out_dtype):
    b, s, dm = q3.shape
    m = k4.shape[2]
    body = _attn_few_kernel if tq < LANES else _attn_kernel
    return pl.pallas_call(
        functools.partial(body, n_heads=n_heads, nb=nb),
        grid=(b // nb, s // tq),
        in_specs=[pl.BlockSpec((nb, tq, dm), lambda n, i: (n, i, 0)),
                  pl.BlockSpec((1, nb, m, dm), lambda n, i: (layer, n, 0, 0)),
                  pl.BlockSpec((1, nb, m, dm), lambda n, i: (layer, n, 0, 0))],
        out_specs=pl.BlockSpec((nb, tq, dm), lambda n, i: (n, i, 0)),
        out_shape=jax.ShapeDtypeStruct((b, s, dm), out_dtype),
        compiler_params=_cparams("parallel", "arbitrary"),
        name="cross_attention",
    )(q3, k4, v4)


def _ffn_mid_kernel(g_ref, v_ref, gp_ref, vp_ref, gw_ref, vw_ref, gb_ref, vb_ref, o_ref, gext_ref, vext_ref,
                    *, L, from_state, conv_k):
    hist = conv_k - 1
    i = pl.program_id(1)

    def conv(cur_ref, prev_ref, w_ref, b_ref, ext_ref):
        if from_state:
            ext_ref[:, 0:CONV_HALO, :] = jnp.zeros((ext_ref.shape[0], CONV_HALO, ext_ref.shape[2]), F32)
            ext_ref[:, CONV_HALO - hist:CONV_HALO, :] = prev_ref[0]
        else:
            ext_ref[:, 0:CONV_HALO, :] = jnp.where(i == 0, 0.0, prev_ref[...])
        ext_ref[:, CONV_HALO:CONV_HALO + L, :] = cur_ref[...]
        return _causal_conv(ext_ref[...], w_ref, L, 1) + b_ref[...]

    g = conv(g_ref, gp_ref, gw_ref, gb_ref, gext_ref)
    v = conv(v_ref, vp_ref, vw_ref, vb_ref, vext_ref)
    o_ref[...] = (_silu(g) * v).astype(o_ref.dtype)


def ffn_mid(up3, prev_state, cw, cb, *, layer, nb, L, tc, out_dtype):
    b, s, two_f = up3.shape
    f = two_f // 2
    conv_k = cw.shape[0]
    assert f % tc == 0
    nj = f // tc
    from_state = prev_state is not None
    if from_state:
        assert L == s
        prev = prev_state
        gp_spec = pl.BlockSpec((1, nb, conv_k - 1, tc), lambda n, i, j: (layer, n, 0, j))
        vp_spec = pl.BlockSpec((1, nb, conv_k - 1, tc), lambda n, i, j: (layer, n, 0, j + nj))
    else:
        assert L % CONV_HALO == 0
        prev = up3
        r = L // CONV_HALO
        gp_spec = pl.BlockSpec((nb, CONV_HALO, tc), lambda n, i, j: (n, jnp.maximum(i * r - 1, 0), j))
        vp_spec = pl.BlockSpec((nb, CONV_HALO, tc), lambda n, i, j: (n, jnp.maximum(i * r - 1, 0), j + nj))
    cb2 = cb.reshape(1, two_f)
    return pl.pallas_call(
        functools.partial(_ffn_mid_kernel, L=L, from_state=from_state, conv_k=conv_k),
        grid=(b // nb, s // L, nj),
        in_specs=[
            pl.BlockSpec((nb, L, tc), lambda n, i, j: (n, i, j)),
            pl.BlockSpec((nb, L, tc), lambda n, i, j: (n, i, j + nj)),
            gp_spec, vp_spec,
            pl.BlockSpec((conv_k, tc), lambda n, i, j: (0, j)),
            pl.BlockSpec((conv_k, tc), lambda n, i, j: (0, j + nj)),
            pl.BlockSpec((1, tc), lambda n, i, j: (0, j)),
            pl.BlockSpec((1, tc), lambda n, i, j: (0, j + nj)),
        ],
        out_specs=pl.BlockSpec((nb, L, tc), lambda n, i, j: (n, i, j)),
        out_shape=jax.ShapeDtypeStruct((b, s, f), out_dtype),
        scratch_shapes=[pltpu.VMEM((nb, CONV_HALO + L, tc), F32), pltpu.VMEM((nb, CONV_HALO + L, tc), F32)],
        compiler_params=_cparams("parallel", "arbitrary", "arbitrary"),
        name="ffn_mid_state" if from_state else "ffn_mid_scan",
    )(up3, up3, prev, prev, cw, cw, cb2, cb2)


def _layer(x3, lw, dims, states, ssm_out, mem_kv, *, layer, kv_layer, pos0, tiles):
    b, s, d = x3.shape
    t = b * s
    x2 = x3.reshape(t, d)
    G, R, P, N = dims["groups"], dims["heads_per_group"], dims["head_dim"], dims["n_state"]
    inner = G * R * P
    conv_dim = inner + 2 * G * N
    z_col, u_col, xbc_col, gp_col, gs_col = 0, inner, inner + d, inner + d + conv_dim, inner + 2 * d + conv_dim
    n_main = gs_col + d
    tm, tn, tn_k = tiles["tm"], tiles["tn"], tiles["tn_long_k"]
    act_dtype = tiles["act_dtype"]

    main, dt = norm_matmul(x2, lw["norm_mix"], lw["w_main"], lw["w_dt"], tm=tm, tn=tn, name="in_proj")
    main3 = main.reshape(b, s, n_main)
    dt3 = dt.reshape(b, s, LANES)

    if states is None:
        pool_prev = conv_prev = ssm_prev = ffn_prev = None
    else:
        pool_prev, conv_prev, ssm_prev, ffn_prev = states

    gated_pool = pool_branch(main3, pool_prev, lw["w_pool_group"], lw["pool_scale"], lw["w_pool_out"], layer=layer,
                             u_col=u_col, gate_col=gp_col, nb=tiles["pool_nb"], L=tiles["pool_L"], pos0=pos0)
    y, ssm_new = ssd_branch(main3, dt3, conv_prev, ssm_prev, ssm_out, lw["ssm_conv_w"], lw["ssm_conv_b"],
                            lw["dt_bias"], lw["a_log"], lw["d_exp"], lw["ssm_norm"], lw["expand"], layer=layer,
                            xbc_col=xbc_col, z_col=z_col, q=tiles["ssd_q"], n_groups=G, heads_per_group=R,
                            head_dim=P, n_state=N, out_dtype=act_dtype)
    merged = matmul(y.reshape(t, inner), lw["w_ssm_out"],
                    [(gated_pool.reshape(t, d), 0), (main, gs_col)], _ep_gate_merge,
                    tm=tm, tn=tn_k, out_dtype=BF16, name="ssm_out_merge")
    x2 = matmul(merged, lw["w_out"], [(x2, 0)], _ep_residual, tm=tm, tn=tn, out_dtype=F32, name="mix_out")

    qm = norm_matmul(x2, lw["norm_mem_q"], lw["w_mem_q"], tm=tm, tn=tn, out_dtype=act_dtype, name="mem_q")
    k4, v4 = mem_kv
    o = cross_attention(qm.reshape(b, s, d), k4, v4, layer=kv_layer, nb=tiles["attn_nb"], tq=tiles["attn_tq"],
                        n_heads=dims["mem_heads"], out_dtype=act_dtype)
    x2 = matmul(o.reshape(t, d), lw["w_mem_o"], [(x2, 0)], _ep_residual, tm=tm, tn=tn, out_dtype=F32, name="mem_o")

    up = norm_matmul(x2, lw["norm_ffn"], lw["w_ffn_up"], tm=tm, tn=tn, name="ffn_up")
    up3 = up.reshape(b, s, up.shape[1])
    act = ffn_mid(up3, ffn_prev, lw["ffn_conv_w"], lw["ffn_conv_b"], layer=layer, nb=tiles["ffn_nb"],
                  L=tiles["ffn_L"], tc=tiles["ffn_tc"], out_dtype=act_dtype)
    x2 = matmul(act.reshape(t, act.shape[2]), lw["w_ffn_down"], [(x2, 0)], _ep_residual,
                tm=tiles["tm_down"], tn=tn_k, out_dtype=F32, name="ffn_down")

    u3 = main3[:, :, u_col:u_col + d]
    xbc3 = main3[:, :, xbc_col:xbc_col + conv_dim]
    hp, hc, hf = POOL_HALO - 1, lw["ssm_conv_w"].shape[0] - 1, lw["ffn_conv_w"].shape[0] - 1
    if states is None:
        pool_new, conv_new, ffn_new = u3[:, s - hp:], xbc3[:, s - hc:], up3[:, s - hf:]
    else:
        pool_new = jnp.concatenate([pool_prev[layer], u3], axis=1)[:, s:]
        conv_new = jnp.concatenate([conv_prev[layer], xbc3], axis=1)[:, s:]
        ffn_new = jnp.concatenate([ffn_prev[layer], up3], axis=1)[:, s:]
    return x2.reshape(b, s, d), pool_new, conv_new, ssm_new, ffn_new


def kernel(x_prompt, x_sample, state_pool, state_ssm_conv, state_ssm, state_ffn_conv, cache_mem_k, cache_mem_v,
           mem_prompt, norm_mix, w_in, w_pool_group, pool_scale, w_pool_out, ssm_conv_w, ssm_conv_b, ssm_dt_bias,
           ssm_a_log, ssm_d, ssm_norm, w_ssm_out, w_out, norm_mem_q, w_mem_q, w_mem_o, norm_mem_kv, w_mem_k,
           w_mem_v, norm_ffn, w_ffn_up, ffn_conv_w, ffn_conv_b, w_ffn_down, norm_final):
    depth = w_in.shape[0]
    bp, sp, d = x_prompt.shape
    bs, ss, _ = x_sample.shape
    n_heads = ssm_d.shape[1]
    inner = w_ssm_out.shape[1]
    head_dim = inner // n_heads
    n_state = state_ssm.shape[-1]
    conv_dim = ssm_conv_w.shape[2]
    n_groups = (conv_dim - inner) // (2 * n_state)
    mem_heads = cache_mem_k.shape[3]
    mem_len = mem_prompt.shape[1]
    dims = dict(groups=n_groups, heads_per_group=n_heads // n_groups, head_dim=head_dim, n_state=n_state,
                mem_heads=mem_heads)
    assert n_heads <= LANES

    c_u, c_z, c_x, c_dt, c_gp = d, d + inner, d + inner + conv_dim, d + inner + conv_dim + n_heads, 2 * d + inner + conv_dim + n_heads
    head_of_col = jnp.arange(inner, dtype=jnp.int32) // head_dim
    expand = (jnp.arange(LANES, dtype=jnp.int32)[:, None] == head_of_col[None, :]).astype(BF16)

    def pad_heads(v):
        return jnp.pad(v, (0, LANES - n_heads)).reshape(1, LANES)

    tiles_p = dict(tm=1024, tm_down=1024, tn=1024, tn_long_k=512, pool_nb=1, pool_L=512, ssd_q=128,
                   attn_nb=1, attn_tq=512, ffn_nb=1, ffn_L=512, ffn_tc=512, act_dtype=BF16)
    tiles_s = dict(tm=512, tm_down=512, tn=1024, tn_long_k=512, pool_nb=32, pool_L=ss, ssd_q=ss,
                   attn_nb=4, attn_tq=ss, ffn_nb=64, ffn_L=ss, ffn_tc=512, act_dtype=F32)

    ssm_states = state_ssm.reshape(depth, bs, n_groups, inner // n_groups, n_state)
    kv_s = (cache_mem_k.reshape(depth, bs, mem_len, d), cache_mem_v.reshape(depth, bs, mem_len, d))
    states_s = (state_pool, state_ssm_conv, ssm_states, state_ffn_conv)
    ssm_s_all = None

    yp, ys = x_prompt, x_sample
    outs = [[] for _ in range(9)]
    for i in range(depth):
        wi = w_in[i]
        lw = dict(
            norm_mix=norm_mix[i],
            w_main=jnp.concatenate([wi[:, c_u:c_z], wi[:, :c_u], wi[:, c_z:c_x], wi[:, c_dt:]], axis=1).astype(BF16),
            w_dt=jnp.pad(wi[:, c_x:c_dt], ((0, 0), (0, LANES - n_heads))).astype(BF16),
            w_pool_group=w_pool_group[i].astype(BF16), pool_scale=pool_scale[i],
            w_pool_out=w_pool_out[i].astype(BF16),
            ssm_conv_w=ssm_conv_w[i], ssm_conv_b=ssm_conv_b[i],
            dt_bias=pad_heads(ssm_dt_bias[i]), a_log=pad_heads(ssm_a_log[i]),
            d_exp=jnp.repeat(ssm_d[i], head_dim).reshape(1, inner), ssm_norm=ssm_norm[i], expand=expand,
            w_ssm_out=w_ssm_out[i].astype(BF16), w_out=w_out[i].astype(BF16),
            norm_mem_q=norm_mem_q[i], w_mem_q=w_mem_q[i].astype(BF16), w_mem_o=w_mem_o[i].astype(BF16),
            norm_ffn=norm_ffn[i], w_ffn_up=w_ffn_up[i].astype(BF16),
            ffn_conv_w=ffn_conv_w[i], ffn_conv_b=ffn_conv_b[i], w_ffn_down=w_ffn_down[i].astype(BF16),
        )
        mem2 = mem_prompt.reshape(bp * mem_len, d)
        k_i = norm_matmul(mem2, norm_mem_kv[i], w_mem_k[i].astype(BF16), tm=512, tn=1024, name="mem_k")
        v_i = norm_matmul(mem2, norm_mem_kv[i], w_mem_v[i].astype(BF16), tm=512, tn=1024, name="mem_v")
        kv_p = (k_i.reshape(1, bp, mem_len, d), v_i.reshape(1, bp, mem_len, d))

        yp, a0, a1, a2, a3 = _layer(yp, lw, dims, None, None, kv_p, layer=0, kv_layer=0, pos0=0, tiles=tiles_p)
        ys, b0, b1, ssm_s_all, b3 = _layer(ys, lw, dims, states_s, ssm_s_all, kv_s, layer=i, kv_layer=i,
                                           pos0=PAST_LEN, tiles=tiles_s)
        for lst, val in zip(outs, (a0, b0, a1, b1, a2.reshape(bp, n_heads, head_dim, n_state), a3, b3,
                                   k_i.reshape(bp, mem_len, mem_heads, d // mem_heads),
                                   v_i.reshape(bp, mem_len, mem_heads, d // mem_heads))):
            lst.append(val)

    y_prompt = rmsnorm(yp.reshape(bp * sp, d), norm_final, tm=512).reshape(bp, sp, d)
    y_sample = rmsnorm(ys.reshape(bs * ss, d), norm_final, tm=512).reshape(bs, ss, d)
    pool_p, pool_s, conv_p, conv_s, ssm_p, ffn_p, ffn_s, mk_p, mv_p = (jnp.stack(lst) for lst in outs)
    ssm_s = ssm_s_all.reshape(depth, bs, n_heads, head_dim, n_state)
    return (y_prompt, y_sample, pool_p, pool_s, conv_p, conv_s, ssm_p, ssm_s, ffn_p, ffn_s, mk_p, mv_p)
```

```python
import functools
import math

import jax
import jax.numpy as jnp
from jax import lax
from jax.experimental import pallas as pl
from jax.experimental.pallas import tpu as pltpu

F32 = jnp.float32
BF16 = jnp.bfloat16
EPS = 1e-6

LANES = 128
SUBLANES = 8
VMEM_LIMIT = 56 * 1024 * 1024

PAST_LEN = 16384
POOL_WINDOWS = (2, 4, 8, 16)
POOL_HALO = 16
CONV_HALO = SUBLANES


def _cparams(*sem):
    return pltpu.CompilerParams(dimension_semantics=sem, vmem_limit_bytes=VMEM_LIMIT)


def _silu(x):
    h = 0.5 * x
    return h + h * jnp.tanh(h)


def _causal_conv(ext, w_ref, rows, axis):
    conv_k = w_ref.shape[0]
    tail = (slice(None),) * axis + (slice(CONV_HALO, CONV_HALO + rows),)
    acc = ext[tail] * w_ref[conv_k - 1:conv_k, :]
    for j in range(conv_k - 1):
        acc = acc + pltpu.roll(ext, conv_k - 1 - j, axis=axis)[tail] * w_ref[j:j + 1, :]
    return acc


def _norm_matmul_kernel(x_ref, g_ref, w_ref, *rest, has_aux):
    if has_aux:
        wa_ref, o_ref, oa_ref, xn_ref = rest
    else:
        o_ref, xn_ref = rest

    @pl.when(pl.program_id(1) == 0)
    def _():
        x = x_ref[...]
        ms = jnp.mean(x * x, axis=-1, keepdims=True)
        xn = (x * lax.rsqrt(ms + EPS) * g_ref[...]).astype(BF16)
        xn_ref[...] = xn
        if has_aux:
            oa_ref[...] = jnp.dot(xn, wa_ref[...], preferred_element_type=F32)

    o_ref[...] = jnp.dot(xn_ref[...], w_ref[...], preferred_element_type=F32).astype(o_ref.dtype)


def norm_matmul(x, g, w, w_aux=None, *, tm, tn, out_dtype=F32, name):
    t, k = x.shape
    n = w.shape[1]
    assert t % tm == 0 and n % tn == 0
    has_aux = w_aux is not None
    in_specs = [
        pl.BlockSpec((tm, k), lambda i, j: (i, 0)),
        pl.BlockSpec((1, k), lambda i, j: (0, 0)),
        pl.BlockSpec((k, tn), lambda i, j: (0, j)),
    ]
    args = [x, g.reshape(1, k), w]
    out_shape = [jax.ShapeDtypeStruct((t, n), out_dtype)]
    out_specs = [pl.BlockSpec((tm, tn), lambda i, j: (i, j))]
    if has_aux:
        na = w_aux.shape[1]
        in_specs.append(pl.BlockSpec((k, na), lambda i, j: (0, 0)))
        args.append(w_aux)
        out_shape.append(jax.ShapeDtypeStruct((t, na), F32))
        out_specs.append(pl.BlockSpec((tm, na), lambda i, j: (i, 0)))
    res = pl.pallas_call(
        functools.partial(_norm_matmul_kernel, has_aux=has_aux),
        grid=(t // tm, n // tn),
        in_specs=in_specs,
        out_specs=out_specs,
        out_shape=out_shape,
        scratch_shapes=[pltpu.VMEM((tm, k), BF16)],
        compiler_params=_cparams("parallel", "arbitrary"),
        name=name,
    )(*args)
    return res if has_aux else res[0]


def _matmul_kernel(a_ref, w_ref, *rest, epilogue):
    *extra, o_ref = rest
    acc = jnp.dot(a_ref[...].astype(BF16), w_ref[...], preferred_element_type=F32)
    o_ref[...] = epilogue(acc, *[e[...] for e in extra]).astype(o_ref.dtype)


def _ep_residual(acc, r):
    return r + acc


def _ep_gate_merge(acc, gated_pool, g_ssm):
    return gated_pool + jax.nn.sigmoid(g_ssm) * acc


def matmul(a, w, extras, epilogue, *, tm, tn, out_dtype, name):
    t, k = a.shape
    n = w.shape[1]
    assert t % tm == 0 and n % tn == 0
    in_specs = [
        pl.BlockSpec((tm, k), lambda i, j: (i, 0)),
        pl.BlockSpec((k, tn), lambda i, j: (0, j)),
    ]
    args = [a, w]
    for arr, off in extras:
        assert off % tn == 0
        ob = off // tn
        in_specs.append(pl.BlockSpec((tm, tn), lambda i, j, ob=ob: (i, j + ob)))
        args.append(arr)
    return pl.pallas_call(
        functools.partial(_matmul_kernel, epilogue=epilogue),
        grid=(t // tm, n // tn),
        in_specs=in_specs,
        out_specs=pl.BlockSpec((tm, tn), lambda i, j: (i, j)),
        out_shape=jax.ShapeDtypeStruct((t, n), out_dtype),
        compiler_params=_cparams("parallel", "arbitrary"),
        name=name,
    )(*args)


def _rmsnorm_kernel(x_ref, g_ref, o_ref):
    x = x_ref[...]
    ms = jnp.mean(x * x, axis=-1, keepdims=True)
    o_ref[...] = x * lax.rsqrt(ms + EPS) * g_ref[...]


def rmsnorm(x, g, *, tm):
    t, k = x.shape
    return pl.pallas_call(
        _rmsnorm_kernel,
        grid=(t // tm,),
        in_specs=[pl.BlockSpec((tm, k), lambda i: (i, 0)), pl.BlockSpec((1, k), lambda i: (0, 0))],
        out_specs=pl.BlockSpec((tm, k), lambda i: (i, 0)),
        out_shape=jax.ShapeDtypeStruct((t, k), F32),
        compiler_params=_cparams("parallel"),
    )(x, g.reshape(1, k))


def _pool_kernel(cur_ref, prev_ref, gate_ref, wg_ref, scale_ref, wo_ref, o_ref, ext_ref, pooled_ref,
                 *, nb, L, from_state, pos0):
    w_ch = cur_ref.shape[-1]
    gdim = w_ch // len(POOL_WINDOWS)
    if from_state:
        ext_ref[:, 1:POOL_HALO, :] = prev_ref[0]
        pos_start = pos0
    else:
        i = pl.program_id(1)
        ext_ref[:, 0:POOL_HALO, :] = jnp.where(i == 0, 0.0, prev_ref[...])
        pos_start = pos0 + i * L
    ext_ref[:, POOL_HALO:POOL_HALO + L, :] = cur_ref[...]

    pos = pos_start + lax.broadcasted_iota(jnp.int32, (1, L, gdim), 1)
    for k, win in enumerate(POOL_WINDOWS):
        cs = slice(k * gdim, (k + 1) * gdim)
        cur = ext_ref[:, POOL_HALO:POOL_HALO + L, cs]
        acc = cur
        for j in range(1, win):
            acc = acc + ext_ref[:, POOL_HALO - j:POOL_HALO - j + L, cs]
        count = jnp.minimum(pos + 1, win).astype(F32)
        diff = (acc / count - cur).reshape(nb * L, gdim).astype(BF16)
        mixed = jnp.dot(diff, wg_ref[k], preferred_element_type=F32)
        pooled_ref[:, cs] = (mixed * scale_ref[:, cs]).astype(BF16)
    out_pool = jnp.dot(pooled_ref[...], wo_ref[...], preferred_element_type=F32)
    gate = jax.nn.sigmoid(gate_ref[...].reshape(nb * L, w_ch))
    o_ref[...] = (gate * out_pool).reshape(nb, L, w_ch)


def pool_branch(main3, prev_state, wg, scale, wo, *, layer, u_col, gate_col, nb, L, pos0):
    b, s, _ = main3.shape
    w_ch = wo.shape[0]
    ub, gb = u_col // w_ch, gate_col // w_ch
    from_state = prev_state is not None
    if from_state:
        assert L == s
        prev = prev_state
        prev_spec = pl.BlockSpec((1, nb, POOL_HALO - 1, w_ch), lambda n, i: (layer, n, 0, 0))
    else:
        assert nb == 1 and L % POOL_HALO == 0
        prev = main3
        r = L // POOL_HALO
        prev_spec = pl.BlockSpec((1, POOL_HALO, w_ch), lambda n, i: (n, jnp.maximum(i * r - 1, 0), ub))
    return pl.pallas_call(
        functools.partial(_pool_kernel, nb=nb, L=L, from_state=from_state, pos0=pos0),
        grid=(b // nb, s // L),
        in_specs=[
            pl.BlockSpec((nb, L, w_ch), lambda n, i: (n, i, ub)),
            prev_spec,
            pl.BlockSpec((nb, L, w_ch), lambda n, i: (n, i, gb)),
            pl.BlockSpec(wg.shape, lambda n, i: (0, 0, 0)),
            pl.BlockSpec((1, w_ch), lambda n, i: (0, 0)),
            pl.BlockSpec(wo.shape, lambda n, i: (0, 0)),
        ],
        out_specs=pl.BlockSpec((nb, L, w_ch), lambda n, i: (n, i, 0)),
        out_shape=jax.ShapeDtypeStruct((b, s, w_ch), F32),
        scratch_shapes=[pltpu.VMEM((nb, POOL_HALO + L, w_ch), F32), pltpu.VMEM((nb * L, w_ch), BF16)],
        compiler_params=_cparams("parallel", "arbitrary"),
        name="pool_state" if from_state else "pool_scan",
    )(main3, prev, main3, wg, scale.reshape(1, w_ch), wo)


def _cumsum_rows(x):
    q = x.shape[0]
    row = lax.broadcasted_iota(jnp.int32, x.shape, 0)
    k = 1
    while k < q:
        x = x + jnp.where(row >= k, pltpu.roll(x, k, axis=0), 0.0)
        k *= 2
    return x


def _expand_heads(vals, e_ref):
    q = vals[0].shape[0]
    pieces = []
    for v in vals:
        hi = v.astype(BF16).astype(F32)
        r1 = v - hi
        mid = r1.astype(BF16).astype(F32)
        pieces += [hi, mid, r1 - mid]
    lhs = jnp.concatenate(pieces, axis=0).astype(BF16)
    out = jnp.dot(lhs, e_ref[...], preferred_element_type=F32)
    return [out[(3 * i) * q:(3 * i + 1) * q] + out[(3 * i + 1) * q:(3 * i + 2) * q] + out[(3 * i + 2) * q:(3 * i + 3) * q]
            for i in range(len(vals))]


def _ssd_kernel(*refs, q, n_groups, heads_per_group, head_dim, n_state, from_state, conv_k):
    if from_state:
        (xbc_ref, prev_ref, z_ref, dt_ref, h0_ref, cw_ref, cb_ref, dtb_ref, alog_ref, dexp_ref, norm_ref, e_ref,
         *_, y_ref, hout_ref, ext_ref, act_ref, h_ref, yacc_ref) = refs
    else:
        (xbc_ref, prev_ref, z_ref, dt_ref, cw_ref, cb_ref, dtb_ref, alog_ref, dexp_ref, norm_ref, e_ref,
         y_ref, hout_ref, ext_ref, act_ref, h_ref, yacc_ref) = refs
    c = pl.program_id(1)
    gw = heads_per_group * head_dim
    inner = n_groups * gw
    hist = conv_k - 1

    if from_state:
        h_in, h_out = h0_ref.at[0, 0], hout_ref.at[0, 0]
        ext_ref[0:CONV_HALO, :] = jnp.zeros((CONV_HALO, ext_ref.shape[1]), F32)
        ext_ref[CONV_HALO - hist:CONV_HALO, :] = prev_ref[0, 0]
    else:
        h_in = h_out = h_ref

        @pl.when(c == 0)
        def _():
            h_ref[...] = jnp.zeros_like(h_ref)
        ext_ref[0:CONV_HALO, :] = jnp.where(c == 0, 0.0, prev_ref[0])
    ext_ref[CONV_HALO:CONV_HALO + q, :] = xbc_ref[0]

    act_ref[...] = _silu(_causal_conv(ext_ref[...], cw_ref, q, 0) + cb_ref[...])

    dt = jax.nn.softplus(dt_ref[0] + dtb_ref[...])
    la_cs = _cumsum_rows(dt * (-jnp.exp(alog_ref[...])))
    la_cs_t = la_cs.T
    dt_x, la_x = _expand_heads([dt, la_cs], e_ref)
    from_start_x = jnp.exp(la_x)
    to_end_x = jnp.exp(la_x[q - 1:q, :] - la_x)
    chunk_decay = jnp.exp(la_cs[q - 1:q, :])

    xs = act_ref[:, 0:inner]
    xd = xs * dt_x
    xdw = xd * to_end_x
    yacc_ref[...] = dexp_ref[...] * xs

    tri = lax.broadcasted_iota(jnp.int32, (q, q), 0) >= lax.broadcasted_iota(jnp.int32, (q, q), 1)
    assert LANES % head_dim == 0
    hpl = LANES // head_dim
    lane_head = lax.broadcasted_iota(jnp.int32, (q, LANES), 1) // head_dim
    bms, cbs = [], []
    for g in range(n_groups):
        gs = slice(g * gw, (g + 1) * gw)
        bm = act_ref[:, inner + g * n_state:inner + (g + 1) * n_state].astype(BF16)
        cm = act_ref[:, inner + (n_groups + g) * n_state:inner + (n_groups + g + 1) * n_state].astype(BF16)
        bms.append(bm)
        cbs.append(lax.dot_general(cm, bm, (((1,), (1,)), ((), ())), preferred_element_type=F32))
        y_off = lax.dot_general(cm, h_in[g].astype(BF16), (((1,), (1,)), ((), ())),
                                preferred_element_type=F32)
        yacc_ref[:, gs] += y_off * from_start_x[:, gs]
    for g in range(n_groups):
        cb = cbs[g]
        for lt in range(gw // LANES):
            ls = slice(g * gw + lt * LANES, g * gw + (lt + 1) * LANES)
            xd_t = xd[:, ls]
            y_t = None
            for k in range(hpl):
                hd = (g * gw + lt * LANES) // head_dim + k
                seg = la_cs[:, hd:hd + 1] - la_cs_t[hd:hd + 1, :]
                m = (cb * jnp.exp(jnp.where(tri, seg, -jnp.inf))).astype(BF16)
                rhs = jnp.where(lane_head == k, xd_t, 0.0).astype(BF16)
                part = jnp.dot(m, rhs, preferred_element_type=F32)
                y_t = part if y_t is None else y_t + part
            yacc_ref[:, ls] += y_t
    for g in range(n_groups):
        gs = slice(g * gw, (g + 1) * gw)
        s_new = lax.dot_general(xdw[:, gs].astype(BF16), bms[g], (((0,), (0,)), ((), ())),
                                preferred_element_type=F32)
        for r in range(heads_per_group):
            hd = g * heads_per_group + r
            rows = slice(r * head_dim, (r + 1) * head_dim)
            h_out[g, rows, :] = h_in[g, rows, :] * chunk_decay[:, hd:hd + 1] + s_new[rows, :]

    y = yacc_ref[...] * _silu(z_ref[0])
    for g in range(n_groups):
        gs = slice(g * gw, (g + 1) * gw)
        yg = y[:, gs]
        ms = jnp.mean(yg * yg, axis=-1, keepdims=True)
        y_ref[0, :, gs] = (yg * lax.rsqrt(ms + EPS) * norm_ref[:, gs]).astype(y_ref.dtype)

    if not from_state:
        @pl.when(c == pl.num_programs(1) - 1)
        def _():
            hout_ref[0] = h_ref[...]


def ssd_branch(main3, dt3, conv_state, ssm_state, state_out, cw, cb, dt_bias, a_log, d_exp, norm, expand,
               *, layer, xbc_col, z_col, q, n_groups, heads_per_group, head_dim, n_state, out_dtype):
    b, s, _ = main3.shape
    gw = heads_per_group * head_dim
    inner = n_groups * gw
    conv_dim = inner + 2 * n_groups * n_state
    conv_k = cw.shape[0]
    from_state = ssm_state is not None
    xb, zb = xbc_col // conv_dim, z_col // inner
    assert xbc_col % conv_dim == 0 and z_col % inner == 0

    def const(shape):
        nd = len(shape)
        return pl.BlockSpec(shape, lambda n, c: (0,) * nd)

    in_specs = [pl.BlockSpec((1, q, conv_dim), lambda n, c: (n, c, xb))]
    args = [main3]
    if from_state:
        assert q == s
        in_specs.append(pl.BlockSpec((1, 1, conv_k - 1, conv_dim), lambda n, c: (layer, n, 0, 0)))
        args.append(conv_state)
    else:
        assert q % CONV_HALO == 0
        r = q // CONV_HALO
        in_specs.append(pl.BlockSpec((1, CONV_HALO, conv_dim), lambda n, c: (n, jnp.maximum(c * r - 1, 0), xb)))
        args.append(main3)
    in_specs += [pl.BlockSpec((1, q, inner), lambda n, c: (n, c, zb)),
                 pl.BlockSpec((1, q, LANES), lambda n, c: (n, c, 0))]
    args += [main3, dt3]
    if from_state:
        in_specs.append(pl.BlockSpec((1, 1, n_groups, gw, n_state), lambda n, c: (layer, n, 0, 0, 0)))
        args.append(ssm_state)
    small = [cw, cb.reshape(1, conv_dim), dt_bias, a_log, d_exp, norm.reshape(1, inner), expand]
    in_specs += [const(a.shape) for a in small]
    args += small
    aliases = {}
    if from_state:
        state_spec = pl.BlockSpec((1, 1, n_groups, gw, n_state), lambda n, c: (layer, n, 0, 0, 0))
        state_shape = jax.ShapeDtypeStruct(ssm_state.shape, F32)
        if state_out is not None:
            aliases = {len(args): 1}
            in_specs.append(pl.BlockSpec(memory_space=pl.ANY))
            args.append(state_out)
    else:
        state_spec = pl.BlockSpec((1, n_groups, gw, n_state), lambda n, c: (n, 0, 0, 0))
        state_shape = jax.ShapeDtypeStruct((b, n_groups, gw, n_state), F32)
    return pl.pallas_call(
        functools.partial(_ssd_kernel, q=q, n_groups=n_groups, heads_per_group=heads_per_group,
                          head_dim=head_dim, n_state=n_state, from_state=from_state, conv_k=conv_k),
        grid=(b, s // q),
        in_specs=in_specs,
        out_specs=[pl.BlockSpec((1, q, inner), lambda n, c: (n, c, 0)), state_spec],
        out_shape=[jax.ShapeDtypeStruct((b, s, inner), out_dtype), state_shape],
        input_output_aliases=aliases,
        scratch_shapes=[pltpu.VMEM((CONV_HALO + q, conv_dim), F32),
                        pltpu.VMEM((q, conv_dim), F32),
                        pltpu.VMEM((1, SUBLANES, LANES) if from_state else (n_groups, gw, n_state), F32),
                        pltpu.VMEM((q, inner), F32)],
        compiler_params=_cparams("parallel", "arbitrary"),
        name="ssd_state" if from_state else "ssd_scan",
    )(*args)


def _kv_head(ref, n, h, d, stage_ref, slot):
    if len(ref.shape) == 5:
        stage_ref[slot] = ref[0, n, :, h, :]
        return stage_ref[slot].astype(BF16)
    return ref[0, n, :, h * d:(h + 1) * d].astype(BF16)


def _attn_kernel(q_ref, k_ref, v_ref, o_ref, *stage, n_heads, nb):
    stage_ref = stage[0] if stage else None
    d = q_ref.shape[-1] // n_heads
    inv = 1.0 / math.sqrt(d)
    units = [(n, h, slice(h * d, (h + 1) * d)) for n in range(nb) for h in range(n_heads)]
    scores = []
    for u, (n, h, hs) in enumerate(units):
        qh = q_ref[n, :, hs].astype(BF16)
        kh = _kv_head(k_ref, n, h, d, stage_ref, 2 * u)
        scores.append(lax.dot_general(qh, kh, (((1,), (1,)), ((), ())), preferred_element_type=F32) * inv)
    probs = []
    for s in scores:
        e = jnp.exp(s - jnp.max(s, axis=-1, keepdims=True))
        probs.append((e / jnp.sum(e, axis=-1, keepdims=True)).astype(BF16))
    for u, ((n, h, hs), p) in enumerate(zip(units, probs)):
        vh = _kv_head(v_ref, n, h, d, stage_ref, 2 * u + 1)
        o_ref[n, :, hs] = jnp.dot(p, vh, preferred_element_type=F32).astype(o_ref.dtype)


def _attn_few_kernel(q_ref, k_ref, v_ref, o_ref, *stage, n_heads, nb):
    stage_ref = stage[0] if stage else None
    tq = q_ref.shape[1]
    d = q_ref.shape[-1] // n_heads
    inv = 1.0 / math.sqrt(d)
    pad = jnp.zeros((LANES - tq, d), F32)
    units = [(n, h, slice(h * d, (h + 1) * d)) for n in range(nb) for h in range(n_heads)]
    scores_t = []
    for u, (n, h, hs) in enumerate(units):
        qh = jnp.concatenate([q_ref[n, :, hs].astype(F32), pad], axis=0).astype(BF16)
        kh = _kv_head(k_ref, n, h, d, stage_ref, 2 * u)
        scores_t.append(lax.dot_general(kh, qh, (((1,), (1,)), ((), ())), preferred_element_type=F32) * inv)
    probs = []
    for st in scores_t:
        e = jnp.exp(st - jnp.max(st, axis=0, keepdims=True))
        pt = e / jnp.sum(e, axis=0, keepdims=True)
        probs.append(pt.T[0:tq, :].astype(BF16))
    for u, ((n, h, hs), p) in enumerate(zip(units, probs)):
        vh = _kv_head(v_ref, n, h, d, stage_ref, 2 * u + 1)
        o_ref[n, :, hs] = jnp.dot(p, vh, preferred_element_type=F32).astype(o_ref.dtype)


def cross_attention(q3, k4, v4, *, layer, nb, tq, n_heads, out_dtype):
    b, s, dm = q3.shape
    m = k4.shape[2]
    body = _attn_few_kernel if tq < LANES else _attn_kernel
    zeros = (0,) * (k4.ndim - 2)
    kv_spec = pl.BlockSpec((1, nb) + k4.shape[2:], lambda n, i: (layer, n) + zeros)
    stage = [pltpu.VMEM((2 * nb * n_heads, m, dm // n_heads), F32)] if k4.ndim == 5 else []
    return pl.pallas_call(
        functools.partial(body, n_heads=n_heads, nb=nb),
        grid=(b // nb, s // tq),
        in_specs=[pl.BlockSpec((nb, tq, dm), lambda n, i: (n, i, 0)), kv_spec, kv_spec],
        out_specs=pl.BlockSpec((nb, tq, dm), lambda n, i: (n, i, 0)),
        out_shape=jax.ShapeDtypeStruct((b, s, dm), out_dtype),
        scratch_shapes=stage,
        compiler_params=_cparams("parallel", "arbitrary"),
        name="cross_attention",
    )(q3, k4, v4)


def _ffn_mid_kernel(g_ref, v_ref, gp_ref, vp_ref, gw_ref, vw_ref, gb_ref, vb_ref, o_ref, gext_ref, vext_ref,
                    *, L, from_state, conv_k):
    hist = conv_k - 1
    i = pl.program_id(1)

    def conv(cur_ref, prev_ref, w_ref, b_ref, ext_ref):
        if from_state:
            ext_ref[:, 0:CONV_HALO, :] = jnp.zeros((ext_ref.shape[0], CONV_HALO, ext_ref.shape[2]), F32)
            ext_ref[:, CONV_HALO - hist:CONV_HALO, :] = prev_ref[0]
        else:
            ext_ref[:, 0:CONV_HALO, :] = jnp.where(i == 0, 0.0, prev_ref[...])
        ext_ref[:, CONV_HALO:CONV_HALO + L, :] = cur_ref[...]
        return _causal_conv(ext_ref[...], w_ref, L, 1) + b_ref[...]

    g = conv(g_ref, gp_ref, gw_ref, gb_ref, gext_ref)
    v = conv(v_ref, vp_ref, vw_ref, vb_ref, vext_ref)
    o_ref[...] = (_silu(g) * v).astype(o_ref.dtype)


def ffn_mid(up3, prev_state, cw, cb, *, layer, nb, L, tc, out_dtype):
    b, s, two_f = up3.shape
    f = two_f // 2
    conv_k = cw.shape[0]
    assert f % tc == 0
    nj = f // tc
    from_state = prev_state is not None
    if from_state:
        assert L == s
        prev = prev_state
        gp_spec = pl.BlockSpec((1, nb, conv_k - 1, tc), lambda n, i, j: (layer, n, 0, j))
        vp_spec = pl.BlockSpec((1, nb, conv_k - 1, tc), lambda n, i, j: (layer, n, 0, j + nj))
    else:
        assert L % CONV_HALO == 0
        prev = up3
        r = L // CONV_HALO
        gp_spec = pl.BlockSpec((nb, CONV_HALO, tc), lambda n, i, j: (n, jnp.maximum(i * r - 1, 0), j))
        vp_spec = pl.BlockSpec((nb, CONV_HALO, tc), lambda n, i, j: (n, jnp.maximum(i * r - 1, 0), j + nj))
    cb2 = cb.reshape(1, two_f)
    return pl.pallas_call(
        functools.partial(_ffn_mid_kernel, L=L, from_state=from_state, conv_k=conv_k),
        grid=(b // nb, s // L, nj),
        in_specs=[
            pl.BlockSpec((nb, L, tc), lambda n, i, j: (n, i, j)),
            pl.BlockSpec((nb, L, tc), lambda n, i, j: (n, i, j + nj)),
            gp_spec, vp_spec,
            pl.BlockSpec((conv_k, tc), lambda n, i, j: (0, j)),
            pl.BlockSpec((conv_k, tc), lambda n, i, j: (0, j + nj)),
            pl.BlockSpec((1, tc), lambda n, i, j: (0, j)),
            pl.BlockSpec((1, tc), lambda n, i, j: (0, j + nj)),
        ],
        out_specs=pl.BlockSpec((nb, L, tc), lambda n, i, j: (n, i, j)),
        out_shape=jax.ShapeDtypeStruct((b, s, f), out_dtype),
        scratch_shapes=[pltpu.VMEM((nb, CONV_HALO + L, tc), F32), pltpu.VMEM((nb, CONV_HALO + L, tc), F32)],
        compiler_params=_cparams("parallel", "arbitrary", "arbitrary"),
        name="ffn_mid_state" if from_state else "ffn_mid_scan",
    )(up3, up3, prev, prev, cw, cw, cb2, cb2)


def _ffn_front_kernel(x_ref, g_ref, wg_ref, wv_ref, cwg_ref, cwv_ref, cbg_ref, cbv_ref,
                      act_ref, tail_g_ref, tail_v_ref, xn_ref, halo_ref, *, tiles_per_seq):
    i, j = pl.program_id(0), pl.program_id(1)
    tm = x_ref.shape[0]

    @pl.when(j == 0)
    def _():
        x = x_ref[...]
        ms = jnp.mean(x * x, axis=-1, keepdims=True)
        xn_ref[...] = (x * lax.rsqrt(ms + EPS) * g_ref[...]).astype(BF16)

    xn = xn_ref[...]
    seq_start = (i % tiles_per_seq) == 0

    def half(w_ref, cw_ref, cb_ref, tail_ref, slot):
        up = jnp.dot(xn, w_ref[...], preferred_element_type=F32)
        prev = jnp.where(seq_start, 0.0, halo_ref[slot, j])
        tail = up[tm - CONV_HALO:tm, :]
        halo_ref[slot, j] = tail
        tail_ref[0] = tail
        ext = jnp.concatenate([prev, up], axis=0)
        return _causal_conv(ext, cw_ref, tm, 0) + cb_ref[...]

    g = half(wg_ref, cwg_ref, cbg_ref, tail_g_ref, 0)
    v = half(wv_ref, cwv_ref, cbv_ref, tail_v_ref, 1)
    act_ref[...] = (_silu(g) * v).astype(act_ref.dtype)


def ffn_front(x2, gain, w_up, cw, cb, *, seq_len, tm, tn, out_dtype):
    t, k = x2.shape
    f = w_up.shape[1] // 2
    assert t % tm == 0 and f % tn == 0 and seq_len % tm == 0
    nj = f // tn
    conv_k = cw.shape[0]
    cb2 = cb.reshape(1, 2 * f)
    return pl.pallas_call(
        functools.partial(_ffn_front_kernel, tiles_per_seq=seq_len // tm),
        grid=(t // tm, nj),
        in_specs=[
            pl.BlockSpec((tm, k), lambda i, j: (i, 0)),
            pl.BlockSpec((1, k), lambda i, j: (0, 0)),
            pl.BlockSpec((k, tn), lambda i, j: (0, j)),
            pl.BlockSpec((k, tn), lambda i, j: (0, j + nj)),
            pl.BlockSpec((conv_k, tn), lambda i, j: (0, j)),
            pl.BlockSpec((conv_k, tn), lambda i, j: (0, j + nj)),
            pl.BlockSpec((1, tn), lambda i, j: (0, j)),
            pl.BlockSpec((1, tn), lambda i, j: (0, j + nj)),
        ],
        out_specs=[pl.BlockSpec((tm, tn), lambda i, j: (i, j)),
                   pl.BlockSpec((1, CONV_HALO, tn), lambda i, j: (i, 0, j)),
                   pl.BlockSpec((1, CONV_HALO, tn), lambda i, j: (i, 0, j))],
        out_shape=[jax.ShapeDtypeStruct((t, f), out_dtype),
                   jax.ShapeDtypeStruct((t // tm, CONV_HALO, f), F32),
                   jax.ShapeDtypeStruct((t // tm, CONV_HALO, f), F32)],
        scratch_shapes=[pltpu.VMEM((tm, k), BF16), pltpu.VMEM((2, nj, CONV_HALO, tn), F32)],
        compiler_params=_cparams("arbitrary", "arbitrary"),
        name="ffn_front",
    )(x2, gain.reshape(1, k), w_up, w_up, cw, cw, cb2, cb2)


def _layer(x3, lw, dims, states, ssm_out, mem_kv, *, layer, kv_layer, pos0, tiles):
    b, s, d = x3.shape
    t = b * s
    x2 = x3.reshape(t, d)
    G, R, P, N = dims["groups"], dims["heads_per_group"], dims["head_dim"], dims["n_state"]
    inner = G * R * P
    conv_dim = inner + 2 * G * N
    z_col, u_col, xbc_col, gp_col, gs_col = 0, inner, inner + d, inner + d + conv_dim, inner + 2 * d + conv_dim
    n_main = gs_col + d
    tm, tn, tn_k = tiles["tm"], tiles["tn"], tiles["tn_long_k"]
    act_dtype = tiles["act_dtype"]

    main, dt = norm_matmul(x2, lw["norm_mix"], lw["w_main"], lw["w_dt"], tm=tm, tn=tn, name="in_proj")
    main3 = main.reshape(b, s, n_main)
    dt3 = dt.reshape(b, s, LANES)

    if states is None:
        pool_prev = conv_prev = ssm_prev = ffn_prev = None
    else:
        pool_prev, conv_prev, ssm_prev, ffn_prev = states

    gated_pool = pool_branch(main3, pool_prev, lw["w_pool_group"], lw["pool_scale"], lw["w_pool_out"], layer=layer,
                             u_col=u_col, gate_col=gp_col, nb=tiles["pool_nb"], L=tiles["pool_L"], pos0=pos0)
    y, ssm_new = ssd_branch(main3, dt3, conv_prev, ssm_prev, ssm_out, lw["ssm_conv_w"], lw["ssm_conv_b"],
                            lw["dt_bias"], lw["a_log"], lw["d_exp"], lw["ssm_norm"], lw["expand"], layer=layer,
                            xbc_col=xbc_col, z_col=z_col, q=tiles["ssd_q"], n_groups=G, heads_per_group=R,
                            head_dim=P, n_state=N, out_dtype=act_dtype)
    merged = matmul(y.reshape(t, inner), lw["w_ssm_out"],
                    [(gated_pool.reshape(t, d), 0), (main, gs_col)], _ep_gate_merge,
                    tm=tm, tn=tn_k, out_dtype=BF16, name="ssm_out_merge")
    x2 = matmul(merged, lw["w_out"], [(x2, 0)], _ep_residual, tm=tm, tn=tn, out_dtype=F32, name="mix_out")

    qm = norm_matmul(x2, lw["norm_mem_q"], lw["w_mem_q"], tm=tm, tn=tn, out_dtype=act_dtype, name="mem_q")
    k4, v4 = mem_kv
    o = cross_attention(qm.reshape(b, s, d), k4, v4, layer=kv_layer, nb=tiles["attn_nb"], tq=tiles["attn_tq"],
                        n_heads=dims["mem_heads"], out_dtype=act_dtype)
    x2 = matmul(o.reshape(t, d), lw["w_mem_o"], [(x2, 0)], _ep_residual, tm=tm, tn=tn, out_dtype=F32, name="mem_o")

    hf = lw["ffn_conv_w"].shape[0] - 1
    if states is None:
        act2, tail_g, tail_v = ffn_front(x2, lw["norm_ffn"], lw["w_ffn_up"], lw["ffn_conv_w"], lw["ffn_conv_b"],
                                         seq_len=s, tm=tm, tn=tiles["ffn_tc"], out_dtype=act_dtype)
        per_seq = s // tm
        ffn_new = jnp.concatenate([tail_g[per_seq - 1::per_seq, CONV_HALO - hf:],
                                   tail_v[per_seq - 1::per_seq, CONV_HALO - hf:]], axis=-1)
    else:
        up = norm_matmul(x2, lw["norm_ffn"], lw["w_ffn_up"], tm=tm, tn=tn, name="ffn_up")
        up3 = up.reshape(b, s, up.shape[1])
        act2 = ffn_mid(up3, ffn_prev, lw["ffn_conv_w"], lw["ffn_conv_b"], layer=layer, nb=tiles["ffn_nb"],
                       L=tiles["ffn_L"], tc=tiles["ffn_tc"], out_dtype=act_dtype).reshape(t, -1)
    x2 = matmul(act2, lw["w_ffn_down"], [(x2, 0)], _ep_residual,
                tm=tiles["tm_down"], tn=tn_k, out_dtype=F32, name="ffn_down")

    u3 = main3[:, :, u_col:u_col + d]
    xbc3 = main3[:, :, xbc_col:xbc_col + conv_dim]
    hp, hc = POOL_HALO - 1, lw["ssm_conv_w"].shape[0] - 1
    if states is None:
        pool_new, conv_new = u3[:, s - hp:], xbc3[:, s - hc:]
    else:
        pool_new = jnp.concatenate([pool_prev[layer], u3], axis=1)[:, s:]
        conv_new = jnp.concatenate([conv_prev[layer], xbc3], axis=1)[:, s:]
        ffn_new = jnp.concatenate([ffn_prev[layer], up3], axis=1)[:, s:]
    return x2.reshape(b, s, d), pool_new, conv_new, ssm_new, ffn_new


def kernel(x_prompt, x_sample, state_pool, state_ssm_conv, state_ssm, state_ffn_conv, cache_mem_k, cache_mem_v,
           mem_prompt, norm_mix, w_in, w_pool_group, pool_scale, w_pool_out, ssm_conv_w, ssm_conv_b, ssm_dt_bias,
           ssm_a_log, ssm_d, ssm_norm, w_ssm_out, w_out, norm_mem_q, w_mem_q, w_mem_o, norm_mem_kv, w_mem_k,
           w_mem_v, norm_ffn, w_ffn_up, ffn_conv_w, ffn_conv_b, w_ffn_down, norm_final):
    depth = w_in.shape[0]
    bp, sp, d = x_prompt.shape
    bs, ss, _ = x_sample.shape
    n_heads = ssm_d.shape[1]
    inner = w_ssm_out.shape[1]
    head_dim = inner // n_heads
    n_state = state_ssm.shape[-1]
    conv_dim = ssm_conv_w.shape[2]
    n_groups = (conv_dim - inner) // (2 * n_state)
    mem_heads = cache_mem_k.shape[3]
    mem_len = mem_prompt.shape[1]
    dims = dict(groups=n_groups, heads_per_group=n_heads // n_groups, head_dim=head_dim, n_state=n_state,
                mem_heads=mem_heads)
    assert n_heads <= LANES

    c_u, c_z, c_x, c_dt, c_gp = d, d + inner, d + inner + conv_dim, d + inner + conv_dim + n_heads, 2 * d + inner + conv_dim + n_heads
    head_of_col = jnp.arange(inner, dtype=jnp.int32) // head_dim
    expand = (jnp.arange(LANES, dtype=jnp.int32)[:, None] == head_of_col[None, :]).astype(BF16)

    def pad_heads(v):
        return jnp.pad(v, (0, LANES - n_heads)).reshape(1, LANES)

    tiles_p = dict(tm=1024, tm_down=1024, tn=1024, tn_long_k=512, pool_nb=1, pool_L=512, ssd_q=128,
                   attn_nb=1, attn_tq=512, ffn_nb=1, ffn_L=512, ffn_tc=512, act_dtype=BF16)
    tiles_s = dict(tm=512, tm_down=512, tn=1024, tn_long_k=512, pool_nb=32, pool_L=ss, ssd_q=ss,
                   attn_nb=2, attn_tq=ss, ffn_nb=64, ffn_L=ss, ffn_tc=512, act_dtype=F32)

    ssm_states = state_ssm.reshape(depth, bs, n_groups, inner // n_groups, n_state)
    kv_s = (cache_mem_k, cache_mem_v)
    states_s = (state_pool, state_ssm_conv, ssm_states, state_ffn_conv)
    ssm_s_all = None

    yp, ys = x_prompt, x_sample
    outs = [[] for _ in range(9)]
    for i in range(depth):
        wi = w_in[i]
        lw = dict(
            norm_mix=norm_mix[i],
            w_main=jnp.concatenate([wi[:, c_u:c_z], wi[:, :c_u], wi[:, c_z:c_x], wi[:, c_dt:]], axis=1).astype(BF16),
            w_dt=jnp.pad(wi[:, c_x:c_dt], ((0, 0), (0, LANES - n_heads))).astype(BF16),
            w_pool_group=w_pool_group[i].astype(BF16), pool_scale=pool_scale[i],
            w_pool_out=w_pool_out[i].astype(BF16),
            ssm_conv_w=ssm_conv_w[i], ssm_conv_b=ssm_conv_b[i],
            dt_bias=pad_heads(ssm_dt_bias[i]), a_log=pad_heads(ssm_a_log[i]),
            d_exp=jnp.repeat(ssm_d[i], head_dim).reshape(1, inner), ssm_norm=ssm_norm[i], expand=expand,
            w_ssm_out=w_ssm_out[i].astype(BF16), w_out=w_out[i].astype(BF16),
            norm_mem_q=norm_mem_q[i], w_mem_q=w_mem_q[i].astype(BF16), w_mem_o=w_mem_o[i].astype(BF16),
            norm_ffn=norm_ffn[i], w_ffn_up=w_ffn_up[i].astype(BF16),
            ffn_conv_w=ffn_conv_w[i], ffn_conv_b=ffn_conv_b[i], w_ffn_down=w_ffn_down[i].astype(BF16),
        )
        mem2 = mem_prompt.reshape(bp * mem_len, d)
        k_i = norm_matmul(mem2, norm_mem_kv[i], w_mem_k[i].astype(BF16), tm=512, tn=1024, name="mem_k")
        v_i = norm_matmul(mem2, norm_mem_kv[i], w_mem_v[i].astype(BF16), tm=512, tn=1024, name="mem_v")
        kv_p = (k_i.reshape(1, bp, mem_len, d), v_i.reshape(1, bp, mem_len, d))

        yp, a0, a1, a2, a3 = _layer(yp, lw, dims, None, None, kv_p, layer=0, kv_layer=0, pos0=0, tiles=tiles_p)
        ys, b0, b1, ssm_s_all, b3 = _layer(ys, lw, dims, states_s, ssm_s_all, kv_s, layer=i, kv_layer=i,
                                           pos0=PAST_LEN, tiles=tiles_s)
        for lst, val in zip(outs, (a0, b0, a1, b1, a2.reshape(bp, n_heads, head_dim, n_state), a3, b3,
                                   k_i.reshape(bp, mem_len, mem_heads, d // mem_heads),
                                   v_i.reshape(bp, mem_len, mem_heads, d // mem_heads))):
            lst.append(val)

    y_prompt = rmsnorm(yp.reshape(bp * sp, d), norm_final, tm=512).reshape(bp, sp, d)
    y_sample = rmsnorm(ys.reshape(bs * ss, d), norm_final, tm=512).reshape(bs, ss, d)
    pool_p, pool_s, conv_p, conv_s, ssm_p, ffn_p, ffn_s, mk_p, mv_p = (jnp.stack(lst) for lst in outs)
    ssm_s = ssm_s_all.reshape(depth, bs, n_heads, head_dim, n_state)
    return (y_prompt, y_sample, pool_p, pool_s, conv_p, conv_s, ssm_p, ssm_s, ffn_p, ffn_s, mk_p, mv_p)
```

```python
import functools
import math

import jax
import jax.numpy as jnp
from jax import lax
from jax.experimental import pallas as pl
from jax.experimental.pallas import tpu as pltpu

F32 = jnp.float32
BF16 = jnp.bfloat16
EPS = 1e-6

LANES = 128
SUBLANES = 8
VMEM_LIMIT = 56 * 1024 * 1024

PAST_LEN = 16384
POOL_WINDOWS = (2, 4, 8, 16)
POOL_HALO = 16
CONV_HALO = SUBLANES


def _cparams(*sem):
    return pltpu.CompilerParams(dimension_semantics=sem, vmem_limit_bytes=VMEM_LIMIT)


def _silu(x):
    h = 0.5 * x
    return h + h * jnp.tanh(h)


def _causal_conv(ext, w_ref, rows, axis):
    conv_k = w_ref.shape[0]
    tail = (slice(None),) * axis + (slice(CONV_HALO, CONV_HALO + rows),)
    acc = ext[tail] * w_ref[conv_k - 1:conv_k, :]
    for j in range(conv_k - 1):
        acc = acc + pltpu.roll(ext, conv_k - 1 - j, axis=axis)[tail] * w_ref[j:j + 1, :]
    return acc


def _norm_matmul_kernel(x_ref, g_ref, w_ref, *rest, has_aux):
    if has_aux:
        wa_ref, o_ref, oa_ref, xn_ref = rest
    else:
        o_ref, xn_ref = rest

    @pl.when(pl.program_id(1) == 0)
    def _():
        x = x_ref[...]
        ms = jnp.mean(x * x, axis=-1, keepdims=True)
        xn = (x * lax.rsqrt(ms + EPS) * g_ref[...]).astype(BF16)
        xn_ref[...] = xn
        if has_aux:
            oa_ref[...] = jnp.dot(xn, wa_ref[...], preferred_element_type=F32)

    o_ref[...] = jnp.dot(xn_ref[...], w_ref[...], preferred_element_type=F32).astype(o_ref.dtype)


def norm_matmul(x, g, w, w_aux=None, *, tm, tn, out_dtype=F32, name):
    t, k = x.shape
    n = w.shape[1]
    assert t % tm == 0 and n % tn == 0
    has_aux = w_aux is not None
    in_specs = [
        pl.BlockSpec((tm, k), lambda i, j: (i, 0)),
        pl.BlockSpec((1, k), lambda i, j: (0, 0)),
        pl.BlockSpec((k, tn), lambda i, j: (0, j)),
    ]
    args = [x, g.reshape(1, k), w]
    out_shape = [jax.ShapeDtypeStruct((t, n), out_dtype)]
    out_specs = [pl.BlockSpec((tm, tn), lambda i, j: (i, j))]
    if has_aux:
        na = w_aux.shape[1]
        in_specs.append(pl.BlockSpec((k, na), lambda i, j: (0, 0)))
        args.append(w_aux)
        out_shape.append(jax.ShapeDtypeStruct((t, na), F32))
        out_specs.append(pl.BlockSpec((tm, na), lambda i, j: (i, 0)))
    res = pl.pallas_call(
        functools.partial(_norm_matmul_kernel, has_aux=has_aux),
        grid=(t // tm, n // tn),
        in_specs=in_specs,
        out_specs=out_specs,
        out_shape=out_shape,
        scratch_shapes=[pltpu.VMEM((tm, k), BF16)],
        compiler_params=_cparams("parallel", "arbitrary"),
        name=name,
    )(*args)
    return res if has_aux else res[0]


def _matmul_kernel(a_ref, w_ref, *rest, epilogue):
    *extra, o_ref = rest
    acc = jnp.dot(a_ref[...].astype(BF16), w_ref[...], preferred_element_type=F32)
    o_ref[...] = epilogue(acc, *[e[...] for e in extra]).astype(o_ref.dtype)


def _ep_residual(acc, r):
    return r + acc


def _ep_gate_merge(acc, gated_pool, g_ssm):
    return gated_pool + jax.nn.sigmoid(g_ssm) * acc


def matmul(a, w, extras, epilogue, *, tm, tn, out_dtype, name):
    t, k = a.shape
    n = w.shape[1]
    assert t % tm == 0 and n % tn == 0
    in_specs = [
        pl.BlockSpec((tm, k), lambda i, j: (i, 0)),
        pl.BlockSpec((k, tn), lambda i, j: (0, j)),
    ]
    args = [a, w]
    for arr, off in extras:
        assert off % tn == 0
        ob = off // tn
        in_specs.append(pl.BlockSpec((tm, tn), lambda i, j, ob=ob: (i, j + ob)))
        args.append(arr)
    return pl.pallas_call(
        functools.partial(_matmul_kernel, epilogue=epilogue),
        grid=(t // tm, n // tn),
        in_specs=in_specs,
        out_specs=pl.BlockSpec((tm, tn), lambda i, j: (i, j)),
        out_shape=jax.ShapeDtypeStruct((t, n), out_dtype),
        compiler_params=_cparams("parallel", "arbitrary"),
        name=name,
    )(*args)


def _rmsnorm_kernel(x_ref, g_ref, o_ref):
    x = x_ref[...]
    ms = jnp.mean(x * x, axis=-1, keepdims=True)
    o_ref[...] = x * lax.rsqrt(ms + EPS) * g_ref[...]


def rmsnorm(x, g, *, tm):
    t, k = x.shape
    return pl.pallas_call(
        _rmsnorm_kernel,
        grid=(t // tm,),
        in_specs=[pl.BlockSpec((tm, k), lambda i: (i, 0)), pl.BlockSpec((1, k), lambda i: (0, 0))],
        out_specs=pl.BlockSpec((tm, k), lambda i: (i, 0)),
        out_shape=jax.ShapeDtypeStruct((t, k), F32),
        compiler_params=_cparams("parallel"),
    )(x, g.reshape(1, k))


def _pool_kernel(cur_ref, prev_ref, gate_ref, wg_ref, scale_ref, wo_ref, o_ref, ext_ref, pooled_ref,
                 *, nb, L, from_state, pos0):
    w_ch = cur_ref.shape[-1]
    gdim = w_ch // len(POOL_WINDOWS)
    if from_state:
        ext_ref[:, 1:POOL_HALO, :] = prev_ref[0]
        pos_start = pos0
    else:
        i = pl.program_id(1)
        ext_ref[:, 0:POOL_HALO, :] = jnp.where(i == 0, 0.0, prev_ref[...])
        pos_start = pos0 + i * L
    ext_ref[:, POOL_HALO:POOL_HALO + L, :] = cur_ref[...]

    pos = pos_start + lax.broadcasted_iota(jnp.int32, (1, L, gdim), 1)
    for k, win in enumerate(POOL_WINDOWS):
        cs = slice(k * gdim, (k + 1) * gdim)
        cur = ext_ref[:, POOL_HALO:POOL_HALO + L, cs]
        acc = cur
        for j in range(1, win):
            acc = acc + ext_ref[:, POOL_HALO - j:POOL_HALO - j + L, cs]
        count = jnp.minimum(pos + 1, win).astype(F32)
        diff = (acc / count - cur).reshape(nb * L, gdim).astype(BF16)
        mixed = jnp.dot(diff, wg_ref[k], preferred_element_type=F32)
        pooled_ref[:, cs] = (mixed * scale_ref[:, cs]).astype(BF16)
    out_pool = jnp.dot(pooled_ref[...], wo_ref[...], preferred_element_type=F32)
    gate = jax.nn.sigmoid(gate_ref[...].reshape(nb * L, w_ch))
    o_ref[...] = (gate * out_pool).reshape(nb, L, w_ch)


def pool_branch(main3, prev_state, wg, scale, wo, *, layer, u_col, gate_col, nb, L, pos0):
    b, s, _ = main3.shape
    w_ch = wo.shape[0]
    ub, gb = u_col // w_ch, gate_col // w_ch
    from_state = prev_state is not None
    if from_state:
        assert L == s
        prev = prev_state
        prev_spec = pl.BlockSpec((1, nb, POOL_HALO - 1, w_ch), lambda n, i: (layer, n, 0, 0))
    else:
        assert nb == 1 and L % POOL_HALO == 0
        prev = main3
        r = L // POOL_HALO
        prev_spec = pl.BlockSpec((1, POOL_HALO, w_ch), lambda n, i: (n, jnp.maximum(i * r - 1, 0), ub))
    return pl.pallas_call(
        functools.partial(_pool_kernel, nb=nb, L=L, from_state=from_state, pos0=pos0),
        grid=(b // nb, s // L),
        in_specs=[
            pl.BlockSpec((nb, L, w_ch), lambda n, i: (n, i, ub)),
            prev_spec,
            pl.BlockSpec((nb, L, w_ch), lambda n, i: (n, i, gb)),
            pl.BlockSpec(wg.shape, lambda n, i: (0, 0, 0)),
            pl.BlockSpec((1, w_ch), lambda n, i: (0, 0)),
            pl.BlockSpec(wo.shape, lambda n, i: (0, 0)),
        ],
        out_specs=pl.BlockSpec((nb, L, w_ch), lambda n, i: (n, i, 0)),
        out_shape=jax.ShapeDtypeStruct((b, s, w_ch), F32),
        scratch_shapes=[pltpu.VMEM((nb, POOL_HALO + L, w_ch), F32), pltpu.VMEM((nb * L, w_ch), BF16)],
        compiler_params=_cparams("parallel", "arbitrary"),
        name="pool_state" if from_state else "pool_scan",
    )(main3, prev, main3, wg, scale.reshape(1, w_ch), wo)


def _cumsum_rows(x):
    q = x.shape[0]
    row = lax.broadcasted_iota(jnp.int32, x.shape, 0)
    k = 1
    while k < q:
        x = x + jnp.where(row >= k, pltpu.roll(x, k, axis=0), 0.0)
        k *= 2
    return x


def _expand_heads(vals, e_ref):
    q = vals[0].shape[0]
    pieces = []
    for v in vals:
        hi = v.astype(BF16).astype(F32)
        r1 = v - hi
        mid = r1.astype(BF16).astype(F32)
        pieces += [hi, mid, r1 - mid]
    lhs = jnp.concatenate(pieces, axis=0).astype(BF16)
    out = jnp.dot(lhs, e_ref[...], preferred_element_type=F32)
    return [out[(3 * i) * q:(3 * i + 1) * q] + out[(3 * i + 1) * q:(3 * i + 2) * q] + out[(3 * i + 2) * q:(3 * i + 3) * q]
            for i in range(len(vals))]


def _ssd_kernel(*refs, q, nb, n_groups, heads_per_group, head_dim, n_state, from_state, conv_k):
    if from_state:
        (xbc_ref, prev_ref, z_ref, dt_ref, h0_ref, cw_ref, cb_ref, dtb_ref, alog_ref, dexp_ref, norm_ref, e_ref,
         *_, y_ref, hout_ref, ext_ref, act_ref, yacc_ref) = refs
        params = (dtb_ref, alog_ref, dexp_ref, norm_ref, e_ref)
        hist = conv_k - 1
        for n in range(nb):
            ext = ext_ref.at[n]
            ext[0:CONV_HALO, :] = jnp.zeros((CONV_HALO, ext.shape[1]), F32)
            ext[CONV_HALO - hist:CONV_HALO, :] = prev_ref[0, n]
            ext[CONV_HALO:CONV_HALO + q, :] = xbc_ref[n]
            act_ref[n] = _silu(_causal_conv(ext[...], cw_ref, q, 0) + cb_ref[...])
        for n in range(nb):
            _ssd_chunk(act_ref.at[n], z_ref.at[n], dt_ref.at[n], h0_ref.at[0, n], hout_ref.at[0, n], y_ref.at[n],
                       yacc_ref.at[n], params, q=q, n_groups=n_groups, heads_per_group=heads_per_group,
                       head_dim=head_dim, n_state=n_state)
    else:
        (xbc_ref, z_ref, dt_ref, dtb_ref, alog_ref, dexp_ref, norm_ref, e_ref,
         y_ref, hout_ref, h_ref, yacc_ref) = refs
        params = (dtb_ref, alog_ref, dexp_ref, norm_ref, e_ref)
        c = pl.program_id(1)

        @pl.when(c == 0)
        def _():
            h_ref[...] = jnp.zeros_like(h_ref)

        _ssd_chunk(xbc_ref.at[0], z_ref.at[0], dt_ref.at[0], h_ref, h_ref, y_ref.at[0], yacc_ref.at[0], params,
                   q=q, n_groups=n_groups, heads_per_group=heads_per_group, head_dim=head_dim, n_state=n_state)

        @pl.when(c == pl.num_programs(1) - 1)
        def _():
            hout_ref[0] = h_ref[...]


def _ssd_chunk(act_ref, z_ref, dt_ref, h_in, h_out, y_ref, yacc_ref, params,
               *, q, n_groups, heads_per_group, head_dim, n_state):
    dtb_ref, alog_ref, dexp_ref, norm_ref, e_ref = params
    gw = heads_per_group * head_dim
    inner = n_groups * gw

    dt = jax.nn.softplus(dt_ref[...] + dtb_ref[...])
    la_cs = _cumsum_rows(dt * (-jnp.exp(alog_ref[...])))
    la_cs_t = la_cs.T
    dt_x, la_x = _expand_heads([dt, la_cs], e_ref)
    from_start_x = jnp.exp(la_x)
    to_end_x = jnp.exp(la_x[q - 1:q, :] - la_x)
    chunk_decay = jnp.exp(la_cs[q - 1:q, :])

    xs = act_ref[:, 0:inner]
    xd = xs * dt_x
    xdw = xd * to_end_x
    yacc_ref[...] = dexp_ref[...] * xs

    tri = lax.broadcasted_iota(jnp.int32, (q, q), 0) >= lax.broadcasted_iota(jnp.int32, (q, q), 1)
    assert LANES % head_dim == 0
    hpl = LANES // head_dim
    lane_head = lax.broadcasted_iota(jnp.int32, (q, LANES), 1) // head_dim
    bms, cbs = [], []
    for g in range(n_groups):
        gs = slice(g * gw, (g + 1) * gw)
        bm = act_ref[:, inner + g * n_state:inner + (g + 1) * n_state].astype(BF16)
        cm = act_ref[:, inner + (n_groups + g) * n_state:inner + (n_groups + g + 1) * n_state].astype(BF16)
        bms.append(bm)
        cbs.append(lax.dot_general(cm, bm, (((1,), (1,)), ((), ())), preferred_element_type=F32))
        y_off = lax.dot_general(cm, h_in[g].astype(BF16), (((1,), (1,)), ((), ())),
                                preferred_element_type=F32)
        yacc_ref[:, gs] += y_off * from_start_x[:, gs]
    for g in range(n_groups):
        cb = cbs[g]
        for lt in range(gw // LANES):
            ls = slice(g * gw + lt * LANES, g * gw + (lt + 1) * LANES)
            xd_t = xd[:, ls]
            y_t = None
            for k in range(hpl):
                hd = (g * gw + lt * LANES) // head_dim + k
                seg = la_cs[:, hd:hd + 1] - la_cs_t[hd:hd + 1, :]
                m = (cb * jnp.exp(jnp.where(tri, seg, -jnp.inf))).astype(BF16)
                rhs = jnp.where(lane_head == k, xd_t, 0.0).astype(BF16)
                part = jnp.dot(m, rhs, preferred_element_type=F32)
                y_t = part if y_t is None else y_t + part
            yacc_ref[:, ls] += y_t
    for g in range(n_groups):
        gs = slice(g * gw, (g + 1) * gw)
        s_new = lax.dot_general(xdw[:, gs].astype(BF16), bms[g], (((0,), (0,)), ((), ())),
                                preferred_element_type=F32)
        for r in range(heads_per_group):
            hd = g * heads_per_group + r
            rows = slice(r * head_dim, (r + 1) * head_dim)
            h_out[g, rows, :] = h_in[g, rows, :] * chunk_decay[:, hd:hd + 1] + s_new[rows, :]

    y = yacc_ref[...] * _silu(z_ref[...])
    for g in range(n_groups):
        gs = slice(g * gw, (g + 1) * gw)
        yg = y[:, gs]
        ms = jnp.mean(yg * yg, axis=-1, keepdims=True)
        y_ref[:, gs] = (yg * lax.rsqrt(ms + EPS) * norm_ref[:, gs]).astype(y_ref.dtype)


def ssd_branch(xbc3, main3, dt3, conv_state, ssm_state, state_out, cw, cb, dt_bias, a_log, d_exp, norm, expand,
               *, layer, z_col, q, nb, n_groups, heads_per_group, head_dim, n_state, out_dtype):
    b, s, conv_dim = xbc3.shape
    gw = heads_per_group * head_dim
    inner = n_groups * gw
    assert conv_dim == inner + 2 * n_groups * n_state
    conv_k = cw.shape[0]
    from_state = ssm_state is not None
    zb = z_col // inner
    assert z_col % inner == 0

    def const(shape):
        nd = len(shape)
        return pl.BlockSpec(shape, lambda n, c: (0,) * nd)

    assert b % nb == 0 and (from_state or nb == 1)
    in_specs = [pl.BlockSpec((nb, q, conv_dim), lambda n, c: (n, c, 0))]
    args = [xbc3]
    if from_state:
        assert q == s
        in_specs.append(pl.BlockSpec((1, nb, conv_k - 1, conv_dim), lambda n, c: (layer, n, 0, 0)))
        args.append(conv_state)
    in_specs += [pl.BlockSpec((nb, q, inner), lambda n, c: (n, c, zb)),
                 pl.BlockSpec((nb, q, LANES), lambda n, c: (n, c, 0))]
    args += [main3, dt3]
    if from_state:
        in_specs.append(pl.BlockSpec((1, nb, n_groups, gw, n_state), lambda n, c: (layer, n, 0, 0, 0)))
        args.append(ssm_state)
        small = [cw, cb.reshape(1, conv_dim)]
    else:
        small = []
    small += [dt_bias, a_log, d_exp, norm.reshape(1, inner), expand]
    in_specs += [const(a.shape) for a in small]
    args += small
    aliases = {}
    if from_state:
        state_spec = pl.BlockSpec((1, nb, n_groups, gw, n_state), lambda n, c: (layer, n, 0, 0, 0))
        state_shape = jax.ShapeDtypeStruct(ssm_state.shape, F32)
        if state_out is not None:
            aliases = {len(args): 1}
            in_specs.append(pl.BlockSpec(memory_space=pl.ANY))
            args.append(state_out)
    else:
        state_spec = pl.BlockSpec((1, n_groups, gw, n_state), lambda n, c: (n, 0, 0, 0))
        state_shape = jax.ShapeDtypeStruct((b, n_groups, gw, n_state), F32)
    return pl.pallas_call(
        functools.partial(_ssd_kernel, q=q, nb=nb, n_groups=n_groups, heads_per_group=heads_per_group,
                          head_dim=head_dim, n_state=n_state, from_state=from_state, conv_k=conv_k),
        grid=(b // nb, s // q),
        in_specs=in_specs,
        out_specs=[pl.BlockSpec((nb, q, inner), lambda n, c: (n, c, 0)), state_spec],
        out_shape=[jax.ShapeDtypeStruct((b, s, inner), out_dtype), state_shape],
        input_output_aliases=aliases,
        scratch_shapes=([pltpu.VMEM((nb, CONV_HALO + q, conv_dim), F32),
                         pltpu.VMEM((nb, q, conv_dim), F32)]
                        if from_state else
                        [pltpu.VMEM((n_groups, gw, n_state), F32)])
                       + [pltpu.VMEM((nb, q, inner), F32)],
        compiler_params=_cparams("parallel", "arbitrary"),
        name="ssd_state" if from_state else "ssd_scan",
    )(*args)


def _kv_head(ref, n, h, d, stage_ref, slot):
    if len(ref.shape) == 5:
        stage_ref[slot] = ref[0, n, :, h, :]
        return stage_ref[slot].astype(BF16)
    return ref[0, n, :, h * d:(h + 1) * d].astype(BF16)


def _attn_kernel(q_ref, k_ref, v_ref, o_ref, *stage, n_heads, nb):
    stage_ref = stage[0] if stage else None
    d = q_ref.shape[-1] // n_heads
    inv = 1.0 / math.sqrt(d)
    units = [(n, h, slice(h * d, (h + 1) * d)) for n in range(nb) for h in range(n_heads)]
    scores = []
    for u, (n, h, hs) in enumerate(units):
        qh = q_ref[n, :, hs].astype(BF16)
        kh = _kv_head(k_ref, n, h, d, stage_ref, 2 * u)
        scores.append(lax.dot_general(qh, kh, (((1,), (1,)), ((), ())), preferred_element_type=F32) * inv)
    probs = []
    for s in scores:
        e = jnp.exp(s - jnp.max(s, axis=-1, keepdims=True))
        probs.append((e / jnp.sum(e, axis=-1, keepdims=True)).astype(BF16))
    for u, ((n, h, hs), p) in enumerate(zip(units, probs)):
        vh = _kv_head(v_ref, n, h, d, stage_ref, 2 * u + 1)
        o_ref[n, :, hs] = jnp.dot(p, vh, preferred_element_type=F32).astype(o_ref.dtype)


def _attn_few_kernel(q_ref, k_ref, v_ref, o_ref, *stage, n_heads, nb):
    stage_ref = stage[0] if stage else None
    tq = q_ref.shape[1]
    d = q_ref.shape[-1] // n_heads
    inv = 1.0 / math.sqrt(d)
    pad = jnp.zeros((LANES - tq, d), F32)
    units = [(n, h, slice(h * d, (h + 1) * d)) for n in range(nb) for h in range(n_heads)]
    scores_t = []
    for u, (n, h, hs) in enumerate(units):
        qh = jnp.concatenate([q_ref[n, :, hs].astype(F32), pad], axis=0).astype(BF16)
        kh = _kv_head(k_ref, n, h, d, stage_ref, 2 * u)
        scores_t.append(lax.dot_general(kh, qh, (((1,), (1,)), ((), ())), preferred_element_type=F32) * inv)
    probs = []
    for st in scores_t:
        e = jnp.exp(st - jnp.max(st, axis=0, keepdims=True))
        pt = e / jnp.sum(e, axis=0, keepdims=True)
        probs.append(pt.T[0:tq, :].astype(BF16))
    for u, ((n, h, hs), p) in enumerate(zip(units, probs)):
        vh = _kv_head(v_ref, n, h, d, stage_ref, 2 * u + 1)
        o_ref[n, :, hs] = jnp.dot(p, vh, preferred_element_type=F32).astype(o_ref.dtype)


def cross_attention(q3, k4, v4, *, layer, nb, tq, n_heads, out_dtype):
    b, s, dm = q3.shape
    m = k4.shape[2]
    body = _attn_few_kernel if tq < LANES else _attn_kernel
    zeros = (0,) * (k4.ndim - 2)
    kv_spec = pl.BlockSpec((1, nb) + k4.shape[2:], lambda n, i: (layer, n) + zeros)
    stage = [pltpu.VMEM((2 * nb * n_heads, m, dm // n_heads), F32)] if k4.ndim == 5 else []
    return pl.pallas_call(
        functools.partial(body, n_heads=n_heads, nb=nb),
        grid=(b // nb, s // tq),
        in_specs=[pl.BlockSpec((nb, tq, dm), lambda n, i: (n, i, 0)), kv_spec, kv_spec],
        out_specs=pl.BlockSpec((nb, tq, dm), lambda n, i: (n, i, 0)),
        out_shape=jax.ShapeDtypeStruct((b, s, dm), out_dtype),
        scratch_shapes=stage,
        compiler_params=_cparams("parallel", "arbitrary"),
        name="cross_attention",
    )(q3, k4, v4)


def _ffn_mid_kernel(g_ref, v_ref, gp_ref, vp_ref, gw_ref, vw_ref, gb_ref, vb_ref, o_ref, gext_ref, vext_ref,
                    *, L, from_state, conv_k):
    hist = conv_k - 1
    i = pl.program_id(1)

    def conv(cur_ref, prev_ref, w_ref, b_ref, ext_ref):
        if from_state:
            ext_ref[:, 0:CONV_HALO, :] = jnp.zeros((ext_ref.shape[0], CONV_HALO, ext_ref.shape[2]), F32)
            ext_ref[:, CONV_HALO - hist:CONV_HALO, :] = prev_ref[0]
        else:
            ext_ref[:, 0:CONV_HALO, :] = jnp.where(i == 0, 0.0, prev_ref[...])
        ext_ref[:, CONV_HALO:CONV_HALO + L, :] = cur_ref[...]
        return _causal_conv(ext_ref[...], w_ref, L, 1) + b_ref[...]

    g = conv(g_ref, gp_ref, gw_ref, gb_ref, gext_ref)
    v = conv(v_ref, vp_ref, vw_ref, vb_ref, vext_ref)
    o_ref[...] = (_silu(g) * v).astype(o_ref.dtype)


def ffn_mid(up3, prev_state, cw, cb, *, layer, nb, L, tc, out_dtype):
    b, s, two_f = up3.shape
    f = two_f // 2
    conv_k = cw.shape[0]
    assert f % tc == 0
    nj = f // tc
    from_state = prev_state is not None
    if from_state:
        assert L == s
        prev = prev_state
        gp_spec = pl.BlockSpec((1, nb, conv_k - 1, tc), lambda n, i, j: (layer, n, 0, j))
        vp_spec = pl.BlockSpec((1, nb, conv_k - 1, tc), lambda n, i, j: (layer, n, 0, j + nj))
    else:
        assert L % CONV_HALO == 0
        prev = up3
        r = L // CONV_HALO
        gp_spec = pl.BlockSpec((nb, CONV_HALO, tc), lambda n, i, j: (n, jnp.maximum(i * r - 1, 0), j))
        vp_spec = pl.BlockSpec((nb, CONV_HALO, tc), lambda n, i, j: (n, jnp.maximum(i * r - 1, 0), j + nj))
    cb2 = cb.reshape(1, two_f)
    return pl.pallas_call(
        functools.partial(_ffn_mid_kernel, L=L, from_state=from_state, conv_k=conv_k),
        grid=(b // nb, s // L, nj),
        in_specs=[
            pl.BlockSpec((nb, L, tc), lambda n, i, j: (n, i, j)),
            pl.BlockSpec((nb, L, tc), lambda n, i, j: (n, i, j + nj)),
            gp_spec, vp_spec,
            pl.BlockSpec((conv_k, tc), lambda n, i, j: (0, j)),
            pl.BlockSpec((conv_k, tc), lambda n, i, j: (0, j + nj)),
            pl.BlockSpec((1, tc), lambda n, i, j: (0, j)),
            pl.BlockSpec((1, tc), lambda n, i, j: (0, j + nj)),
        ],
        out_specs=pl.BlockSpec((nb, L, tc), lambda n, i, j: (n, i, j)),
        out_shape=jax.ShapeDtypeStruct((b, s, f), out_dtype),
        scratch_shapes=[pltpu.VMEM((nb, CONV_HALO + L, tc), F32), pltpu.VMEM((nb, CONV_HALO + L, tc), F32)],
        compiler_params=_cparams("parallel", "arbitrary", "arbitrary"),
        name="ffn_mid_state" if from_state else "ffn_mid_scan",
    )(up3, up3, prev, prev, cw, cw, cb2, cb2)


def _ffn_front_kernel(x_ref, g_ref, wg_ref, wv_ref, cwg_ref, cwv_ref, cbg_ref, cbv_ref,
                      act_ref, tail_g_ref, tail_v_ref, xn_ref, halo_ref, *, tiles_per_seq):
    i, j = pl.program_id(0), pl.program_id(1)
    tm = x_ref.shape[0]

    @pl.when(j == 0)
    def _():
        x = x_ref[...]
        ms = jnp.mean(x * x, axis=-1, keepdims=True)
        xn_ref[...] = (x * lax.rsqrt(ms + EPS) * g_ref[...]).astype(BF16)

    xn = xn_ref[...]
    seq_start = (i % tiles_per_seq) == 0

    def half(w_ref, cw_ref, cb_ref, tail_ref, slot):
        up = jnp.dot(xn, w_ref[...], preferred_element_type=F32)
        prev = jnp.where(seq_start, 0.0, halo_ref[slot, j])
        tail = up[tm - CONV_HALO:tm, :]
        halo_ref[slot, j] = tail
        tail_ref[0] = tail
        ext = jnp.concatenate([prev, up], axis=0)
        return _causal_conv(ext, cw_ref, tm, 0) + cb_ref[...]

    g = half(wg_ref, cwg_ref, cbg_ref, tail_g_ref, 0)
    v = half(wv_ref, cwv_ref, cbv_ref, tail_v_ref, 1)
    act_ref[...] = (_silu(g) * v).astype(act_ref.dtype)


def ffn_front(x2, gain, w_up, cw, cb, *, seq_len, tm, tn, out_dtype):
    t, k = x2.shape
    f = w_up.shape[1] // 2
    assert t % tm == 0 and f % tn == 0 and seq_len % tm == 0
    nj = f // tn
    conv_k = cw.shape[0]
    cb2 = cb.reshape(1, 2 * f)
    return pl.pallas_call(
        functools.partial(_ffn_front_kernel, tiles_per_seq=seq_len // tm),
        grid=(t // tm, nj),
        in_specs=[
            pl.BlockSpec((tm, k), lambda i, j: (i, 0)),
            pl.BlockSpec((1, k), lambda i, j: (0, 0)),
            pl.BlockSpec((k, tn), lambda i, j: (0, j)),
            pl.BlockSpec((k, tn), lambda i, j: (0, j + nj)),
            pl.BlockSpec((conv_k, tn), lambda i, j: (0, j)),
            pl.BlockSpec((conv_k, tn), lambda i, j: (0, j + nj)),
            pl.BlockSpec((1, tn), lambda i, j: (0, j)),
            pl.BlockSpec((1, tn), lambda i, j: (0, j + nj)),
        ],
        out_specs=[pl.BlockSpec((tm, tn), lambda i, j: (i, j)),
                   pl.BlockSpec((1, CONV_HALO, tn), lambda i, j: (i, 0, j)),
                   pl.BlockSpec((1, CONV_HALO, tn), lambda i, j: (i, 0, j))],
        out_shape=[jax.ShapeDtypeStruct((t, f), out_dtype),
                   jax.ShapeDtypeStruct((t // tm, CONV_HALO, f), F32),
                   jax.ShapeDtypeStruct((t // tm, CONV_HALO, f), F32)],
        scratch_shapes=[pltpu.VMEM((tm, k), BF16), pltpu.VMEM((2, nj, CONV_HALO, tn), F32)],
        compiler_params=_cparams("arbitrary", "arbitrary"),
        name="ffn_front",
    )(x2, gain.reshape(1, k), w_up, w_up, cw, cw, cb2, cb2)


def _norm_matmul_conv_kernel(x_ref, g_ref, w_ref, cw_ref, cb_ref, act_ref, tail_ref, xn_ref, halo_ref,
                             *, tiles_per_seq):
    i, j = pl.program_id(0), pl.program_id(1)
    tm = x_ref.shape[0]

    @pl.when(j == 0)
    def _():
        x = x_ref[...]
        ms = jnp.mean(x * x, axis=-1, keepdims=True)
        xn_ref[...] = (x * lax.rsqrt(ms + EPS) * g_ref[...]).astype(BF16)

    up = jnp.dot(xn_ref[...], w_ref[...], preferred_element_type=F32)
    prev = jnp.where((i % tiles_per_seq) == 0, 0.0, halo_ref[j])
    tail = up[tm - CONV_HALO:tm, :]
    halo_ref[j] = tail
    tail_ref[0] = tail
    ext = jnp.concatenate([prev, up], axis=0)
    act_ref[...] = _silu(_causal_conv(ext, cw_ref, tm, 0) + cb_ref[...])


def norm_matmul_conv(x2, gain, w, cw, cb, *, seq_len, tm, tn):
    t, k = x2.shape
    c = w.shape[1]
    assert t % tm == 0 and c % tn == 0 and seq_len % tm == 0
    nj = c // tn
    conv_k = cw.shape[0]
    return pl.pallas_call(
        functools.partial(_norm_matmul_conv_kernel, tiles_per_seq=seq_len // tm),
        grid=(t // tm, nj),
        in_specs=[
            pl.BlockSpec((tm, k), lambda i, j: (i, 0)),
            pl.BlockSpec((1, k), lambda i, j: (0, 0)),
            pl.BlockSpec((k, tn), lambda i, j: (0, j)),
            pl.BlockSpec((conv_k, tn), lambda i, j: (0, j)),
            pl.BlockSpec((1, tn), lambda i, j: (0, j)),
        ],
        out_specs=[pl.BlockSpec((tm, tn), lambda i, j: (i, j)),
                   pl.BlockSpec((1, CONV_HALO, tn), lambda i, j: (i, 0, j))],
        out_shape=[jax.ShapeDtypeStruct((t, c), F32),
                   jax.ShapeDtypeStruct((t // tm, CONV_HALO, c), F32)],
        scratch_shapes=[pltpu.VMEM((tm, k), BF16), pltpu.VMEM((nj, CONV_HALO, tn), F32)],
        compiler_params=_cparams("arbitrary", "arbitrary"),
        name="in_proj_conv",
    )(x2, gain.reshape(1, k), w, cw, cb.reshape(1, c))


def _history_tail(prev, cur, hist):
    s = cur.shape[1]
    if s >= hist:
        return cur[:, s - hist:]
    return jnp.concatenate([prev[:, s:], cur], axis=1)


def _layer(x3, lw, dims, states, ssm_out, mem_kv, *, layer, kv_layer, pos0, tiles):
    b, s, d = x3.shape
    t = b * s
    x2 = x3.reshape(t, d)
    G, R, P, N = dims["groups"], dims["heads_per_group"], dims["head_dim"], dims["n_state"]
    inner = G * R * P
    conv_dim = inner + 2 * G * N
    z_col, u_col, gp_col, gs_col = 0, inner, inner + d, inner + 2 * d
    n_main = gs_col + d
    tm, tn, tn_k = tiles["tm"], tiles["tn"], tiles["tn_long_k"]
    act_dtype = tiles["act_dtype"]
    hc = lw["ssm_conv_w"].shape[0] - 1

    if states is None:
        pool_prev = conv_prev = ssm_prev = ffn_prev = None
    else:
        pool_prev, conv_prev, ssm_prev, ffn_prev = states

    main, dt = norm_matmul(x2, lw["norm_mix"], lw["w_main"], lw["w_dt"], tm=tm, tn=tn, name="in_proj")
    main3 = main.reshape(b, s, n_main)
    dt3 = dt.reshape(b, s, LANES)
    if states is None:
        xbc, xbc_tail = norm_matmul_conv(x2, lw["norm_mix"], lw["w_xbc"], lw["ssm_conv_w"], lw["ssm_conv_b"],
                                         seq_len=s, tm=tm, tn=tiles["ffn_tc"])
        per_seq = s // tm
        conv_new = xbc_tail[per_seq - 1::per_seq, CONV_HALO - hc:]
        xbc3 = xbc.reshape(b, s, conv_dim)
    else:
        xbc3 = norm_matmul(x2, lw["norm_mix"], lw["w_xbc"], tm=tm, tn=tn, name="in_proj_xbc").reshape(b, s, conv_dim)
        conv_new = _history_tail(conv_prev[layer], xbc3, hc)

    gated_pool = pool_branch(main3, pool_prev, lw["w_pool_group"], lw["pool_scale"], lw["w_pool_out"], layer=layer,
                             u_col=u_col, gate_col=gp_col, nb=tiles["pool_nb"], L=tiles["pool_L"], pos0=pos0)
    y, ssm_new = ssd_branch(xbc3, main3, dt3, conv_prev, ssm_prev, ssm_out, lw["ssm_conv_w"], lw["ssm_conv_b"],
                            lw["dt_bias"], lw["a_log"], lw["d_exp"], lw["ssm_norm"], lw["expand"], layer=layer,
                            z_col=z_col, q=tiles["ssd_q"], nb=tiles["ssd_nb"], n_groups=G, heads_per_group=R,
                            head_dim=P, n_state=N, out_dtype=act_dtype)
    merged = matmul(y.reshape(t, inner), lw["w_ssm_out"],
                    [(gated_pool.reshape(t, d), 0), (main, gs_col)], _ep_gate_merge,
                    tm=tm, tn=tn_k, out_dtype=BF16, name="ssm_out_merge")
    x2 = matmul(merged, lw["w_out"], [(x2, 0)], _ep_residual, tm=tm, tn=tn, out_dtype=F32, name="mix_out")

    qm = norm_matmul(x2, lw["norm_mem_q"], lw["w_mem_q"], tm=tm, tn=tn, out_dtype=act_dtype, name="mem_q")
    k4, v4 = mem_kv
    o = cross_attention(qm.reshape(b, s, d), k4, v4, layer=kv_layer, nb=tiles["attn_nb"], tq=tiles["attn_tq"],
                        n_heads=dims["mem_heads"], out_dtype=act_dtype)
    x2 = matmul(o.reshape(t, d), lw["w_mem_o"], [(x2, 0)], _ep_residual, tm=tm, tn=tn, out_dtype=F32, name="mem_o")

    hf = lw["ffn_conv_w"].shape[0] - 1
    if states is None:
        act2, tail_g, tail_v = ffn_front(x2, lw["norm_ffn"], lw["w_ffn_up"], lw["ffn_conv_w"], lw["ffn_conv_b"],
                                         seq_len=s, tm=tm, tn=tiles["ffn_tc"], out_dtype=act_dtype)
        per_seq = s // tm
        ffn_new = jnp.concatenate([tail_g[per_seq - 1::per_seq, CONV_HALO - hf:],
                                   tail_v[per_seq - 1::per_seq, CONV_HALO - hf:]], axis=-1)
    else:
        up = norm_matmul(x2, lw["norm_ffn"], lw["w_ffn_up"], tm=tm, tn=tn, name="ffn_up")
        up3 = up.reshape(b, s, up.shape[1])
        act2 = ffn_mid(up3, ffn_prev, lw["ffn_conv_w"], lw["ffn_conv_b"], layer=layer, nb=tiles["ffn_nb"],
                       L=tiles["ffn_L"], tc=tiles["ffn_tc"], out_dtype=act_dtype).reshape(t, -1)
    x2 = matmul(act2, lw["w_ffn_down"], [(x2, 0)], _ep_residual,
                tm=tiles["tm_down"], tn=tn_k, out_dtype=F32, name="ffn_down")

    u3 = main3[:, :, u_col:u_col + d]
    hp = POOL_HALO - 1
    if states is None:
        pool_new = u3[:, s - hp:]
    else:
        pool_new = _history_tail(pool_prev[layer], u3, hp)
        ffn_new = _history_tail(ffn_prev[layer], up3, hf)
    return x2.reshape(b, s, d), pool_new, conv_new, ssm_new, ffn_new


def kernel(x_prompt, x_sample, state_pool, state_ssm_conv, state_ssm, state_ffn_conv, cache_mem_k, cache_mem_v,
           mem_prompt, norm_mix, w_in, w_pool_group, pool_scale, w_pool_out, ssm_conv_w, ssm_conv_b, ssm_dt_bias,
           ssm_a_log, ssm_d, ssm_norm, w_ssm_out, w_out, norm_mem_q, w_mem_q, w_mem_o, norm_mem_kv, w_mem_k,
           w_mem_v, norm_ffn, w_ffn_up, ffn_conv_w, ffn_conv_b, w_ffn_down, norm_final):
    depth = w_in.shape[0]
    bp, sp, d = x_prompt.shape
    bs, ss, _ = x_sample.shape
    n_heads = ssm_d.shape[1]
    inner = w_ssm_out.shape[1]
    head_dim = inner // n_heads
    n_state = state_ssm.shape[-1]
    conv_dim = ssm_conv_w.shape[2]
    n_groups = (conv_dim - inner) // (2 * n_state)
    mem_heads = cache_mem_k.shape[3]
    mem_len = mem_prompt.shape[1]
    dims = dict(groups=n_groups, heads_per_group=n_heads // n_groups, head_dim=head_dim, n_state=n_state,
                mem_heads=mem_heads)
    assert n_heads <= LANES

    c_u, c_z, c_x, c_dt, c_gp = d, d + inner, d + inner + conv_dim, d + inner + conv_dim + n_heads, 2 * d + inner + conv_dim + n_heads
    head_of_col = jnp.arange(inner, dtype=jnp.int32) // head_dim
    expand = (jnp.arange(LANES, dtype=jnp.int32)[:, None] == head_of_col[None, :]).astype(BF16)

    def pad_heads(v):
        return jnp.pad(v, (0, LANES - n_heads)).reshape(1, LANES)

    tiles_p = dict(tm=1024, tm_down=1024, tn=1024, tn_long_k=512, pool_nb=1, pool_L=512, ssd_q=128, ssd_nb=1,
                   attn_nb=1, attn_tq=512, ffn_nb=1, ffn_L=512, ffn_tc=512, act_dtype=BF16)
    tiles_s = dict(tm=512, tm_down=512, tn=1024, tn_long_k=512, pool_nb=32, pool_L=ss, ssd_q=ss, ssd_nb=2,
                   attn_nb=2, attn_tq=ss, ffn_nb=64, ffn_L=ss, ffn_tc=512, act_dtype=F32)

    ssm_states = state_ssm.reshape(depth, bs, n_groups, inner // n_groups, n_state)
    kv_s = (cache_mem_k, cache_mem_v)
    states_s = (state_pool, state_ssm_conv, ssm_states, state_ffn_conv)
    ssm_s_all = None

    yp, ys = x_prompt, x_sample
    outs = [[] for _ in range(9)]
    for i in range(depth):
        wi = w_in[i]
        lw = dict(
            norm_mix=norm_mix[i],
            w_main=jnp.concatenate([wi[:, c_u:c_z], wi[:, :c_u], wi[:, c_dt:]], axis=1).astype(BF16),
            w_xbc=wi[:, c_z:c_x].astype(BF16),
            w_dt=jnp.pad(wi[:, c_x:c_dt], ((0, 0), (0, LANES - n_heads))).astype(BF16),
            w_pool_group=w_pool_group[i].astype(BF16), pool_scale=pool_scale[i],
            w_pool_out=w_pool_out[i].astype(BF16),
            ssm_conv_w=ssm_conv_w[i], ssm_conv_b=ssm_conv_b[i],
            dt_bias=pad_heads(ssm_dt_bias[i]), a_log=pad_heads(ssm_a_log[i]),
            d_exp=jnp.repeat(ssm_d[i], head_dim).reshape(1, inner), ssm_norm=ssm_norm[i], expand=expand,
            w_ssm_out=w_ssm_out[i].astype(BF16), w_out=w_out[i].astype(BF16),
            norm_mem_q=norm_mem_q[i], w_mem_q=w_mem_q[i].astype(BF16), w_mem_o=w_mem_o[i].astype(BF16),
            norm_ffn=norm_ffn[i], w_ffn_up=w_ffn_up[i].astype(BF16),
            ffn_conv_w=ffn_conv_w[i], ffn_conv_b=ffn_conv_b[i], w_ffn_down=w_ffn_down[i].astype(BF16),
        )
        mem2 = mem_prompt.reshape(bp * mem_len, d)
        k_i = norm_matmul(mem2, norm_mem_kv[i], w_mem_k[i].astype(BF16), tm=512, tn=1024, name="mem_k")
        v_i = norm_matmul(mem2, norm_mem_kv[i], w_mem_v[i].astype(BF16), tm=512, tn=1024, name="mem_v")
        kv_p = (k_i.reshape(1, bp, mem_len, d), v_i.reshape(1, bp, mem_len, d))

        yp, a0, a1, a2, a3 = _layer(yp, lw, dims, None, None, kv_p, layer=0, kv_layer=0, pos0=0, tiles=tiles_p)
        ys, b0, b1, ssm_s_all, b3 = _layer(ys, lw, dims, states_s, ssm_s_all, kv_s, layer=i, kv_layer=i,
                                           pos0=PAST_LEN, tiles=tiles_s)
        for lst, val in zip(outs, (a0, b0, a1, b1, a2.reshape(bp, n_heads, head_dim, n_state), a3, b3,
                                   k_i.reshape(bp, mem_len, mem_heads, d // mem_heads),
                                   v_i.reshape(bp, mem_len, mem_heads, d // mem_heads))):
            lst.append(val)

    y_prompt = rmsnorm(yp.reshape(bp * sp, d), norm_final, tm=512).reshape(bp, sp, d)
    y_sample = rmsnorm(ys.reshape(bs * ss, d), norm_final, tm=512).reshape(bs, ss, d)
    pool_p, pool_s, conv_p, conv_s, ssm_p, ffn_p, ffn_s, mk_p, mv_p = (jnp.stack(lst) for lst in outs)
    ssm_s = ssm_s_all.reshape(depth, bs, n_heads, head_dim, n_state)
    return (y_prompt, y_sample, pool_p, pool_s, conv_p, conv_s, ssm_p, ssm_s, ffn_p, ffn_s, mk_p, mv_p)
```

```python
import functools
import math

import jax
import jax.numpy as jnp
from jax import lax
from jax.experimental import pallas as pl
from jax.experimental.pallas import tpu as pltpu

F32 = jnp.float32
BF16 = jnp.bfloat16
EPS = 1e-6

LANES = 128
SUBLANES = 8
VMEM_LIMIT = 56 * 1024 * 1024

PAST_LEN = 16384
POOL_WINDOWS = (2, 4, 8, 16)
POOL_HALO = 16
CONV_HALO = SUBLANES


def _cparams(*sem):
    return pltpu.CompilerParams(dimension_semantics=sem, vmem_limit_bytes=VMEM_LIMIT)


def _silu(x):
    h = 0.5 * x
    return h + h * jnp.tanh(h)


def _causal_conv(ext, w_ref, rows, axis):
    conv_k = w_ref.shape[0]
    tail = (slice(None),) * axis + (slice(CONV_HALO, CONV_HALO + rows),)
    acc = ext[tail] * w_ref[conv_k - 1:conv_k, :]
    for j in range(conv_k - 1):
        acc = acc + pltpu.roll(ext, conv_k - 1 - j, axis=axis)[tail] * w_ref[j:j + 1, :]
    return acc


def _norm_matmul_kernel(x_ref, g_ref, w_ref, *rest, has_aux):
    if has_aux:
        wa_ref, o_ref, oa_ref, xn_ref = rest
    else:
        o_ref, xn_ref = rest

    @pl.when(pl.program_id(1) == 0)
    def _():
        x = x_ref[...]
        ms = jnp.mean(x * x, axis=-1, keepdims=True)
        xn = (x * lax.rsqrt(ms + EPS) * g_ref[...]).astype(BF16)
        xn_ref[...] = xn
        if has_aux:
            oa_ref[...] = jnp.dot(xn, wa_ref[0], preferred_element_type=F32)

    o_ref[...] = jnp.dot(xn_ref[...], w_ref[0], preferred_element_type=F32).astype(o_ref.dtype)


def norm_matmul(x, g, w, w_aux=None, *, layer, tm, tn, out_dtype=F32, name):
    t, k = x.shape
    n = w.shape[2]
    assert t % tm == 0 and n % tn == 0
    has_aux = w_aux is not None
    in_specs = [
        pl.BlockSpec((tm, k), lambda i, j: (i, 0)),
        pl.BlockSpec((1, k), lambda i, j: (0, 0)),
        pl.BlockSpec((1, k, tn), lambda i, j: (layer, 0, j)),
    ]
    args = [x, g.reshape(1, k), w]
    out_shape = [jax.ShapeDtypeStruct((t, n), out_dtype)]
    out_specs = [pl.BlockSpec((tm, tn), lambda i, j: (i, j))]
    if has_aux:
        na = w_aux.shape[2]
        in_specs.append(pl.BlockSpec((1, k, na), lambda i, j: (layer, 0, 0)))
        args.append(w_aux)
        out_shape.append(jax.ShapeDtypeStruct((t, na), F32))
        out_specs.append(pl.BlockSpec((tm, na), lambda i, j: (i, 0)))
    res = pl.pallas_call(
        functools.partial(_norm_matmul_kernel, has_aux=has_aux),
        grid=(t // tm, n // tn),
        in_specs=in_specs,
        out_specs=out_specs,
        out_shape=out_shape,
        scratch_shapes=[pltpu.VMEM((tm, k), BF16)],
        compiler_params=_cparams("parallel", "arbitrary"),
        name=name,
    )(*args)
    return res if has_aux else res[0]


def _matmul_kernel(a_ref, w_ref, *rest, epilogue):
    *extra, o_ref = rest
    acc = jnp.dot(a_ref[...].astype(BF16), w_ref[0], preferred_element_type=F32)
    o_ref[...] = epilogue(acc, *[e[...] for e in extra]).astype(o_ref.dtype)


def _ep_residual(acc, r):
    return r + acc


def _ep_gate_merge(acc, gated_pool, g_ssm):
    return gated_pool + jax.nn.sigmoid(g_ssm) * acc


def matmul(a, w, extras, epilogue, *, layer, tm, tn, out_dtype, name):
    t, k = a.shape
    n = w.shape[2]
    assert t % tm == 0 and n % tn == 0
    in_specs = [
        pl.BlockSpec((tm, k), lambda i, j: (i, 0)),
        pl.BlockSpec((1, k, tn), lambda i, j: (layer, 0, j)),
    ]
    args = [a, w]
    for arr, off in extras:
        assert off % tn == 0
        ob = off // tn
        in_specs.append(pl.BlockSpec((tm, tn), lambda i, j, ob=ob: (i, j + ob)))
        args.append(arr)
    return pl.pallas_call(
        functools.partial(_matmul_kernel, epilogue=epilogue),
        grid=(t // tm, n // tn),
        in_specs=in_specs,
        out_specs=pl.BlockSpec((tm, tn), lambda i, j: (i, j)),
        out_shape=jax.ShapeDtypeStruct((t, n), out_dtype),
        compiler_params=_cparams("parallel", "arbitrary"),
        name=name,
    )(*args)


def _rmsnorm_kernel(x_ref, g_ref, o_ref):
    x = x_ref[...]
    ms = jnp.mean(x * x, axis=-1, keepdims=True)
    o_ref[...] = x * lax.rsqrt(ms + EPS) * g_ref[...]


def rmsnorm(x, g, *, tm):
    t, k = x.shape
    return pl.pallas_call(
        _rmsnorm_kernel,
        grid=(t // tm,),
        in_specs=[pl.BlockSpec((tm, k), lambda i: (i, 0)), pl.BlockSpec((1, k), lambda i: (0, 0))],
        out_specs=pl.BlockSpec((tm, k), lambda i: (i, 0)),
        out_shape=jax.ShapeDtypeStruct((t, k), F32),
        compiler_params=_cparams("parallel"),
    )(x, g.reshape(1, k))


def _pool_kernel(cur_ref, prev_ref, gate_ref, wg_ref, scale_ref, wo_ref, o_ref, ext_ref, pooled_ref,
                 *, nb, L, from_state, pos0):
    w_ch = cur_ref.shape[-1]
    gdim = w_ch // len(POOL_WINDOWS)
    if from_state:
        ext_ref[:, 1:POOL_HALO, :] = prev_ref[0]
        pos_start = pos0
    else:
        i = pl.program_id(1)
        ext_ref[:, 0:POOL_HALO, :] = jnp.where(i == 0, 0.0, prev_ref[...])
        pos_start = pos0 + i * L
    ext_ref[:, POOL_HALO:POOL_HALO + L, :] = cur_ref[...]

    pos = pos_start + lax.broadcasted_iota(jnp.int32, (1, L, gdim), 1)
    for k, win in enumerate(POOL_WINDOWS):
        cs = slice(k * gdim, (k + 1) * gdim)
        cur = ext_ref[:, POOL_HALO:POOL_HALO + L, cs]
        acc = cur
        for j in range(1, win):
            acc = acc + ext_ref[:, POOL_HALO - j:POOL_HALO - j + L, cs]
        count = jnp.minimum(pos + 1, win).astype(F32)
        diff = (acc / count - cur).reshape(nb * L, gdim).astype(BF16)
        mixed = jnp.dot(diff, wg_ref[0, k], preferred_element_type=F32)
        pooled_ref[:, cs] = (mixed * scale_ref[:, cs]).astype(BF16)
    out_pool = jnp.dot(pooled_ref[...], wo_ref[0], preferred_element_type=F32)
    gate = jax.nn.sigmoid(gate_ref[...].reshape(nb * L, w_ch))
    o_ref[...] = (gate * out_pool).reshape(nb, L, w_ch)


def pool_branch(main3, prev_state, wg, scale, wo, *, layer, u_col, gate_col, nb, L, pos0):
    b, s, _ = main3.shape
    w_ch = wo.shape[1]
    ub, gb = u_col // w_ch, gate_col // w_ch
    from_state = prev_state is not None
    if from_state:
        assert L == s
        prev = prev_state
        prev_spec = pl.BlockSpec((1, nb, POOL_HALO - 1, w_ch), lambda n, i: (layer, n, 0, 0))
    else:
        assert nb == 1 and L % POOL_HALO == 0
        prev = main3
        r = L // POOL_HALO
        prev_spec = pl.BlockSpec((1, POOL_HALO, w_ch), lambda n, i: (n, jnp.maximum(i * r - 1, 0), ub))
    return pl.pallas_call(
        functools.partial(_pool_kernel, nb=nb, L=L, from_state=from_state, pos0=pos0),
        grid=(b // nb, s // L),
        in_specs=[
            pl.BlockSpec((nb, L, w_ch), lambda n, i: (n, i, ub)),
            prev_spec,
            pl.BlockSpec((nb, L, w_ch), lambda n, i: (n, i, gb)),
            pl.BlockSpec((1,) + wg.shape[1:], lambda n, i: (layer, 0, 0, 0)),
            pl.BlockSpec((1, w_ch), lambda n, i: (0, 0)),
            pl.BlockSpec((1,) + wo.shape[1:], lambda n, i: (layer, 0, 0)),
        ],
        out_specs=pl.BlockSpec((nb, L, w_ch), lambda n, i: (n, i, 0)),
        out_shape=jax.ShapeDtypeStruct((b, s, w_ch), F32),
        scratch_shapes=[pltpu.VMEM((nb, POOL_HALO + L, w_ch), F32), pltpu.VMEM((nb * L, w_ch), BF16)],
        compiler_params=_cparams("parallel", "arbitrary"),
        name="pool_state" if from_state else "pool_scan",
    )(main3, prev, main3, wg, scale.reshape(1, w_ch), wo)


def _cumsum_rows(x):
    q = x.shape[0]
    row = lax.broadcasted_iota(jnp.int32, x.shape, 0)
    k = 1
    while k < q:
        x = x + jnp.where(row >= k, pltpu.roll(x, k, axis=0), 0.0)
        k *= 2
    return x


def _expand_heads(vals, e_ref):
    q = vals[0].shape[0]
    pieces = []
    for v in vals:
        hi = v.astype(BF16).astype(F32)
        r1 = v - hi
        mid = r1.astype(BF16).astype(F32)
        pieces += [hi, mid, r1 - mid]
    lhs = jnp.concatenate(pieces, axis=0).astype(BF16)
    out = jnp.dot(lhs, e_ref[...], preferred_element_type=F32)
    return [out[(3 * i) * q:(3 * i + 1) * q] + out[(3 * i + 1) * q:(3 * i + 2) * q] + out[(3 * i + 2) * q:(3 * i + 3) * q]
            for i in range(len(vals))]


def _ssd_kernel(*refs, q, nb, n_groups, heads_per_group, head_dim, n_state, from_state, conv_k):
    if from_state:
        (xbc_ref, prev_ref, z_ref, dt_ref, h0_ref, cw_ref, cb_ref, dtb_ref, alog_ref, dexp_ref, norm_ref, e_ref,
         *_, y_ref, hout_ref, ext_ref, act_ref, yacc_ref) = refs
        params = (dtb_ref, alog_ref, dexp_ref, norm_ref, e_ref)
        hist = conv_k - 1
        for n in range(nb):
            ext = ext_ref.at[n]
            ext[0:CONV_HALO, :] = jnp.zeros((CONV_HALO, ext.shape[1]), F32)
            ext[CONV_HALO - hist:CONV_HALO, :] = prev_ref[0, n]
            ext[CONV_HALO:CONV_HALO + q, :] = xbc_ref[n]
            act_ref[n] = _silu(_causal_conv(ext[...], cw_ref, q, 0) + cb_ref[...])
        for n in range(nb):
            _ssd_chunk(act_ref.at[n], z_ref.at[n], dt_ref.at[n], h0_ref.at[0, n], hout_ref.at[0, n], y_ref.at[n],
                       yacc_ref.at[n], params, q=q, n_groups=n_groups, heads_per_group=heads_per_group,
                       head_dim=head_dim, n_state=n_state)
    else:
        (xbc_ref, prev_ref, z_ref, dt_ref, cw_ref, cb_ref, dtb_ref, alog_ref, dexp_ref, norm_ref, e_ref,
         y_ref, hout_ref, ext_ref, act_ref, h_ref, yacc_ref) = refs
        params = (dtb_ref, alog_ref, dexp_ref, norm_ref, e_ref)
        c = pl.program_id(1)

        @pl.when(c == 0)
        def _():
            h_ref[...] = jnp.zeros_like(h_ref)

        ext = ext_ref.at[0]
        ext[0:CONV_HALO, :] = jnp.where(c == 0, 0.0, prev_ref[0])
        ext[CONV_HALO:CONV_HALO + q, :] = xbc_ref[0]
        act_ref[0] = _silu(_causal_conv(ext[...], cw_ref, q, 0) + cb_ref[...])
        _ssd_chunk(act_ref.at[0], z_ref.at[0], dt_ref.at[0], h_ref, h_ref, y_ref.at[0], yacc_ref.at[0], params,
                   q=q, n_groups=n_groups, heads_per_group=heads_per_group, head_dim=head_dim, n_state=n_state)

        @pl.when(c == pl.num_programs(1) - 1)
        def _():
            hout_ref[0] = h_ref[...]


def _ssd_chunk(act_ref, z_ref, dt_ref, h_in, h_out, y_ref, yacc_ref, params,
               *, q, n_groups, heads_per_group, head_dim, n_state):
    dtb_ref, alog_ref, dexp_ref, norm_ref, e_ref = params
    gw = heads_per_group * head_dim
    inner = n_groups * gw

    dt = jax.nn.softplus(dt_ref[...] + dtb_ref[...])
    la_cs = _cumsum_rows(dt * (-jnp.exp(alog_ref[...])))
    la_cs_t = la_cs.T
    dt_x, la_x = _expand_heads([dt, la_cs], e_ref)
    from_start_x = jnp.exp(la_x)
    to_end_x = jnp.exp(la_x[q - 1:q, :] - la_x)
    chunk_decay = jnp.exp(la_cs[q - 1:q, :])

    xs = act_ref[:, 0:inner]
    xd = xs * dt_x
    xdw = xd * to_end_x
    yacc_ref[...] = dexp_ref[...] * xs

    tri = lax.broadcasted_iota(jnp.int32, (q, q), 0) >= lax.broadcasted_iota(jnp.int32, (q, q), 1)
    assert LANES % head_dim == 0
    hpl = LANES // head_dim
    lane_head = lax.broadcasted_iota(jnp.int32, (q, LANES), 1) // head_dim
    bms, cbs = [], []
    for g in range(n_groups):
        gs = slice(g * gw, (g + 1) * gw)
        bm = act_ref[:, inner + g * n_state:inner + (g + 1) * n_state].astype(BF16)
        cm = act_ref[:, inner + (n_groups + g) * n_state:inner + (n_groups + g + 1) * n_state].astype(BF16)
        bms.append(bm)
        cbs.append(lax.dot_general(cm, bm, (((1,), (1,)), ((), ())), preferred_element_type=F32))
        y_off = lax.dot_general(cm, h_in[g].astype(BF16), (((1,), (1,)), ((), ())),
                                preferred_element_type=F32)
        yacc_ref[:, gs] += y_off * from_start_x[:, gs]
    for g in range(n_groups):
        cb = cbs[g]
        for lt in range(gw // LANES):
            ls = slice(g * gw + lt * LANES, g * gw + (lt + 1) * LANES)
            xd_t = xd[:, ls]
            y_t = None
            for k in range(hpl):
                hd = (g * gw + lt * LANES) // head_dim + k
                seg = la_cs[:, hd:hd + 1] - la_cs_t[hd:hd + 1, :]
                m = (cb * jnp.exp(jnp.where(tri, seg, -jnp.inf))).astype(BF16)
                rhs = jnp.where(lane_head == k, xd_t, 0.0).astype(BF16)
                part = jnp.dot(m, rhs, preferred_element_type=F32)
                y_t = part if y_t is None else y_t + part
            yacc_ref[:, ls] += y_t
    for g in range(n_groups):
        gs = slice(g * gw, (g + 1) * gw)
        s_new = lax.dot_general(xdw[:, gs].astype(BF16), bms[g], (((0,), (0,)), ((), ())),
                                preferred_element_type=F32)
        for r in range(heads_per_group):
            hd = g * heads_per_group + r
            rows = slice(r * head_dim, (r + 1) * head_dim)
            h_out[g, rows, :] = h_in[g, rows, :] * chunk_decay[:, hd:hd + 1] + s_new[rows, :]

    y = yacc_ref[...] * _silu(z_ref[...])
    for g in range(n_groups):
        gs = slice(g * gw, (g + 1) * gw)
        yg = y[:, gs]
        ms = jnp.mean(yg * yg, axis=-1, keepdims=True)
        y_ref[:, gs] = (yg * lax.rsqrt(ms + EPS) * norm_ref[:, gs]).astype(y_ref.dtype)


def ssd_branch(main3, dt3, conv_state, ssm_state, state_out, cw, cb, dt_bias, a_log, d_exp, norm, expand,
               *, layer, xbc_col, z_col, q, nb, n_groups, heads_per_group, head_dim, n_state, out_dtype):
    b, s, _ = main3.shape
    gw = heads_per_group * head_dim
    inner = n_groups * gw
    conv_dim = inner + 2 * n_groups * n_state
    conv_k = cw.shape[0]
    from_state = ssm_state is not None
    xb, zb = xbc_col // conv_dim, z_col // inner
    assert xbc_col % conv_dim == 0 and z_col % inner == 0

    def const(shape):
        nd = len(shape)
        return pl.BlockSpec(shape, lambda n, c: (0,) * nd)

    assert b % nb == 0 and (from_state or nb == 1)
    in_specs = [pl.BlockSpec((nb, q, conv_dim), lambda n, c: (n, c, xb))]
    args = [main3]
    if from_state:
        assert q == s
        in_specs.append(pl.BlockSpec((1, nb, conv_k - 1, conv_dim), lambda n, c: (layer, n, 0, 0)))
        args.append(conv_state)
    else:
        assert q % CONV_HALO == 0
        r = q // CONV_HALO
        in_specs.append(pl.BlockSpec((1, CONV_HALO, conv_dim), lambda n, c: (n, jnp.maximum(c * r - 1, 0), xb)))
        args.append(main3)
    in_specs += [pl.BlockSpec((nb, q, inner), lambda n, c: (n, c, zb)),
                 pl.BlockSpec((nb, q, LANES), lambda n, c: (n, c, 0))]
    args += [main3, dt3]
    if from_state:
        in_specs.append(pl.BlockSpec((1, nb, n_groups, gw, n_state), lambda n, c: (layer, n, 0, 0, 0)))
        args.append(ssm_state)
    small = [cw, cb.reshape(1, conv_dim), dt_bias, a_log, d_exp, norm.reshape(1, inner), expand]
    in_specs += [const(a.shape) for a in small]
    args += small
    aliases = {}
    if from_state:
        state_spec = pl.BlockSpec((1, nb, n_groups, gw, n_state), lambda n, c: (layer, n, 0, 0, 0))
        state_shape = jax.ShapeDtypeStruct(ssm_state.shape, F32)
        if state_out is not None:
            aliases = {len(args): 1}
            in_specs.append(pl.BlockSpec(memory_space=pl.ANY))
            args.append(state_out)
    else:
        state_spec = pl.BlockSpec((1, n_groups, gw, n_state), lambda n, c: (n, 0, 0, 0))
        state_shape = jax.ShapeDtypeStruct((b, n_groups, gw, n_state), F32)
    return pl.pallas_call(
        functools.partial(_ssd_kernel, q=q, nb=nb, n_groups=n_groups, heads_per_group=heads_per_group,
                          head_dim=head_dim, n_state=n_state, from_state=from_state, conv_k=conv_k),
        grid=(b // nb, s // q),
        in_specs=in_specs,
        out_specs=[pl.BlockSpec((nb, q, inner), lambda n, c: (n, c, 0)), state_spec],
        out_shape=[jax.ShapeDtypeStruct((b, s, inner), out_dtype), state_shape],
        input_output_aliases=aliases,
        scratch_shapes=[pltpu.VMEM((nb, CONV_HALO + q, conv_dim), F32),
                        pltpu.VMEM((nb, q, conv_dim), F32)]
                       + ([] if from_state else [pltpu.VMEM((n_groups, gw, n_state), F32)])
                       + [pltpu.VMEM((nb, q, inner), F32)],
        compiler_params=_cparams("parallel", "arbitrary"),
        name="ssd_state" if from_state else "ssd_scan",
    )(*args)


def _kv_head(ref, n, h, d, stage_ref, slot):
    if len(ref.shape) == 5:
        stage_ref[slot] = ref[0, n, :, h, :]
        return stage_ref[slot].astype(BF16)
    return ref[0, n, :, h * d:(h + 1) * d].astype(BF16)


def _attn_kernel(q_ref, k_ref, v_ref, o_ref, *stage, n_heads, nb):
    stage_ref = stage[0] if stage else None
    d = q_ref.shape[-1] // n_heads
    inv = 1.0 / math.sqrt(d)
    units = [(n, h, slice(h * d, (h + 1) * d)) for n in range(nb) for h in range(n_heads)]
    scores = []
    for u, (n, h, hs) in enumerate(units):
        qh = q_ref[n, :, hs].astype(BF16)
        kh = _kv_head(k_ref, n, h, d, stage_ref, 2 * u)
        scores.append(lax.dot_general(qh, kh, (((1,), (1,)), ((), ())), preferred_element_type=F32) * inv)
    probs = []
    for s in scores:
        e = jnp.exp(s - jnp.max(s, axis=-1, keepdims=True))
        probs.append((e / jnp.sum(e, axis=-1, keepdims=True)).astype(BF16))
    for u, ((n, h, hs), p) in enumerate(zip(units, probs)):
        vh = _kv_head(v_ref, n, h, d, stage_ref, 2 * u + 1)
        o_ref[n, :, hs] = jnp.dot(p, vh, preferred_element_type=F32).astype(o_ref.dtype)


def _attn_few_kernel(q_ref, k_ref, v_ref, o_ref, *stage, n_heads, nb):
    stage_ref = stage[0] if stage else None
    tq = q_ref.shape[1]
    d = q_ref.shape[-1] // n_heads
    inv = 1.0 / math.sqrt(d)
    pad = jnp.zeros((LANES - tq, d), F32)
    units = [(n, h, slice(h * d, (h + 1) * d)) for n in range(nb) for h in range(n_heads)]
    scores_t = []
    for u, (n, h, hs) in enumerate(units):
        qh = jnp.concatenate([q_ref[n, :, hs].astype(F32), pad], axis=0).astype(BF16)
        kh = _kv_head(k_ref, n, h, d, stage_ref, 2 * u)
        scores_t.append(lax.dot_general(kh, qh, (((1,), (1,)), ((), ())), preferred_element_type=F32) * inv)
    probs = []
    for st in scores_t:
        e = jnp.exp(st - jnp.max(st, axis=0, keepdims=True))
        pt = e / jnp.sum(e, axis=0, keepdims=True)
        probs.append(pt.T[0:tq, :].astype(BF16))
    for u, ((n, h, hs), p) in enumerate(zip(units, probs)):
        vh = _kv_head(v_ref, n, h, d, stage_ref, 2 * u + 1)
        o_ref[n, :, hs] = jnp.dot(p, vh, preferred_element_type=F32).astype(o_ref.dtype)


def cross_attention(q3, k4, v4, *, layer, nb, tq, n_heads, out_dtype):
    b, s, dm = q3.shape
    m = k4.shape[2]
    body = _attn_few_kernel if tq < LANES else _attn_kernel
    zeros = (0,) * (k4.ndim - 2)
    kv_spec = pl.BlockSpec((1, nb) + k4.shape[2:], lambda n, i: (layer, n) + zeros)
    stage = [pltpu.VMEM((2 * nb * n_heads, m, dm // n_heads), F32)] if k4.ndim == 5 else []
    return pl.pallas_call(
        functools.partial(body, n_heads=n_heads, nb=nb),
        grid=(b // nb, s // tq),
        in_specs=[pl.BlockSpec((nb, tq, dm), lambda n, i: (n, i, 0)), kv_spec, kv_spec],
        out_specs=pl.BlockSpec((nb, tq, dm), lambda n, i: (n, i, 0)),
        out_shape=jax.ShapeDtypeStruct((b, s, dm), out_dtype),
        scratch_shapes=stage,
        compiler_params=_cparams("parallel", "arbitrary"),
        name="cross_attention",
    )(q3, k4, v4)


def _ffn_mid_kernel(g_ref, v_ref, gp_ref, vp_ref, gw_ref, vw_ref, gb_ref, vb_ref, o_ref, gext_ref, vext_ref,
                    *, L, from_state, conv_k):
    hist = conv_k - 1
    i = pl.program_id(1)

    def conv(cur_ref, prev_ref, w_ref, b_ref, ext_ref):
        if from_state:
            ext_ref[:, 0:CONV_HALO, :] = jnp.zeros((ext_ref.shape[0], CONV_HALO, ext_ref.shape[2]), F32)
            ext_ref[:, CONV_HALO - hist:CONV_HALO, :] = prev_ref[0]
        else:
            ext_ref[:, 0:CONV_HALO, :] = jnp.where(i == 0, 0.0, prev_ref[...])
        ext_ref[:, CONV_HALO:CONV_HALO + L, :] = cur_ref[...]
        return _causal_conv(ext_ref[...], w_ref, L, 1) + b_ref[...]

    g = conv(g_ref, gp_ref, gw_ref, gb_ref, gext_ref)
    v = conv(v_ref, vp_ref, vw_ref, vb_ref, vext_ref)
    o_ref[...] = (_silu(g) * v).astype(o_ref.dtype)


def ffn_mid(up3, prev_state, cw, cb, *, layer, nb, L, tc, out_dtype):
    b, s, two_f = up3.shape
    f = two_f // 2
    conv_k = cw.shape[0]
    assert f % tc == 0
    nj = f // tc
    from_state = prev_state is not None
    if from_state:
        assert L == s
        prev = prev_state
        gp_spec = pl.BlockSpec((1, nb, conv_k - 1, tc), lambda n, i, j: (layer, n, 0, j))
        vp_spec = pl.BlockSpec((1, nb, conv_k - 1, tc), lambda n, i, j: (layer, n, 0, j + nj))
    else:
        assert L % CONV_HALO == 0
        prev = up3
        r = L // CONV_HALO
        gp_spec = pl.BlockSpec((nb, CONV_HALO, tc), lambda n, i, j: (n, jnp.maximum(i * r - 1, 0), j))
        vp_spec = pl.BlockSpec((nb, CONV_HALO, tc), lambda n, i, j: (n, jnp.maximum(i * r - 1, 0), j + nj))
    cb2 = cb.reshape(1, two_f)
    return pl.pallas_call(
        functools.partial(_ffn_mid_kernel, L=L, from_state=from_state, conv_k=conv_k),
        grid=(b // nb, s // L, nj),
        in_specs=[
            pl.BlockSpec((nb, L, tc), lambda n, i, j: (n, i, j)),
            pl.BlockSpec((nb, L, tc), lambda n, i, j: (n, i, j + nj)),
            gp_spec, vp_spec,
            pl.BlockSpec((conv_k, tc), lambda n, i, j: (0, j)),
            pl.BlockSpec((conv_k, tc), lambda n, i, j: (0, j + nj)),
            pl.BlockSpec((1, tc), lambda n, i, j: (0, j)),
            pl.BlockSpec((1, tc), lambda n, i, j: (0, j + nj)),
        ],
        out_specs=pl.BlockSpec((nb, L, tc), lambda n, i, j: (n, i, j)),
        out_shape=jax.ShapeDtypeStruct((b, s, f), out_dtype),
        scratch_shapes=[pltpu.VMEM((nb, CONV_HALO + L, tc), F32), pltpu.VMEM((nb, CONV_HALO + L, tc), F32)],
        compiler_params=_cparams("parallel", "arbitrary", "arbitrary"),
        name="ffn_mid_state" if from_state else "ffn_mid_scan",
    )(up3, up3, prev, prev, cw, cw, cb2, cb2)


def _ffn_front_kernel(x_ref, g_ref, wg_ref, wv_ref, cwg_ref, cwv_ref, cbg_ref, cbv_ref,
                      act_ref, tail_g_ref, tail_v_ref, xn_ref, halo_ref, *, tiles_per_seq):
    i, j = pl.program_id(0), pl.program_id(1)
    tm = x_ref.shape[0]

    @pl.when(j == 0)
    def _():
        x = x_ref[...]
        ms = jnp.mean(x * x, axis=-1, keepdims=True)
        xn_ref[...] = (x * lax.rsqrt(ms + EPS) * g_ref[...]).astype(BF16)

    xn = xn_ref[...]
    seq_start = (i % tiles_per_seq) == 0

    def half(w_ref, cw_ref, cb_ref, tail_ref, slot):
        up = jnp.dot(xn, w_ref[0], preferred_element_type=F32)
        prev = jnp.where(seq_start, 0.0, halo_ref[slot, j])
        tail = up[tm - CONV_HALO:tm, :]
        halo_ref[slot, j] = tail
        tail_ref[0] = tail
        ext = jnp.concatenate([prev, up], axis=0)
        return _causal_conv(ext, cw_ref, tm, 0) + cb_ref[...]

    g = half(wg_ref, cwg_ref, cbg_ref, tail_g_ref, 0)
    v = half(wv_ref, cwv_ref, cbv_ref, tail_v_ref, 1)
    act_ref[...] = (_silu(g) * v).astype(act_ref.dtype)


def ffn_front(x2, gain, w_up, cw, cb, *, layer, seq_len, tm, tn, out_dtype):
    t, k = x2.shape
    f = w_up.shape[2] // 2
    assert t % tm == 0 and f % tn == 0 and seq_len % tm == 0
    nj = f // tn
    conv_k = cw.shape[0]
    cb2 = cb.reshape(1, 2 * f)
    return pl.pallas_call(
        functools.partial(_ffn_front_kernel, tiles_per_seq=seq_len // tm),
        grid=(t // tm, nj),
        in_specs=[
            pl.BlockSpec((tm, k), lambda i, j: (i, 0)),
            pl.BlockSpec((1, k), lambda i, j: (0, 0)),
            pl.BlockSpec((1, k, tn), lambda i, j: (layer, 0, j)),
            pl.BlockSpec((1, k, tn), lambda i, j: (layer, 0, j + nj)),
            pl.BlockSpec((conv_k, tn), lambda i, j: (0, j)),
            pl.BlockSpec((conv_k, tn), lambda i, j: (0, j + nj)),
            pl.BlockSpec((1, tn), lambda i, j: (0, j)),
            pl.BlockSpec((1, tn), lambda i, j: (0, j + nj)),
        ],
        out_specs=[pl.BlockSpec((tm, tn), lambda i, j: (i, j)),
                   pl.BlockSpec((1, CONV_HALO, tn), lambda i, j: (i, 0, j)),
                   pl.BlockSpec((1, CONV_HALO, tn), lambda i, j: (i, 0, j))],
        out_shape=[jax.ShapeDtypeStruct((t, f), out_dtype),
                   jax.ShapeDtypeStruct((t // tm, CONV_HALO, f), F32),
                   jax.ShapeDtypeStruct((t // tm, CONV_HALO, f), F32)],
        scratch_shapes=[pltpu.VMEM((tm, k), BF16), pltpu.VMEM((2, nj, CONV_HALO, tn), F32)],
        compiler_params=_cparams("arbitrary", "arbitrary"),
        name="ffn_front",
    )(x2, gain.reshape(1, k), w_up, w_up, cw, cw, cb2, cb2)


def _history_tail(prev, cur, hist):
    s = cur.shape[1]
    if s >= hist:
        return cur[:, s - hist:]
    return jnp.concatenate([prev[:, s:], cur], axis=1)


def _layer(x3, wts, lw, dims, states, ssm_out, mem_kv, *, layer, kv_layer, pos0, tiles):
    b, s, d = x3.shape
    t = b * s
    x2 = x3.reshape(t, d)
    G, R, P, N = dims["groups"], dims["heads_per_group"], dims["head_dim"], dims["n_state"]
    inner = G * R * P
    conv_dim = inner + 2 * G * N
    z_col, u_col, xbc_col, gp_col, gs_col = 0, inner, inner + d, inner + d + conv_dim, inner + 2 * d + conv_dim
    n_main = gs_col + d
    tm, tn, tn_k = tiles["tm"], tiles["tn"], tiles["tn_long_k"]
    act_dtype = tiles["act_dtype"]
    hp, hc, hf = POOL_HALO - 1, lw["ssm_conv_w"].shape[0] - 1, lw["ffn_conv_w"].shape[0] - 1

    if states is None:
        pool_prev = conv_prev = ssm_prev = ffn_prev = None
    else:
        pool_prev, conv_prev, ssm_prev, ffn_prev = states

    main, dt = norm_matmul(x2, lw["norm_mix"], wts["w_main"], wts["w_dt"], layer=layer, tm=tm, tn=tn, name="in_proj")
    main3 = main.reshape(b, s, n_main)
    dt3 = dt.reshape(b, s, LANES)

    gated_pool = pool_branch(main3, pool_prev, wts["w_pool_group"], lw["pool_scale"], wts["w_pool_out"], layer=layer,
                             u_col=u_col, gate_col=gp_col, nb=tiles["pool_nb"], L=tiles["pool_L"], pos0=pos0)
    y, ssm_new = ssd_branch(main3, dt3, conv_prev, ssm_prev, ssm_out, lw["ssm_conv_w"], lw["ssm_conv_b"],
                            lw["dt_bias"], lw["a_log"], lw["d_exp"], lw["ssm_norm"], lw["expand"], layer=layer,
                            xbc_col=xbc_col, z_col=z_col, q=tiles["ssd_q"], nb=tiles["ssd_nb"], n_groups=G,
                            heads_per_group=R, head_dim=P, n_state=N, out_dtype=act_dtype)
    merged = matmul(y.reshape(t, inner), wts["w_ssm_out"],
                    [(gated_pool.reshape(t, d), 0), (main, gs_col)], _ep_gate_merge,
                    layer=layer, tm=tm, tn=tn_k, out_dtype=BF16, name="ssm_out_merge")
    x2 = matmul(merged, wts["w_out"], [(x2, 0)], _ep_residual, layer=layer, tm=tm, tn=tn, out_dtype=F32,
                name="mix_out")

    qm = norm_matmul(x2, lw["norm_mem_q"], wts["w_mem_q"], layer=layer, tm=tm, tn=tn, out_dtype=act_dtype,
                     name="mem_q")
    k4, v4 = mem_kv
    o = cross_attention(qm.reshape(b, s, d), k4, v4, layer=kv_layer, nb=tiles["attn_nb"], tq=tiles["attn_tq"],
                        n_heads=dims["mem_heads"], out_dtype=act_dtype)
    x2 = matmul(o.reshape(t, d), wts["w_mem_o"], [(x2, 0)], _ep_residual, layer=layer, tm=tm, tn=tn, out_dtype=F32,
                name="mem_o")

    if states is None:
        act2, tail_g, tail_v = ffn_front(x2, lw["norm_ffn"], wts["w_ffn_up"], lw["ffn_conv_w"], lw["ffn_conv_b"],
                                         layer=layer, seq_len=s, tm=tm, tn=tiles["ffn_tc"], out_dtype=act_dtype)
        per_seq = s // tm
        ffn_new = jnp.concatenate([tail_g[per_seq - 1::per_seq, CONV_HALO - hf:],
                                   tail_v[per_seq - 1::per_seq, CONV_HALO - hf:]], axis=-1)
    else:
        up = norm_matmul(x2, lw["norm_ffn"], wts["w_ffn_up"], layer=layer, tm=tm, tn=tn, name="ffn_up")
        up3 = up.reshape(b, s, up.shape[1])
        act2 = ffn_mid(up3, ffn_prev, lw["ffn_conv_w"], lw["ffn_conv_b"], layer=layer, nb=tiles["ffn_nb"],
                       L=tiles["ffn_L"], tc=tiles["ffn_tc"], out_dtype=act_dtype).reshape(t, -1)
        ffn_new = _history_tail(ffn_prev[layer], up3, hf)
    x2 = matmul(act2, wts["w_ffn_down"], [(x2, 0)], _ep_residual,
                layer=layer, tm=tiles["tm_down"], tn=tn_k, out_dtype=F32, name="ffn_down")

    u3 = main3[:, :, u_col:u_col + d]
    xbc3 = main3[:, :, xbc_col:xbc_col + conv_dim]
    if states is None:
        pool_new, conv_new = u3[:, s - hp:], xbc3[:, s - hc:]
    else:
        pool_new = _history_tail(pool_prev[layer], u3, hp)
        conv_new = _history_tail(conv_prev[layer], xbc3, hc)
    return x2.reshape(b, s, d), pool_new, conv_new, ssm_new, ffn_new


def kernel(x_prompt, x_sample, state_pool, state_ssm_conv, state_ssm, state_ffn_conv, cache_mem_k, cache_mem_v,
           mem_prompt, norm_mix, w_in, w_pool_group, pool_scale, w_pool_out, ssm_conv_w, ssm_conv_b, ssm_dt_bias,
           ssm_a_log, ssm_d, ssm_norm, w_ssm_out, w_out, norm_mem_q, w_mem_q, w_mem_o, norm_mem_kv, w_mem_k,
           w_mem_v, norm_ffn, w_ffn_up, ffn_conv_w, ffn_conv_b, w_ffn_down, norm_final):
    depth = w_in.shape[0]
    bp, sp, d = x_prompt.shape
    bs, ss, _ = x_sample.shape
    n_heads = ssm_d.shape[1]
    inner = w_ssm_out.shape[1]
    head_dim = inner // n_heads
    n_state = state_ssm.shape[-1]
    conv_dim = ssm_conv_w.shape[2]
    n_groups = (conv_dim - inner) // (2 * n_state)
    mem_heads = cache_mem_k.shape[3]
    mem_len = mem_prompt.shape[1]
    dims = dict(groups=n_groups, heads_per_group=n_heads // n_groups, head_dim=head_dim, n_state=n_state,
                mem_heads=mem_heads)
    assert n_heads <= LANES

    c_u, c_z, c_x, c_dt, c_gp = d, d + inner, d + inner + conv_dim, d + inner + conv_dim + n_heads, 2 * d + inner + conv_dim + n_heads
    head_of_col = jnp.arange(inner, dtype=jnp.int32) // head_dim
    expand = (jnp.arange(LANES, dtype=jnp.int32)[:, None] == head_of_col[None, :]).astype(BF16)

    def pad_heads(v):
        return jnp.pad(v, (0, LANES - n_heads)).reshape(1, LANES)

    tiles_p = dict(tm=1024, tm_down=1024, tn=1024, tn_long_k=512, pool_nb=1, pool_L=512, ssd_q=128, ssd_nb=1,
                   attn_nb=1, attn_tq=512, ffn_nb=1, ffn_L=512, ffn_tc=512, act_dtype=BF16)
    tiles_s = dict(tm=512, tm_down=512, tn=1024, tn_long_k=512, pool_nb=32, pool_L=ss, ssd_q=ss, ssd_nb=2,
                   attn_nb=2, attn_tq=ss, ffn_nb=64, ffn_L=ss, ffn_tc=512, act_dtype=F32)

    ssm_states = state_ssm.reshape(depth, bs, n_groups, inner // n_groups, n_state)
    kv_s = (cache_mem_k, cache_mem_v)
    states_s = (state_pool, state_ssm_conv, ssm_states, state_ffn_conv)
    ssm_s_all = None

    wts = dict(
        w_main=jnp.concatenate([w_in[:, :, c_u:c_z], w_in[:, :, :c_u], w_in[:, :, c_z:c_x], w_in[:, :, c_dt:]],
                               axis=2).astype(BF16),
        w_dt=jnp.pad(w_in[:, :, c_x:c_dt], ((0, 0), (0, 0), (0, LANES - n_heads))).astype(BF16),
        w_pool_group=w_pool_group.astype(BF16), w_pool_out=w_pool_out.astype(BF16),
        w_ssm_out=w_ssm_out.astype(BF16), w_out=w_out.astype(BF16),
        w_mem_q=w_mem_q.astype(BF16), w_mem_o=w_mem_o.astype(BF16),
        w_mem_k=w_mem_k.astype(BF16), w_mem_v=w_mem_v.astype(BF16),
        w_ffn_up=w_ffn_up.astype(BF16), w_ffn_down=w_ffn_down.astype(BF16),
    )

    yp, ys = x_prompt, x_sample
    outs = [[] for _ in range(9)]
    for i in range(depth):
        lw = dict(
            norm_mix=norm_mix[i], pool_scale=pool_scale[i],
            ssm_conv_w=ssm_conv_w[i], ssm_conv_b=ssm_conv_b[i],
            dt_bias=pad_heads(ssm_dt_bias[i]), a_log=pad_heads(ssm_a_log[i]),
            d_exp=jnp.repeat(ssm_d[i], head_dim).reshape(1, inner), ssm_norm=ssm_norm[i], expand=expand,
            norm_mem_q=norm_mem_q[i], norm_ffn=norm_ffn[i],
            ffn_conv_w=ffn_conv_w[i], ffn_conv_b=ffn_conv_b[i],
        )
        mem2 = mem_prompt.reshape(bp * mem_len, d)
        k_i = norm_matmul(mem2, norm_mem_kv[i], wts["w_mem_k"], layer=i, tm=512, tn=1024, name="mem_k")
        v_i = norm_matmul(mem2, norm_mem_kv[i], wts["w_mem_v"], layer=i, tm=512, tn=1024, name="mem_v")
        kv_p = (k_i.reshape(1, bp, mem_len, d), v_i.reshape(1, bp, mem_len, d))

        yp, a0, a1, a2, a3 = _layer(yp, wts, lw, dims, None, None, kv_p, layer=i, kv_layer=0, pos0=0, tiles=tiles_p)
        ys, b0, b1, ssm_s_all, b3 = _layer(ys, wts, lw, dims, states_s, ssm_s_all, kv_s, layer=i, kv_layer=i,
                                           pos0=PAST_LEN, tiles=tiles_s)
        for lst, val in zip(outs, (a0, b0, a1, b1, a2.reshape(bp, n_heads, head_dim, n_state), a3, b3,
                                   k_i.reshape(bp, mem_len, mem_heads, d // mem_heads),
                                   v_i.reshape(bp, mem_len, mem_heads, d // mem_heads))):
            lst.append(val)

    y_prompt = rmsnorm(yp.reshape(bp * sp, d), norm_final, tm=512).reshape(bp, sp, d)
    y_sample = rmsnorm(ys.reshape(bs * ss, d), norm_final, tm=512).reshape(bs, ss, d)
    pool_p, pool_s, conv_p, conv_s, ssm_p, ffn_p, ffn_s, mk_p, mv_p = (jnp.stack(lst) for lst in outs)
    ssm_s = ssm_s_all.reshape(depth, bs, n_heads, head_dim, n_state)
    return (y_prompt, y_sample, pool_p, pool_s, conv_p, conv_s, ssm_p, ssm_s, ffn_p, ffn_s, mk_p, mv_p)
```

```python
import functools
import math

import jax
import jax.numpy as jnp
from jax import lax
from jax.experimental import pallas as pl
from jax.experimental.pallas import tpu as pltpu

F32 = jnp.float32
BF16 = jnp.bfloat16
EPS = 1e-6

LANES = 128
SUBLANES = 8
VMEM_LIMIT = 56 * 1024 * 1024

PAST_LEN = 16384
POOL_WINDOWS = (2, 4, 8, 16)
POOL_HALO = 16
CONV_HALO = SUBLANES


def _cparams(*sem):
    return pltpu.CompilerParams(dimension_semantics=sem, vmem_limit_bytes=VMEM_LIMIT)


def _silu(x):
    h = 0.5 * x
    return h + h * jnp.tanh(h)


def _causal_conv(ext, w_ref, rows, axis):
    conv_k = w_ref.shape[0]
    tail = (slice(None),) * axis + (slice(CONV_HALO, CONV_HALO + rows),)
    acc = ext[tail] * w_ref[conv_k - 1:conv_k, :]
    for j in range(conv_k - 1):
        acc = acc + pltpu.roll(ext, conv_k - 1 - j, axis=axis)[tail] * w_ref[j:j + 1, :]
    return acc


def _norm_matmul_kernel(x_ref, g_ref, *rest, bounds, has_aux):
    n_parts = len(bounds) - 1
    w_refs, rest = rest[:n_parts], rest[n_parts:]
    if has_aux:
        wa_ref, o_ref, oa_ref, xn_ref = rest
    else:
        o_ref, xn_ref = rest
    j = pl.program_id(1)

    @pl.when(j == 0)
    def _():
        x = x_ref[...]
        ms = jnp.mean(x * x, axis=-1, keepdims=True)
        xn = (x * lax.rsqrt(ms + EPS) * g_ref[...]).astype(BF16)
        xn_ref[...] = xn
        if has_aux:
            oa_ref[...] = jnp.dot(xn, wa_ref[0], preferred_element_type=F32)

    if n_parts == 1:
        o_ref[...] = jnp.dot(xn_ref[...], w_refs[0][0], preferred_element_type=F32).astype(o_ref.dtype)
    else:
        for p in range(n_parts):
            @pl.when((j >= bounds[p]) & (j < bounds[p + 1]))
            def _(p=p):
                o_ref[...] = jnp.dot(xn_ref[...], w_refs[p][0], preferred_element_type=F32).astype(o_ref.dtype)


def norm_matmul(x, g, w, w_aux=None, *, layer, tm, tn, out_dtype=F32, out_block=None, name):
    t, k = x.shape
    parts = list(w) if isinstance(w, (list, tuple)) else [w]
    bounds = [0]
    for part in parts:
        assert part.shape[2] % tn == 0
        bounds.append(bounds[-1] + part.shape[2] // tn)
    n = bounds[-1] * tn
    assert t % tm == 0
    has_aux = w_aux is not None
    in_specs = [
        pl.BlockSpec((tm, k), lambda i, j: (i, 0)),
        pl.BlockSpec((1, k), lambda i, j: (0, 0)),
    ]
    for lo, hi in zip(bounds[:-1], bounds[1:]):
        in_specs.append(pl.BlockSpec((1, k, tn), lambda i, j, lo=lo, hi=hi: (layer, 0, jnp.clip(j - lo, 0, hi - lo - 1))))
    args = [x, g.reshape(1, k)] + parts
    out_block = out_block or (lambda j: j)
    out_shape = [jax.ShapeDtypeStruct((t, n), out_dtype)]
    out_specs = [pl.BlockSpec((tm, tn), lambda i, j: (i, out_block(j)))]
    if has_aux:
        na = w_aux.shape[2]
        in_specs.append(pl.BlockSpec((1, k, na), lambda i, j: (layer, 0, 0)))
        args.append(w_aux)
        out_shape.append(jax.ShapeDtypeStruct((t, na), F32))
        out_specs.append(pl.BlockSpec((tm, na), lambda i, j: (i, 0)))
    res = pl.pallas_call(
        functools.partial(_norm_matmul_kernel, bounds=tuple(bounds), has_aux=has_aux),
        grid=(t // tm, n // tn),
        in_specs=in_specs,
        out_specs=out_specs,
        out_shape=out_shape,
        scratch_shapes=[pltpu.VMEM((tm, k), BF16)],
        compiler_params=_cparams("parallel", "arbitrary"),
        name=name,
    )(*args)
    return res if has_aux else res[0]


def _matmul_kernel(a_ref, w_ref, *rest, epilogue):
    *extra, o_ref = rest
    acc = jnp.dot(a_ref[...].astype(BF16), w_ref[0], preferred_element_type=F32)
    o_ref[...] = epilogue(acc, *[e[...] for e in extra]).astype(o_ref.dtype)


def _ep_residual(acc, r):
    return r + acc


def _ep_gate_merge(acc, gated_pool, g_ssm):
    return gated_pool + jax.nn.sigmoid(g_ssm) * acc


def matmul(a, w, extras, epilogue, *, layer, tm, tn, out_dtype, name):
    t, k = a.shape
    n = w.shape[2]
    assert t % tm == 0 and n % tn == 0
    in_specs = [
        pl.BlockSpec((tm, k), lambda i, j: (i, 0)),
        pl.BlockSpec((1, k, tn), lambda i, j: (layer, 0, j)),
    ]
    args = [a, w]
    for arr, off in extras:
        assert off % tn == 0
        ob = off // tn
        in_specs.append(pl.BlockSpec((tm, tn), lambda i, j, ob=ob: (i, j + ob)))
        args.append(arr)
    return pl.pallas_call(
        functools.partial(_matmul_kernel, epilogue=epilogue),
        grid=(t // tm, n // tn),
        in_specs=in_specs,
        out_specs=pl.BlockSpec((tm, tn), lambda i, j: (i, j)),
        out_shape=jax.ShapeDtypeStruct((t, n), out_dtype),
        compiler_params=_cparams("parallel", "arbitrary"),
        name=name,
    )(*args)


def _rmsnorm_kernel(x_ref, g_ref, o_ref):
    x = x_ref[...]
    ms = jnp.mean(x * x, axis=-1, keepdims=True)
    o_ref[...] = x * lax.rsqrt(ms + EPS) * g_ref[...]


def rmsnorm(x, g, *, tm):
    t, k = x.shape
    return pl.pallas_call(
        _rmsnorm_kernel,
        grid=(t // tm,),
        in_specs=[pl.BlockSpec((tm, k), lambda i: (i, 0)), pl.BlockSpec((1, k), lambda i: (0, 0))],
        out_specs=pl.BlockSpec((tm, k), lambda i: (i, 0)),
        out_shape=jax.ShapeDtypeStruct((t, k), F32),
        compiler_params=_cparams("parallel"),
    )(x, g.reshape(1, k))


def _pool_kernel(cur_ref, prev_ref, gate_ref, wg_ref, scale_ref, wo_ref, o_ref, ext_ref, pooled_ref,
                 *, nb, L, from_state, pos0):
    w_ch = cur_ref.shape[-1]
    gdim = w_ch // len(POOL_WINDOWS)
    if from_state:
        ext_ref[:, 1:POOL_HALO, :] = prev_ref[0]
        pos_start = pos0
    else:
        i = pl.program_id(1)
        ext_ref[:, 0:POOL_HALO, :] = jnp.where(i == 0, 0.0, prev_ref[...])
        pos_start = pos0 + i * L
    ext_ref[:, POOL_HALO:POOL_HALO + L, :] = cur_ref[...]

    pos = pos_start + lax.broadcasted_iota(jnp.int32, (1, L, gdim), 1)
    for k, win in enumerate(POOL_WINDOWS):
        cs = slice(k * gdim, (k + 1) * gdim)
        cur = ext_ref[:, POOL_HALO:POOL_HALO + L, cs]
        acc = cur
        for j in range(1, win):
            acc = acc + ext_ref[:, POOL_HALO - j:POOL_HALO - j + L, cs]
        count = jnp.minimum(pos + 1, win).astype(F32)
        diff = (acc / count - cur).reshape(nb * L, gdim).astype(BF16)
        mixed = jnp.dot(diff, wg_ref[0, k], preferred_element_type=F32)
        pooled_ref[:, cs] = (mixed * scale_ref[:, cs]).astype(BF16)
    out_pool = jnp.dot(pooled_ref[...], wo_ref[0], preferred_element_type=F32)
    gate = jax.nn.sigmoid(gate_ref[...].reshape(nb * L, w_ch))
    o_ref[...] = (gate * out_pool).reshape(nb, L, w_ch)


def pool_branch(main3, prev_state, wg, scale, wo, *, layer, u_col, gate_col, nb, L, pos0):
    b, s, _ = main3.shape
    w_ch = wo.shape[1]
    ub, gb = u_col // w_ch, gate_col // w_ch
    from_state = prev_state is not None
    if from_state:
        assert L == s
        prev = prev_state
        prev_spec = pl.BlockSpec((1, nb, POOL_HALO - 1, w_ch), lambda n, i: (layer, n, 0, 0))
    else:
        assert nb == 1 and L % POOL_HALO == 0
        prev = main3
        r = L // POOL_HALO
        prev_spec = pl.BlockSpec((1, POOL_HALO, w_ch), lambda n, i: (n, jnp.maximum(i * r - 1, 0), ub))
    return pl.pallas_call(
        functools.partial(_pool_kernel, nb=nb, L=L, from_state=from_state, pos0=pos0),
        grid=(b // nb, s // L),
        in_specs=[
            pl.BlockSpec((nb, L, w_ch), lambda n, i: (n, i, ub)),
            prev_spec,
            pl.BlockSpec((nb, L, w_ch), lambda n, i: (n, i, gb)),
            pl.BlockSpec((1,) + wg.shape[1:], lambda n, i: (layer, 0, 0, 0)),
            pl.BlockSpec((1, w_ch), lambda n, i: (0, 0)),
            pl.BlockSpec((1,) + wo.shape[1:], lambda n, i: (layer, 0, 0)),
        ],
        out_specs=pl.BlockSpec((nb, L, w_ch), lambda n, i: (n, i, 0)),
        out_shape=jax.ShapeDtypeStruct((b, s, w_ch), F32),
        scratch_shapes=[pltpu.VMEM((nb, POOL_HALO + L, w_ch), F32), pltpu.VMEM((nb * L, w_ch), BF16)],
        compiler_params=_cparams("parallel", "arbitrary"),
        name="pool_state" if from_state else "pool_scan",
    )(main3, prev, main3, wg, scale.reshape(1, w_ch), wo)


def _cumsum_rows(x):
    q = x.shape[0]
    row = lax.broadcasted_iota(jnp.int32, x.shape, 0)
    k = 1
    while k < q:
        x = x + jnp.where(row >= k, pltpu.roll(x, k, axis=0), 0.0)
        k *= 2
    return x


def _expand_heads(vals, e_ref):
    q = vals[0].shape[0]
    pieces = []
    for v in vals:
        hi = v.astype(BF16).astype(F32)
        r1 = v - hi
        mid = r1.astype(BF16).astype(F32)
        pieces += [hi, mid, r1 - mid]
    lhs = jnp.concatenate(pieces, axis=0).astype(BF16)
    out = jnp.dot(lhs, e_ref[...], preferred_element_type=F32)
    return [out[(3 * i) * q:(3 * i + 1) * q] + out[(3 * i + 1) * q:(3 * i + 2) * q] + out[(3 * i + 2) * q:(3 * i + 3) * q]
            for i in range(len(vals))]


def _ssd_kernel(*refs, q, nb, n_groups, heads_per_group, head_dim, n_state, from_state, conv_k):
    if from_state:
        (xbc_ref, prev_ref, z_ref, dt_ref, h0_ref, cw_ref, cb_ref, dtb_ref, alog_ref, dexp_ref, norm_ref, e_ref,
         *_, y_ref, hout_ref, ext_ref, act_ref, yacc_ref) = refs
        params = (dtb_ref, alog_ref, dexp_ref, norm_ref, e_ref)
        hist = conv_k - 1
        for n in range(nb):
            ext = ext_ref.at[n]
            ext[0:CONV_HALO, :] = jnp.zeros((CONV_HALO, ext.shape[1]), F32)
            ext[CONV_HALO - hist:CONV_HALO, :] = prev_ref[0, n]
            ext[CONV_HALO:CONV_HALO + q, :] = xbc_ref[n]
            act_ref[n] = _silu(_causal_conv(ext[...], cw_ref, q, 0) + cb_ref[...])
        for n in range(nb):
            _ssd_chunk(act_ref.at[n], z_ref.at[n], dt_ref.at[n], h0_ref.at[0, n], hout_ref.at[0, n], y_ref.at[n],
                       yacc_ref.at[n], params, q=q, n_groups=n_groups, heads_per_group=heads_per_group,
                       head_dim=head_dim, n_state=n_state)
    else:
        (xbc_ref, prev_ref, z_ref, dt_ref, cw_ref, cb_ref, dtb_ref, alog_ref, dexp_ref, norm_ref, e_ref,
         y_ref, hout_ref, ext_ref, act_ref, h_ref, yacc_ref) = refs
        params = (dtb_ref, alog_ref, dexp_ref, norm_ref, e_ref)
        c = pl.program_id(1)

        @pl.when(c == 0)
        def _():
            h_ref[...] = jnp.zeros_like(h_ref)

        ext = ext_ref.at[0]
        ext[0:CONV_HALO, :] = jnp.where(c == 0, 0.0, prev_ref[0])
        ext[CONV_HALO:CONV_HALO + q, :] = xbc_ref[0]
        act_ref[0] = _silu(_causal_conv(ext[...], cw_ref, q, 0) + cb_ref[...])
        _ssd_chunk(act_ref.at[0], z_ref.at[0], dt_ref.at[0], h_ref, h_ref, y_ref.at[0], yacc_ref.at[0], params,
                   q=q, n_groups=n_groups, heads_per_group=heads_per_group, head_dim=head_dim, n_state=n_state)

        @pl.when(c == pl.num_programs(1) - 1)
        def _():
            hout_ref[0] = h_ref[...]


def _ssd_chunk(act_ref, z_ref, dt_ref, h_in, h_out, y_ref, yacc_ref, params,
               *, q, n_groups, heads_per_group, head_dim, n_state):
    dtb_ref, alog_ref, dexp_ref, norm_ref, e_ref = params
    gw = heads_per_group * head_dim
    inner = n_groups * gw

    dt = jax.nn.softplus(dt_ref[...] + dtb_ref[...])
    la_cs = _cumsum_rows(dt * (-jnp.exp(alog_ref[...])))
    la_cs_t = la_cs.T
    dt_x, la_x = _expand_heads([dt, la_cs], e_ref)
    from_start_x = jnp.exp(la_x)
    to_end_x = jnp.exp(la_x[q - 1:q, :] - la_x)
    chunk_decay = jnp.exp(la_cs[q - 1:q, :])

    xs = act_ref[:, 0:inner]
    xd = xs * dt_x
    xdw = xd * to_end_x
    yacc_ref[...] = dexp_ref[...] * xs

    tri = lax.broadcasted_iota(jnp.int32, (q, q), 0) >= lax.broadcasted_iota(jnp.int32, (q, q), 1)
    assert LANES % head_dim == 0
    hpl = LANES // head_dim
    lane_head = lax.broadcasted_iota(jnp.int32, (q, LANES), 1) // head_dim
    bms, cbs = [], []
    for g in range(n_groups):
        gs = slice(g * gw, (g + 1) * gw)
        bm = act_ref[:, inner + g * n_state:inner + (g + 1) * n_state].astype(BF16)
        cm = act_ref[:, inner + (n_groups + g) * n_state:inner + (n_groups + g + 1) * n_state].astype(BF16)
        bms.append(bm)
        cbs.append(lax.dot_general(cm, bm, (((1,), (1,)), ((), ())), preferred_element_type=F32))
        y_off = lax.dot_general(cm, h_in[g].astype(BF16), (((1,), (1,)), ((), ())),
                                preferred_element_type=F32)
        yacc_ref[:, gs] += y_off * from_start_x[:, gs]
    for g in range(n_groups):
        cb = cbs[g]
        for lt in range(gw // LANES):
            ls = slice(g * gw + lt * LANES, g * gw + (lt + 1) * LANES)
            xd_t = xd[:, ls]
            y_t = None
            for k in range(hpl):
                hd = (g * gw + lt * LANES) // head_dim + k
                seg = la_cs[:, hd:hd + 1] - la_cs_t[hd:hd + 1, :]
                m = (cb * jnp.exp(jnp.where(tri, seg, -jnp.inf))).astype(BF16)
                rhs = jnp.where(lane_head == k, xd_t, 0.0).astype(BF16)
                part = jnp.dot(m, rhs, preferred_element_type=F32)
                y_t = part if y_t is None else y_t + part
            yacc_ref[:, ls] += y_t
    for g in range(n_groups):
        gs = slice(g * gw, (g + 1) * gw)
        s_new = lax.dot_general(xdw[:, gs].astype(BF16), bms[g], (((0,), (0,)), ((), ())),
                                preferred_element_type=F32)
        for r in range(heads_per_group):
            hd = g * heads_per_group + r
            rows = slice(r * head_dim, (r + 1) * head_dim)
            h_out[g, rows, :] = h_in[g, rows, :] * chunk_decay[:, hd:hd + 1] + s_new[rows, :]

    y = yacc_ref[...] * _silu(z_ref[...])
    for g in range(n_groups):
        gs = slice(g * gw, (g + 1) * gw)
        yg = y[:, gs]
        ms = jnp.mean(yg * yg, axis=-1, keepdims=True)
        y_ref[:, gs] = (yg * lax.rsqrt(ms + EPS) * norm_ref[:, gs]).astype(y_ref.dtype)


def ssd_branch(main3, dt3, conv_state, ssm_state, state_out, cw, cb, dt_bias, a_log, d_exp, norm, expand,
               *, layer, xbc_col, z_col, q, nb, n_groups, heads_per_group, head_dim, n_state, out_dtype):
    b, s, _ = main3.shape
    gw = heads_per_group * head_dim
    inner = n_groups * gw
    conv_dim = inner + 2 * n_groups * n_state
    conv_k = cw.shape[0]
    from_state = ssm_state is not None
    xb, zb = xbc_col // conv_dim, z_col // inner
    assert xbc_col % conv_dim == 0 and z_col % inner == 0

    def const(shape):
        nd = len(shape)
        return pl.BlockSpec(shape, lambda n, c: (0,) * nd)

    assert b % nb == 0 and (from_state or nb == 1)
    in_specs = [pl.BlockSpec((nb, q, conv_dim), lambda n, c: (n, c, xb))]
    args = [main3]
    if from_state:
        assert q == s
        in_specs.append(pl.BlockSpec((1, nb, conv_k - 1, conv_dim), lambda n, c: (layer, n, 0, 0)))
        args.append(conv_state)
    else:
        assert q % CONV_HALO == 0
        r = q // CONV_HALO
        in_specs.append(pl.BlockSpec((1, CONV_HALO, conv_dim), lambda n, c: (n, jnp.maximum(c * r - 1, 0), xb)))
        args.append(main3)
    in_specs += [pl.BlockSpec((nb, q, inner), lambda n, c: (n, c, zb)),
                 pl.BlockSpec((nb, q, LANES), lambda n, c: (n, c, 0))]
    args += [main3, dt3]
    if from_state:
        in_specs.append(pl.BlockSpec((1, nb, n_groups, gw, n_state), lambda n, c: (layer, n, 0, 0, 0)))
        args.append(ssm_state)
    small = [cw, cb.reshape(1, conv_dim), dt_bias, a_log, d_exp, norm.reshape(1, inner), expand]
    in_specs += [const(a.shape) for a in small]
    args += small
    aliases = {}
    if from_state:
        state_spec = pl.BlockSpec((1, nb, n_groups, gw, n_state), lambda n, c: (layer, n, 0, 0, 0))
        state_shape = jax.ShapeDtypeStruct(ssm_state.shape, F32)
        if state_out is not None:
            aliases = {len(args): 1}
            in_specs.append(pl.BlockSpec(memory_space=pl.ANY))
            args.append(state_out)
    else:
        state_spec = pl.BlockSpec((1, n_groups, gw, n_state), lambda n, c: (n, 0, 0, 0))
        state_shape = jax.ShapeDtypeStruct((b, n_groups, gw, n_state), F32)
    return pl.pallas_call(
        functools.partial(_ssd_kernel, q=q, nb=nb, n_groups=n_groups, heads_per_group=heads_per_group,
                          head_dim=head_dim, n_state=n_state, from_state=from_state, conv_k=conv_k),
        grid=(b // nb, s // q),
        in_specs=in_specs,
        out_specs=[pl.BlockSpec((nb, q, inner), lambda n, c: (n, c, 0)), state_spec],
        out_shape=[jax.ShapeDtypeStruct((b, s, inner), out_dtype), state_shape],
        input_output_aliases=aliases,
        scratch_shapes=[pltpu.VMEM((nb, CONV_HALO + q, conv_dim), F32),
                        pltpu.VMEM((nb, q, conv_dim), F32)]
                       + ([] if from_state else [pltpu.VMEM((n_groups, gw, n_state), F32)])
                       + [pltpu.VMEM((nb, q, inner), F32)],
        compiler_params=_cparams("parallel", "arbitrary"),
        name="ssd_state" if from_state else "ssd_scan",
    )(*args)


def _kv_head(ref, n, h, d, stage_ref, slot):
    if len(ref.shape) == 5:
        stage_ref[slot] = ref[0, n, :, h, :]
        return stage_ref[slot].astype(BF16)
    return ref[0, n, :, h * d:(h + 1) * d].astype(BF16)


def _attn_kernel(q_ref, k_ref, v_ref, o_ref, *stage, n_heads, nb):
    stage_ref = stage[0] if stage else None
    d = q_ref.shape[-1] // n_heads
    inv = 1.0 / math.sqrt(d)
    units = [(n, h, slice(h * d, (h + 1) * d)) for n in range(nb) for h in range(n_heads)]
    scores = []
    for u, (n, h, hs) in enumerate(units):
        qh = q_ref[n, :, hs].astype(BF16)
        kh = _kv_head(k_ref, n, h, d, stage_ref, 2 * u)
        scores.append(lax.dot_general(qh, kh, (((1,), (1,)), ((), ())), preferred_element_type=F32) * inv)
    probs = []
    for s in scores:
        e = jnp.exp(s - jnp.max(s, axis=-1, keepdims=True))
        probs.append((e / jnp.sum(e, axis=-1, keepdims=True)).astype(BF16))
    for u, ((n, h, hs), p) in enumerate(zip(units, probs)):
        vh = _kv_head(v_ref, n, h, d, stage_ref, 2 * u + 1)
        o_ref[n, :, hs] = jnp.dot(p, vh, preferred_element_type=F32).astype(o_ref.dtype)


def _attn_few_kernel(q_ref, k_ref, v_ref, o_ref, *stage, n_heads, nb):
    stage_ref = stage[0] if stage else None
    tq = q_ref.shape[1]
    d = q_ref.shape[-1] // n_heads
    inv = 1.0 / math.sqrt(d)
    pad = jnp.zeros((LANES - tq, d), F32)
    units = [(n, h, slice(h * d, (h + 1) * d)) for n in range(nb) for h in range(n_heads)]
    scores_t = []
    for u, (n, h, hs) in enumerate(units):
        qh = jnp.concatenate([q_ref[n, :, hs].astype(F32), pad], axis=0).astype(BF16)
        kh = _kv_head(k_ref, n, h, d, stage_ref, 2 * u)
        scores_t.append(lax.dot_general(kh, qh, (((1,), (1,)), ((), ())), preferred_element_type=F32) * inv)
    probs = []
    for st in scores_t:
        e = jnp.exp(st - jnp.max(st, axis=0, keepdims=True))
        pt = e / jnp.sum(e, axis=0, keepdims=True)
        probs.append(pt.T[0:tq, :].astype(BF16))
    for u, ((n, h, hs), p) in enumerate(zip(units, probs)):
        vh = _kv_head(v_ref, n, h, d, stage_ref, 2 * u + 1)
        o_ref[n, :, hs] = jnp.dot(p, vh, preferred_element_type=F32).astype(o_ref.dtype)


def cross_attention(q3, k4, v4, *, layer, nb, tq, n_heads, out_dtype):
    b, s, dm = q3.shape
    m = k4.shape[2]
    body = _attn_few_kernel if tq < LANES else _attn_kernel
    zeros = (0,) * (k4.ndim - 2)
    kv_spec = pl.BlockSpec((1, nb) + k4.shape[2:], lambda n, i: (layer, n) + zeros)
    stage = [pltpu.VMEM((2 * nb * n_heads, m, dm // n_heads), F32)] if k4.ndim == 5 else []
    return pl.pallas_call(
        functools.partial(body, n_heads=n_heads, nb=nb),
        grid=(b // nb, s // tq),
        in_specs=[pl.BlockSpec((nb, tq, dm), lambda n, i: (n, i, 0)), kv_spec, kv_spec],
        out_specs=pl.BlockSpec((nb, tq, dm), lambda n, i: (n, i, 0)),
        out_shape=jax.ShapeDtypeStruct((b, s, dm), out_dtype),
        scratch_shapes=stage,
        compiler_params=_cparams("parallel", "arbitrary"),
        name="cross_attention",
    )(q3, k4, v4)


def _ffn_mid_kernel(g_ref, v_ref, gp_ref, vp_ref, gw_ref, vw_ref, gb_ref, vb_ref, o_ref, gext_ref, vext_ref,
                    *, L, from_state, conv_k):
    hist = conv_k - 1
    i = pl.program_id(1)

    def conv(cur_ref, prev_ref, w_ref, b_ref, ext_ref):
        if from_state:
            ext_ref[:, 0:CONV_HALO, :] = jnp.zeros((ext_ref.shape[0], CONV_HALO, ext_ref.shape[2]), F32)
            ext_ref[:, CONV_HALO - hist:CONV_HALO, :] = prev_ref[0]
        else:
            ext_ref[:, 0:CONV_HALO, :] = jnp.where(i == 0, 0.0, prev_ref[...])
        ext_ref[:, CONV_HALO:CONV_HALO + L, :] = cur_ref[...]
        return _causal_conv(ext_ref[...], w_ref, L, 1) + b_ref[...]

    g = conv(g_ref, gp_ref, gw_ref, gb_ref, gext_ref)
    v = conv(v_ref, vp_ref, vw_ref, vb_ref, vext_ref)
    o_ref[...] = (_silu(g) * v).astype(o_ref.dtype)


def ffn_mid(up3, prev_state, cw, cb, *, layer, nb, L, tc, out_dtype):
    b, s, two_f = up3.shape
    f = two_f // 2
    conv_k = cw.shape[0]
    assert f % tc == 0
    nj = f // tc
    from_state = prev_state is not None
    if from_state:
        assert L == s
        prev = prev_state
        gp_spec = pl.BlockSpec((1, nb, conv_k - 1, tc), lambda n, i, j: (layer, n, 0, j))
        vp_spec = pl.BlockSpec((1, nb, conv_k - 1, tc), lambda n, i, j: (layer, n, 0, j + nj))
    else:
        assert L % CONV_HALO == 0
        prev = up3
        r = L // CONV_HALO
        gp_spec = pl.BlockSpec((nb, CONV_HALO, tc), lambda n, i, j: (n, jnp.maximum(i * r - 1, 0), j))
        vp_spec = pl.BlockSpec((nb, CONV_HALO, tc), lambda n, i, j: (n, jnp.maximum(i * r - 1, 0), j + nj))
    cb2 = cb.reshape(1, two_f)
    return pl.pallas_call(
        functools.partial(_ffn_mid_kernel, L=L, from_state=from_state, conv_k=conv_k),
        grid=(b // nb, s // L, nj),
        in_specs=[
            pl.BlockSpec((nb, L, tc), lambda n, i, j: (n, i, j)),
            pl.BlockSpec((nb, L, tc), lambda n, i, j: (n, i, j + nj)),
            gp_spec, vp_spec,
            pl.BlockSpec((conv_k, tc), lambda n, i, j: (0, j)),
            pl.BlockSpec((conv_k, tc), lambda n, i, j: (0, j + nj)),
            pl.BlockSpec((1, tc), lambda n, i, j: (0, j)),
            pl.BlockSpec((1, tc), lambda n, i, j: (0, j + nj)),
        ],
        out_specs=pl.BlockSpec((nb, L, tc), lambda n, i, j: (n, i, j)),
        out_shape=jax.ShapeDtypeStruct((b, s, f), out_dtype),
        scratch_shapes=[pltpu.VMEM((nb, CONV_HALO + L, tc), F32), pltpu.VMEM((nb, CONV_HALO + L, tc), F32)],
        compiler_params=_cparams("parallel", "arbitrary", "arbitrary"),
        name="ffn_mid_state" if from_state else "ffn_mid_scan",
    )(up3, up3, prev, prev, cw, cw, cb2, cb2)


def _ffn_front_kernel(x_ref, g_ref, wg_ref, wv_ref, cwg_ref, cwv_ref, cbg_ref, cbv_ref,
                      act_ref, tail_g_ref, tail_v_ref, xn_ref, halo_ref, *, tiles_per_seq):
    i, j = pl.program_id(0), pl.program_id(1)
    tm = x_ref.shape[0]

    @pl.when(j == 0)
    def _():
        x = x_ref[...]
        ms = jnp.mean(x * x, axis=-1, keepdims=True)
        xn_ref[...] = (x * lax.rsqrt(ms + EPS) * g_ref[...]).astype(BF16)

    xn = xn_ref[...]
    seq_start = (i % tiles_per_seq) == 0

    def half(w_ref, cw_ref, cb_ref, tail_ref, slot):
        up = jnp.dot(xn, w_ref[0], preferred_element_type=F32)
        prev = jnp.where(seq_start, 0.0, halo_ref[slot, j])
        tail = up[tm - CONV_HALO:tm, :]
        halo_ref[slot, j] = tail
        tail_ref[0] = tail
        ext = jnp.concatenate([prev, up], axis=0)
        return _causal_conv(ext, cw_ref, tm, 0) + cb_ref[...]

    g = half(wg_ref, cwg_ref, cbg_ref, tail_g_ref, 0)
    v = half(wv_ref, cwv_ref, cbv_ref, tail_v_ref, 1)
    act_ref[...] = (_silu(g) * v).astype(act_ref.dtype)


def ffn_front(x2, gain, w_up, cw, cb, *, layer, seq_len, tm, tn, out_dtype):
    t, k = x2.shape
    f = w_up.shape[2] // 2
    assert t % tm == 0 and f % tn == 0 and seq_len % tm == 0
    nj = f // tn
    conv_k = cw.shape[0]
    cb2 = cb.reshape(1, 2 * f)
    return pl.pallas_call(
        functools.partial(_ffn_front_kernel, tiles_per_seq=seq_len // tm),
        grid=(t // tm, nj),
        in_specs=[
            pl.BlockSpec((tm, k), lambda i, j: (i, 0)),
            pl.BlockSpec((1, k), lambda i, j: (0, 0)),
            pl.BlockSpec((1, k, tn), lambda i, j: (layer, 0, j)),
            pl.BlockSpec((1, k, tn), lambda i, j: (layer, 0, j + nj)),
            pl.BlockSpec((conv_k, tn), lambda i, j: (0, j)),
            pl.BlockSpec((conv_k, tn), lambda i, j: (0, j + nj)),
            pl.BlockSpec((1, tn), lambda i, j: (0, j)),
            pl.BlockSpec((1, tn), lambda i, j: (0, j + nj)),
        ],
        out_specs=[pl.BlockSpec((tm, tn), lambda i, j: (i, j)),
                   pl.BlockSpec((1, CONV_HALO, tn), lambda i, j: (i, 0, j)),
                   pl.BlockSpec((1, CONV_HALO, tn), lambda i, j: (i, 0, j))],
        out_shape=[jax.ShapeDtypeStruct((t, f), out_dtype),
                   jax.ShapeDtypeStruct((t // tm, CONV_HALO, f), F32),
                   jax.ShapeDtypeStruct((t // tm, CONV_HALO, f), F32)],
        scratch_shapes=[pltpu.VMEM((tm, k), BF16), pltpu.VMEM((2, nj, CONV_HALO, tn), F32)],
        compiler_params=_cparams("arbitrary", "arbitrary"),
        name="ffn_front",
    )(x2, gain.reshape(1, k), w_up, w_up, cw, cw, cb2, cb2)


def _history_tail(prev, cur, hist):
    s = cur.shape[1]
    if s >= hist:
        return cur[:, s - hist:]
    return jnp.concatenate([prev[:, s:], cur], axis=1)


def _layer(x3, wts, lw, dims, states, ssm_out, mem_kv, *, layer, kv_layer, pos0, tiles):
    b, s, d = x3.shape
    t = b * s
    x2 = x3.reshape(t, d)
    G, R, P, N = dims["groups"], dims["heads_per_group"], dims["head_dim"], dims["n_state"]
    inner = G * R * P
    conv_dim = inner + 2 * G * N
    z_col, u_col, xbc_col, gp_col, gs_col = 0, inner, inner + d, inner + d + conv_dim, inner + 2 * d + conv_dim
    n_main = gs_col + d
    tm, tn, tn_k = tiles["tm"], tiles["tn"], tiles["tn_long_k"]
    act_dtype = tiles["act_dtype"]
    hp, hc, hf = POOL_HALO - 1, lw["ssm_conv_w"].shape[0] - 1, lw["ffn_conv_w"].shape[0] - 1

    if states is None:
        pool_prev = conv_prev = ssm_prev = ffn_prev = None
    else:
        pool_prev, conv_prev, ssm_prev, ffn_prev = states

    nu, nz = d // tn, inner // tn
    assert d % tn == 0 and inner % tn == 0
    main, dt = norm_matmul(x2, lw["norm_mix"], [wts["w_in_uzx"], wts["w_in_gates"]], wts["w_dt"], layer=layer,
                           tm=tiles["tm_norm"], tn=tn, name="in_proj",
                           out_block=lambda j: jnp.where(j < nu, j + nz, jnp.where(j < nu + nz, j - nu, j)))
    main3 = main.reshape(b, s, n_main)
    dt3 = dt.reshape(b, s, LANES)

    gated_pool = pool_branch(main3, pool_prev, wts["w_pool_group"], lw["pool_scale"], wts["w_pool_out"], layer=layer,
                             u_col=u_col, gate_col=gp_col, nb=tiles["pool_nb"], L=tiles["pool_L"], pos0=pos0)
    y, ssm_new = ssd_branch(main3, dt3, conv_prev, ssm_prev, ssm_out, lw["ssm_conv_w"], lw["ssm_conv_b"],
                            lw["dt_bias"], lw["a_log"], lw["d_exp"], lw["ssm_norm"], lw["expand"], layer=layer,
                            xbc_col=xbc_col, z_col=z_col, q=tiles["ssd_q"], nb=tiles["ssd_nb"], n_groups=G,
                            heads_per_group=R, head_dim=P, n_state=N, out_dtype=act_dtype)
    merged = matmul(y.reshape(t, inner), wts["w_ssm_out"],
                    [(gated_pool.reshape(t, d), 0), (main, gs_col)], _ep_gate_merge,
                    layer=layer, tm=tm, tn=tn_k, out_dtype=BF16, name="ssm_out_merge")
    x2 = matmul(merged, wts["w_out"], [(x2, 0)], _ep_residual, layer=layer, tm=tm, tn=tn, out_dtype=F32,
                name="mix_out")

    qm = norm_matmul(x2, lw["norm_mem_q"], wts["w_mem_q"], layer=layer, tm=tiles["tm_norm"], tn=tn, out_dtype=act_dtype,
                     name="mem_q")
    k4, v4 = mem_kv
    o = cross_attention(qm.reshape(b, s, d), k4, v4, layer=kv_layer, nb=tiles["attn_nb"], tq=tiles["attn_tq"],
                        n_heads=dims["mem_heads"], out_dtype=act_dtype)
    x2 = matmul(o.reshape(t, d), wts["w_mem_o"], [(x2, 0)], _ep_residual, layer=layer, tm=tm, tn=tn, out_dtype=F32,
                name="mem_o")

    if states is None:
        act2, tail_g, tail_v = ffn_front(x2, lw["norm_ffn"], wts["w_ffn_up"], lw["ffn_conv_w"], lw["ffn_conv_b"],
                                         layer=layer, seq_len=s, tm=tm, tn=tiles["ffn_tc"], out_dtype=act_dtype)
        per_seq = s // tm
        ffn_new = jnp.concatenate([tail_g[per_seq - 1::per_seq, CONV_HALO - hf:],
                                   tail_v[per_seq - 1::per_seq, CONV_HALO - hf:]], axis=-1)
    else:
        up = norm_matmul(x2, lw["norm_ffn"], wts["w_ffn_up"], layer=layer, tm=tiles["tm_norm"], tn=tn, name="ffn_up")
        up3 = up.reshape(b, s, up.shape[1])
        act2 = ffn_mid(up3, ffn_prev, lw["ffn_conv_w"], lw["ffn_conv_b"], layer=layer, nb=tiles["ffn_nb"],
                       L=tiles["ffn_L"], tc=tiles["ffn_tc"], out_dtype=act_dtype).reshape(t, -1)
        ffn_new = _history_tail(ffn_prev[layer], up3, hf)
    x2 = matmul(act2, wts["w_ffn_down"], [(x2, 0)], _ep_residual,
                layer=layer, tm=tiles["tm_down"], tn=tn_k, out_dtype=F32, name="ffn_down")

    u3 = main3[:, :, u_col:u_col + d]
    xbc3 = main3[:, :, xbc_col:xbc_col + conv_dim]
    if states is None:
        pool_new, conv_new = u3[:, s - hp:], xbc3[:, s - hc:]
    else:
        pool_new = _history_tail(pool_prev[layer], u3, hp)
        conv_new = _history_tail(conv_prev[layer], xbc3, hc)
    return x2.reshape(b, s, d), pool_new, conv_new, ssm_new, ffn_new


def kernel(x_prompt, x_sample, state_pool, state_ssm_conv, state_ssm, state_ffn_conv, cache_mem_k, cache_mem_v,
           mem_prompt, norm_mix, w_in, w_pool_group, pool_scale, w_pool_out, ssm_conv_w, ssm_conv_b, ssm_dt_bias,
           ssm_a_log, ssm_d, ssm_norm, w_ssm_out, w_out, norm_mem_q, w_mem_q, w_mem_o, norm_mem_kv, w_mem_k,
           w_mem_v, norm_ffn, w_ffn_up, ffn_conv_w, ffn_conv_b, w_ffn_down, norm_final):
    depth = w_in.shape[0]
    bp, sp, d = x_prompt.shape
    bs, ss, _ = x_sample.shape
    n_heads = ssm_d.shape[1]
    inner = w_ssm_out.shape[1]
    head_dim = inner // n_heads
    n_state = state_ssm.shape[-1]
    conv_dim = ssm_conv_w.shape[2]
    n_groups = (conv_dim - inner) // (2 * n_state)
    mem_heads = cache_mem_k.shape[3]
    mem_len = mem_prompt.shape[1]
    dims = dict(groups=n_groups, heads_per_group=n_heads // n_groups, head_dim=head_dim, n_state=n_state,
                mem_heads=mem_heads)
    assert n_heads <= LANES

    c_u, c_z, c_x, c_dt, c_gp = d, d + inner, d + inner + conv_dim, d + inner + conv_dim + n_heads, 2 * d + inner + conv_dim + n_heads
    head_of_col = jnp.arange(inner, dtype=jnp.int32) // head_dim
    expand = (jnp.arange(LANES, dtype=jnp.int32)[:, None] == head_of_col[None, :]).astype(BF16)

    def pad_heads(v):
        return jnp.pad(v, (0, LANES - n_heads)).reshape(1, LANES)

    tiles_p = dict(tm=1024, tm_norm=1024, tm_down=1024, tn=1024, tn_long_k=512, pool_nb=1, pool_L=512, ssd_q=128,
                   ssd_nb=1, attn_nb=1, attn_tq=512, ffn_nb=1, ffn_L=512, ffn_tc=512, act_dtype=BF16)
    tiles_s = dict(tm=512, tm_norm=1024, tm_down=512, tn=1024, tn_long_k=512, pool_nb=32, pool_L=ss, ssd_q=ss,
                   ssd_nb=2, attn_nb=2, attn_tq=ss, ffn_nb=64, ffn_L=ss, ffn_tc=512, act_dtype=F32)

    ssm_states = state_ssm.reshape(depth, bs, n_groups, inner // n_groups, n_state)
    kv_s = (cache_mem_k, cache_mem_v)
    states_s = (state_pool, state_ssm_conv, ssm_states, state_ffn_conv)
    ssm_s_all = None

    wts = dict(
        w_in_uzx=w_in[:, :, :c_x].astype(BF16), w_in_gates=w_in[:, :, c_dt:].astype(BF16),
        w_dt=jnp.pad(w_in[:, :, c_x:c_dt], ((0, 0), (0, 0), (0, LANES - n_heads))).astype(BF16),
        w_pool_group=w_pool_group.astype(BF16), w_pool_out=w_pool_out.astype(BF16),
        w_ssm_out=w_ssm_out.astype(BF16), w_out=w_out.astype(BF16),
        w_mem_q=w_mem_q.astype(BF16), w_mem_o=w_mem_o.astype(BF16),
        w_mem_k=w_mem_k.astype(BF16), w_mem_v=w_mem_v.astype(BF16),
        w_ffn_up=w_ffn_up.astype(BF16), w_ffn_down=w_ffn_down.astype(BF16),
    )

    yp, ys = x_prompt, x_sample
    outs = [[] for _ in range(9)]
    for i in range(depth):
        lw = dict(
            norm_mix=norm_mix[i], pool_scale=pool_scale[i],
            ssm_conv_w=ssm_conv_w[i], ssm_conv_b=ssm_conv_b[i],
            dt_bias=pad_heads(ssm_dt_bias[i]), a_log=pad_heads(ssm_a_log[i]),
            d_exp=jnp.repeat(ssm_d[i], head_dim).reshape(1, inner), ssm_norm=ssm_norm[i], expand=expand,
            norm_mem_q=norm_mem_q[i], norm_ffn=norm_ffn[i],
            ffn_conv_w=ffn_conv_w[i], ffn_conv_b=ffn_conv_b[i],
        )
        mem2 = mem_prompt.reshape(bp * mem_len, d)
        k_i = norm_matmul(mem2, norm_mem_kv[i], wts["w_mem_k"], layer=i, tm=512, tn=1024, name="mem_k")
        v_i = norm_matmul(mem2, norm_mem_kv[i], wts["w_mem_v"], layer=i, tm=512, tn=1024, name="mem_v")
        kv_p = (k_i.reshape(1, bp, mem_len, d), v_i.reshape(1, bp, mem_len, d))

        yp, a0, a1, a2, a3 = _layer(yp, wts, lw, dims, None, None, kv_p, layer=i, kv_layer=0, pos0=0, tiles=tiles_p)
        ys, b0, b1, ssm_s_all, b3 = _layer(ys, wts, lw, dims, states_s, ssm_s_all, kv_s, layer=i, kv_layer=i,
                                           pos0=PAST_LEN, tiles=tiles_s)
        for lst, val in zip(outs, (a0, b0, a1, b1, a2.reshape(bp, n_heads, head_dim, n_state), a3, b3,
                                   k_i.reshape(bp, mem_len, mem_heads, d // mem_heads),
                                   v_i.reshape(bp, mem_len, mem_heads, d // mem_heads))):
            lst.append(val)

    y_prompt = rmsnorm(yp.reshape(bp * sp, d), norm_final, tm=512).reshape(bp, sp, d)
    y_sample = rmsnorm(ys.reshape(bs * ss, d), norm_final, tm=512).reshape(bs, ss, d)
    pool_p, pool_s, conv_p, conv_s, ssm_p, ffn_p, ffn_s, mk_p, mv_p = (jnp.stack(lst) for lst in outs)
    ssm_s = ssm_s_all.reshape(depth, bs, n_heads, head_dim, n_state)
    return (y_prompt, y_sample, pool_p, pool_s, conv_p, conv_s, ssm_p, ssm_s, ffn_p, ffn_s, mk_p, mv_p)
```

```python
import functools
import math

import jax
import jax.numpy as jnp
from jax import lax
from jax.experimental import pallas as pl
from jax.experimental.pallas import tpu as pltpu

F32 = jnp.float32
BF16 = jnp.bfloat16
EPS = 1e-6

LANES = 128
SUBLANES = 8
VMEM_LIMIT = 56 * 1024 * 1024

PAST_LEN = 16384
POOL_WINDOWS = (2, 4, 8, 16)
POOL_HALO = 16
CONV_HALO = SUBLANES


def _cparams(*sem):
    return pltpu.CompilerParams(dimension_semantics=sem, vmem_limit_bytes=VMEM_LIMIT)


def _silu(x):
    h = 0.5 * x
    return h + h * jnp.tanh(h)


def _causal_conv(ext, w_ref, rows, axis):
    conv_k = w_ref.shape[0]
    tail = (slice(None),) * axis + (slice(CONV_HALO, CONV_HALO + rows),)
    acc = ext[tail] * w_ref[conv_k - 1:conv_k, :]
    for j in range(conv_k - 1):
        acc = acc + pltpu.roll(ext, conv_k - 1 - j, axis=axis)[tail] * w_ref[j:j + 1, :]
    return acc


def _norm_matmul_kernel(x_ref, g_ref, *rest, bounds, has_aux):
    n_parts = len(bounds) - 1
    w_refs, rest = rest[:n_parts], rest[n_parts:]
    if has_aux:
        wa_ref, o_ref, oa_ref, xn_ref = rest
    else:
        o_ref, xn_ref = rest
    j = pl.program_id(1)

    @pl.when(j == 0)
    def _():
        x = x_ref[...]
        ms = jnp.mean(x * x, axis=-1, keepdims=True)
        xn = (x * lax.rsqrt(ms + EPS) * g_ref[...]).astype(BF16)
        xn_ref[...] = xn
        if has_aux:
            oa_ref[...] = jnp.dot(xn, wa_ref[0], preferred_element_type=F32)

    if n_parts == 1:
        o_ref[...] = jnp.dot(xn_ref[...], w_refs[0][0], preferred_element_type=F32).astype(o_ref.dtype)
    else:
        for p in range(n_parts):
            @pl.when((j >= bounds[p]) & (j < bounds[p + 1]))
            def _(p=p):
                o_ref[...] = jnp.dot(xn_ref[...], w_refs[p][0], preferred_element_type=F32).astype(o_ref.dtype)


def norm_matmul(x, g, w, w_aux=None, *, layer, tm, tn, out_dtype=F32, out_block=None, name):
    t, k = x.shape
    parts, bounds = [], [0]
    for part in (w if isinstance(w, list) else [w]):
        arr, ncols = part if isinstance(part, tuple) else (part, part.shape[2])
        assert ncols % tn == 0 and ncols <= arr.shape[2]
        parts.append(arr)
        bounds.append(bounds[-1] + ncols // tn)
    n = bounds[-1] * tn
    assert t % tm == 0
    has_aux = w_aux is not None
    in_specs = [
        pl.BlockSpec((tm, k), lambda i, j: (i, 0)),
        pl.BlockSpec((1, k), lambda i, j: (0, 0)),
    ]
    for lo, hi in zip(bounds[:-1], bounds[1:]):
        in_specs.append(pl.BlockSpec((1, k, tn), lambda i, j, lo=lo, hi=hi: (layer, 0, jnp.clip(j - lo, 0, hi - lo - 1))))
    args = [x, g.reshape(1, k)] + parts
    out_block = out_block or (lambda j: j)
    out_shape = [jax.ShapeDtypeStruct((t, n), out_dtype)]
    out_specs = [pl.BlockSpec((tm, tn), lambda i, j: (i, out_block(j)))]
    if has_aux:
        na = w_aux.shape[2]
        in_specs.append(pl.BlockSpec((1, k, na), lambda i, j: (layer, 0, 0)))
        args.append(w_aux)
        out_shape.append(jax.ShapeDtypeStruct((t, na), F32))
        out_specs.append(pl.BlockSpec((tm, na), lambda i, j: (i, 0)))
    res = pl.pallas_call(
        functools.partial(_norm_matmul_kernel, bounds=tuple(bounds), has_aux=has_aux),
        grid=(t // tm, n // tn),
        in_specs=in_specs,
        out_specs=out_specs,
        out_shape=out_shape,
        scratch_shapes=[pltpu.VMEM((tm, k), BF16)],
        compiler_params=_cparams("parallel", "arbitrary"),
        name=name,
    )(*args)
    return res if has_aux else res[0]


def _matmul_kernel(a_ref, w_ref, *rest, epilogue):
    *extra, o_ref = rest
    acc = jnp.dot(a_ref[...].astype(BF16), w_ref[0], preferred_element_type=F32)
    o_ref[...] = epilogue(acc, *[e[...] for e in extra]).astype(o_ref.dtype)


def _ep_residual(acc, r):
    return r + acc


def _ep_gate_merge(acc, gated_pool, g_ssm):
    return gated_pool + jax.nn.sigmoid(g_ssm) * acc


def matmul(a, w, extras, epilogue, *, layer, tm, tn, out_dtype, name):
    t, k = a.shape
    n = w.shape[2]
    assert t % tm == 0 and n % tn == 0
    in_specs = [
        pl.BlockSpec((tm, k), lambda i, j: (i, 0)),
        pl.BlockSpec((1, k, tn), lambda i, j: (layer, 0, j)),
    ]
    args = [a, w]
    for arr, off in extras:
        assert off % tn == 0
        ob = off // tn
        in_specs.append(pl.BlockSpec((tm, tn), lambda i, j, ob=ob: (i, j + ob)))
        args.append(arr)
    return pl.pallas_call(
        functools.partial(_matmul_kernel, epilogue=epilogue),
        grid=(t // tm, n // tn),
        in_specs=in_specs,
        out_specs=pl.BlockSpec((tm, tn), lambda i, j: (i, j)),
        out_shape=jax.ShapeDtypeStruct((t, n), out_dtype),
        compiler_params=_cparams("parallel", "arbitrary"),
        name=name,
    )(*args)


def _rmsnorm_kernel(x_ref, g_ref, o_ref):
    x = x_ref[...]
    ms = jnp.mean(x * x, axis=-1, keepdims=True)
    o_ref[...] = x * lax.rsqrt(ms + EPS) * g_ref[...]


def rmsnorm(x, g, *, tm):
    t, k = x.shape
    return pl.pallas_call(
        _rmsnorm_kernel,
        grid=(t // tm,),
        in_specs=[pl.BlockSpec((tm, k), lambda i: (i, 0)), pl.BlockSpec((1, k), lambda i: (0, 0))],
        out_specs=pl.BlockSpec((tm, k), lambda i: (i, 0)),
        out_shape=jax.ShapeDtypeStruct((t, k), F32),
        compiler_params=_cparams("parallel"),
    )(x, g.reshape(1, k))


def _pool_kernel(cur_ref, prev_ref, gate_ref, wg_ref, scale_ref, wo_ref, o_ref, ext_ref, pooled_ref,
                 *, nb, L, from_state, pos0):
    w_ch = cur_ref.shape[-1]
    gdim = w_ch // len(POOL_WINDOWS)
    if from_state:
        ext_ref[:, 1:POOL_HALO, :] = prev_ref[0]
        pos_start = pos0
    else:
        i = pl.program_id(1)
        ext_ref[:, 0:POOL_HALO, :] = jnp.where(i == 0, 0.0, prev_ref[...])
        pos_start = pos0 + i * L
    ext_ref[:, POOL_HALO:POOL_HALO + L, :] = cur_ref[...]

    pos = pos_start + lax.broadcasted_iota(jnp.int32, (1, L, gdim), 1)
    for k, win in enumerate(POOL_WINDOWS):
        cs = slice(k * gdim, (k + 1) * gdim)
        cur = ext_ref[:, POOL_HALO:POOL_HALO + L, cs]
        acc = cur
        for j in range(1, win):
            acc = acc + ext_ref[:, POOL_HALO - j:POOL_HALO - j + L, cs]
        count = jnp.minimum(pos + 1, win).astype(F32)
        diff = (acc / count - cur).reshape(nb * L, gdim).astype(BF16)
        mixed = jnp.dot(diff, wg_ref[0, k], preferred_element_type=F32)
        pooled_ref[:, cs] = (mixed * scale_ref[:, cs]).astype(BF16)
    out_pool = jnp.dot(pooled_ref[...], wo_ref[0], preferred_element_type=F32)
    gate = jax.nn.sigmoid(gate_ref[...].reshape(nb * L, w_ch))
    o_ref[...] = (gate * out_pool).reshape(nb, L, w_ch)


def pool_branch(main3, prev_state, wg, scale, wo, *, layer, u_col, gate_col, nb, L, pos0):
    b, s, _ = main3.shape
    w_ch = wo.shape[1]
    ub, gb = u_col // w_ch, gate_col // w_ch
    from_state = prev_state is not None
    if from_state:
        assert L == s
        prev = prev_state
        prev_spec = pl.BlockSpec((1, nb, POOL_HALO - 1, w_ch), lambda n, i: (layer, n, 0, 0))
    else:
        assert nb == 1 and L % POOL_HALO == 0
        prev = main3
        r = L // POOL_HALO
        prev_spec = pl.BlockSpec((1, POOL_HALO, w_ch), lambda n, i: (n, jnp.maximum(i * r - 1, 0), ub))
    return pl.pallas_call(
        functools.partial(_pool_kernel, nb=nb, L=L, from_state=from_state, pos0=pos0),
        grid=(b // nb, s // L),
        in_specs=[
            pl.BlockSpec((nb, L, w_ch), lambda n, i: (n, i, ub)),
            prev_spec,
            pl.BlockSpec((nb, L, w_ch), lambda n, i: (n, i, gb)),
            pl.BlockSpec((1,) + wg.shape[1:], lambda n, i: (layer, 0, 0, 0)),
            pl.BlockSpec((1, w_ch), lambda n, i: (0, 0)),
            pl.BlockSpec((1,) + wo.shape[1:], lambda n, i: (layer, 0, 0)),
        ],
        out_specs=pl.BlockSpec((nb, L, w_ch), lambda n, i: (n, i, 0)),
        out_shape=jax.ShapeDtypeStruct((b, s, w_ch), F32),
        scratch_shapes=[pltpu.VMEM((nb, POOL_HALO + L, w_ch), F32), pltpu.VMEM((nb * L, w_ch), BF16)],
        compiler_params=_cparams("parallel", "arbitrary"),
        name="pool_state" if from_state else "pool_scan",
    )(main3, prev, main3, wg, scale.reshape(1, w_ch), wo)


def _cumsum_rows(x):
    q = x.shape[0]
    row = lax.broadcasted_iota(jnp.int32, x.shape, 0)
    k = 1
    while k < q:
        x = x + jnp.where(row >= k, pltpu.roll(x, k, axis=0), 0.0)
        k *= 2
    return x


def _expand_heads(vals, e_ref):
    q = vals[0].shape[0]
    pieces = []
    for v in vals:
        hi = v.astype(BF16).astype(F32)
        r1 = v - hi
        mid = r1.astype(BF16).astype(F32)
        pieces += [hi, mid, r1 - mid]
    lhs = jnp.concatenate(pieces, axis=0).astype(BF16)
    out = jnp.dot(lhs, e_ref[...], preferred_element_type=F32)
    return [out[(3 * i) * q:(3 * i + 1) * q] + out[(3 * i + 1) * q:(3 * i + 2) * q] + out[(3 * i + 2) * q:(3 * i + 3) * q]
            for i in range(len(vals))]


def _ssd_kernel(*refs, q, nb, n_groups, heads_per_group, head_dim, n_state, from_state, conv_k):
    if from_state:
        (xbc_ref, prev_ref, z_ref, dt_ref, h0_ref, cw_ref, cb_ref, dtb_ref, alog_ref, dexp_ref, norm_ref, e_ref,
         *_, y_ref, hout_ref, ext_ref, act_ref, yacc_ref) = refs
        params = (dtb_ref, alog_ref, dexp_ref, norm_ref, e_ref)
        hist = conv_k - 1
        for n in range(nb):
            ext = ext_ref.at[n]
            ext[0:CONV_HALO, :] = jnp.zeros((CONV_HALO, ext.shape[1]), F32)
            ext[CONV_HALO - hist:CONV_HALO, :] = prev_ref[0, n]
            ext[CONV_HALO:CONV_HALO + q, :] = xbc_ref[n]
            act_ref[n] = _silu(_causal_conv(ext[...], cw_ref, q, 0) + cb_ref[...])
        for n in range(nb):
            _ssd_chunk(act_ref.at[n], z_ref.at[n], dt_ref.at[n], h0_ref.at[0, n], hout_ref.at[0, n], y_ref.at[n],
                       yacc_ref.at[n], params, q=q, n_groups=n_groups, heads_per_group=heads_per_group,
                       head_dim=head_dim, n_state=n_state)
    else:
        (xbc_ref, prev_ref, z_ref, dt_ref, cw_ref, cb_ref, dtb_ref, alog_ref, dexp_ref, norm_ref, e_ref,
         y_ref, hout_ref, ext_ref, act_ref, h_ref, yacc_ref) = refs
        params = (dtb_ref, alog_ref, dexp_ref, norm_ref, e_ref)
        c = pl.program_id(1)

        @pl.when(c == 0)
        def _():
            h_ref[...] = jnp.zeros_like(h_ref)

        ext = ext_ref.at[0]
        ext[0:CONV_HALO, :] = jnp.where(c == 0, 0.0, prev_ref[0])
        ext[CONV_HALO:CONV_HALO + q, :] = xbc_ref[0]
        act_ref[0] = _silu(_causal_conv(ext[...], cw_ref, q, 0) + cb_ref[...])
        _ssd_chunk(act_ref.at[0], z_ref.at[0], dt_ref.at[0], h_ref, h_ref, y_ref.at[0], yacc_ref.at[0], params,
                   q=q, n_groups=n_groups, heads_per_group=heads_per_group, head_dim=head_dim, n_state=n_state)

        @pl.when(c == pl.num_programs(1) - 1)
        def _():
            hout_ref[0] = h_ref[...]


def _ssd_chunk(act_ref, z_ref, dt_ref, h_in, h_out, y_ref, yacc_ref, params,
               *, q, n_groups, heads_per_group, head_dim, n_state):
    dtb_ref, alog_ref, dexp_ref, norm_ref, e_ref = params
    gw = heads_per_group * head_dim
    inner = n_groups * gw

    dt = jax.nn.softplus(dt_ref[...] + dtb_ref[...])
    la_cs = _cumsum_rows(dt * (-jnp.exp(alog_ref[...])))
    la_cs_t = la_cs.T
    dt_x, la_x = _expand_heads([dt, la_cs], e_ref)
    from_start_x = jnp.exp(la_x)
    to_end_x = jnp.exp(la_x[q - 1:q, :] - la_x)
    chunk_decay = jnp.exp(la_cs[q - 1:q, :])

    xs = act_ref[:, 0:inner]
    xd = xs * dt_x
    xdw = xd * to_end_x
    yacc_ref[...] = dexp_ref[...] * xs

    tri = lax.broadcasted_iota(jnp.int32, (q, q), 0) >= lax.broadcasted_iota(jnp.int32, (q, q), 1)
    assert LANES % head_dim == 0
    hpl = LANES // head_dim
    lane_head = lax.broadcasted_iota(jnp.int32, (q, LANES), 1) // head_dim
    bms, cbs = [], []
    for g in range(n_groups):
        gs = slice(g * gw, (g + 1) * gw)
        bm = act_ref[:, inner + g * n_state:inner + (g + 1) * n_state].astype(BF16)
        cm = act_ref[:, inner + (n_groups + g) * n_state:inner + (n_groups + g + 1) * n_state].astype(BF16)
        bms.append(bm)
        cbs.append(lax.dot_general(cm, bm, (((1,), (1,)), ((), ())), preferred_element_type=F32))
        y_off = lax.dot_general(cm, h_in[g].astype(BF16), (((1,), (1,)), ((), ())),
                                preferred_element_type=F32)
        yacc_ref[:, gs] += y_off * from_start_x[:, gs]
    for g in range(n_groups):
        cb = cbs[g]
        for lt in range(gw // LANES):
            ls = slice(g * gw + lt * LANES, g * gw + (lt + 1) * LANES)
            xd_t = xd[:, ls]
            y_t = None
            for k in range(hpl):
                hd = (g * gw + lt * LANES) // head_dim + k
                seg = la_cs[:, hd:hd + 1] - la_cs_t[hd:hd + 1, :]
                m = (cb * jnp.exp(jnp.where(tri, seg, -jnp.inf))).astype(BF16)
                rhs = jnp.where(lane_head == k, xd_t, 0.0).astype(BF16)
                part = jnp.dot(m, rhs, preferred_element_type=F32)
                y_t = part if y_t is None else y_t + part
            yacc_ref[:, ls] += y_t
    for g in range(n_groups):
        gs = slice(g * gw, (g + 1) * gw)
        s_new = lax.dot_general(xdw[:, gs].astype(BF16), bms[g], (((0,), (0,)), ((), ())),
                                preferred_element_type=F32)
        for r in range(heads_per_group):
            hd = g * heads_per_group + r
            rows = slice(r * head_dim, (r + 1) * head_dim)
            h_out[g, rows, :] = h_in[g, rows, :] * chunk_decay[:, hd:hd + 1] + s_new[rows, :]

    y = yacc_ref[...] * _silu(z_ref[...])
    for g in range(n_groups):
        gs = slice(g * gw, (g + 1) * gw)
        yg = y[:, gs]
        ms = jnp.mean(yg * yg, axis=-1, keepdims=True)
        y_ref[:, gs] = (yg * lax.rsqrt(ms + EPS) * norm_ref[:, gs]).astype(y_ref.dtype)


def ssd_branch(main3, dt3, conv_state, ssm_state, state_out, cw, cb, dt_bias, a_log, d_exp, norm, expand,
               *, layer, xbc_col, z_col, q, nb, n_groups, heads_per_group, head_dim, n_state, out_dtype):
    b, s, _ = main3.shape
    gw = heads_per_group * head_dim
    inner = n_groups * gw
    conv_dim = inner + 2 * n_groups * n_state
    conv_k = cw.shape[0]
    from_state = ssm_state is not None
    xb, zb = xbc_col // conv_dim, z_col // inner
    assert xbc_col % conv_dim == 0 and z_col % inner == 0

    def const(shape):
        nd = len(shape)
        return pl.BlockSpec(shape, lambda n, c: (0,) * nd)

    assert b % nb == 0 and (from_state or nb == 1)
    in_specs = [pl.BlockSpec((nb, q, conv_dim), lambda n, c: (n, c, xb))]
    args = [main3]
    if from_state:
        assert q == s
        in_specs.append(pl.BlockSpec((1, nb, conv_k - 1, conv_dim), lambda n, c: (layer, n, 0, 0)))
        args.append(conv_state)
    else:
        assert q % CONV_HALO == 0
        r = q // CONV_HALO
        in_specs.append(pl.BlockSpec((1, CONV_HALO, conv_dim), lambda n, c: (n, jnp.maximum(c * r - 1, 0), xb)))
        args.append(main3)
    in_specs += [pl.BlockSpec((nb, q, inner), lambda n, c: (n, c, zb)),
                 pl.BlockSpec((nb, q, LANES), lambda n, c: (n, c, 0))]
    args += [main3, dt3]
    if from_state:
        in_specs.append(pl.BlockSpec((1, nb, n_groups, gw, n_state), lambda n, c: (layer, n, 0, 0, 0)))
        args.append(ssm_state)
    small = [cw, cb.reshape(1, conv_dim), dt_bias, a_log, d_exp, norm.reshape(1, inner), expand]
    in_specs += [const(a.shape) for a in small]
    args += small
    aliases = {}
    if from_state:
        state_spec = pl.BlockSpec((1, nb, n_groups, gw, n_state), lambda n, c: (layer, n, 0, 0, 0))
        state_shape = jax.ShapeDtypeStruct(ssm_state.shape, F32)
        if state_out is not None:
            aliases = {len(args): 1}
            in_specs.append(pl.BlockSpec(memory_space=pl.ANY))
            args.append(state_out)
    else:
        state_spec = pl.BlockSpec((1, n_groups, gw, n_state), lambda n, c: (n, 0, 0, 0))
        state_shape = jax.ShapeDtypeStruct((b, n_groups, gw, n_state), F32)
    return pl.pallas_call(
        functools.partial(_ssd_kernel, q=q, nb=nb, n_groups=n_groups, heads_per_group=heads_per_group,
                          head_dim=head_dim, n_state=n_state, from_state=from_state, conv_k=conv_k),
        grid=(b // nb, s // q),
        in_specs=in_specs,
        out_specs=[pl.BlockSpec((nb, q, inner), lambda n, c: (n, c, 0)), state_spec],
        out_shape=[jax.ShapeDtypeStruct((b, s, inner), out_dtype), state_shape],
        input_output_aliases=aliases,
        scratch_shapes=[pltpu.VMEM((nb, CONV_HALO + q, conv_dim), F32),
                        pltpu.VMEM((nb, q, conv_dim), F32)]
                       + ([] if from_state else [pltpu.VMEM((n_groups, gw, n_state), F32)])
                       + [pltpu.VMEM((nb, q, inner), F32)],
        compiler_params=_cparams("parallel", "arbitrary"),
        name="ssd_state" if from_state else "ssd_scan",
    )(*args)


def _kv_head(ref, n, h, d, stage_ref, slot):
    if len(ref.shape) == 5:
        stage_ref[slot] = ref[0, n, :, h, :]
        return stage_ref[slot].astype(BF16)
    return ref[0, n, :, h * d:(h + 1) * d].astype(BF16)


def _attn_kernel(q_ref, k_ref, v_ref, o_ref, *stage, n_heads, nb):
    stage_ref = stage[0] if stage else None
    d = q_ref.shape[-1] // n_heads
    inv = 1.0 / math.sqrt(d)
    units = [(n, h, slice(h * d, (h + 1) * d)) for n in range(nb) for h in range(n_heads)]
    scores = []
    for u, (n, h, hs) in enumerate(units):
        qh = q_ref[n, :, hs].astype(BF16)
        kh = _kv_head(k_ref, n, h, d, stage_ref, 2 * u)
        scores.append(lax.dot_general(qh, kh, (((1,), (1,)), ((), ())), preferred_element_type=F32) * inv)
    probs = []
    for s in scores:
        e = jnp.exp(s - jnp.max(s, axis=-1, keepdims=True))
        probs.append((e / jnp.sum(e, axis=-1, keepdims=True)).astype(BF16))
    for u, ((n, h, hs), p) in enumerate(zip(units, probs)):
        vh = _kv_head(v_ref, n, h, d, stage_ref, 2 * u + 1)
        o_ref[n, :, hs] = jnp.dot(p, vh, preferred_element_type=F32).astype(o_ref.dtype)


def _attn_few_kernel(q_ref, k_ref, v_ref, o_ref, *stage, n_heads, nb):
    stage_ref = stage[0] if stage else None
    tq = q_ref.shape[1]
    d = q_ref.shape[-1] // n_heads
    inv = 1.0 / math.sqrt(d)
    pad = jnp.zeros((LANES - tq, d), F32)
    units = [(n, h, slice(h * d, (h + 1) * d)) for n in range(nb) for h in range(n_heads)]
    scores_t = []
    for u, (n, h, hs) in enumerate(units):
        qh = jnp.concatenate([q_ref[n, :, hs].astype(F32), pad], axis=0).astype(BF16)
        kh = _kv_head(k_ref, n, h, d, stage_ref, 2 * u)
        scores_t.append(lax.dot_general(kh, qh, (((1,), (1,)), ((), ())), preferred_element_type=F32) * inv)
    probs = []
    for st in scores_t:
        e = jnp.exp(st - jnp.max(st, axis=0, keepdims=True))
        pt = e / jnp.sum(e, axis=0, keepdims=True)
        probs.append(pt.T[0:tq, :].astype(BF16))
    for u, ((n, h, hs), p) in enumerate(zip(units, probs)):
        vh = _kv_head(v_ref, n, h, d, stage_ref, 2 * u + 1)
        o_ref[n, :, hs] = jnp.dot(p, vh, preferred_element_type=F32).astype(o_ref.dtype)


def cross_attention(q3, k4, v4, *, layer, nb, tq, n_heads, out_dtype):
    b, s, dm = q3.shape
    m = k4.shape[2]
    body = _attn_few_kernel if tq < LANES else _attn_kernel
    zeros = (0,) * (k4.ndim - 2)
    kv_spec = pl.BlockSpec((1, nb) + k4.shape[2:], lambda n, i: (layer, n) + zeros)
    stage = [pltpu.VMEM((2 * nb * n_heads, m, dm // n_heads), F32)] if k4.ndim == 5 else []
    return pl.pallas_call(
        functools.partial(body, n_heads=n_heads, nb=nb),
        grid=(b // nb, s // tq),
        in_specs=[pl.BlockSpec((nb, tq, dm), lambda n, i: (n, i, 0)), kv_spec, kv_spec],
        out_specs=pl.BlockSpec((nb, tq, dm), lambda n, i: (n, i, 0)),
        out_shape=jax.ShapeDtypeStruct((b, s, dm), out_dtype),
        scratch_shapes=stage,
        compiler_params=_cparams("parallel", "arbitrary"),
        name="cross_attention",
    )(q3, k4, v4)


def _ffn_mid_kernel(g_ref, v_ref, gp_ref, vp_ref, gw_ref, vw_ref, gb_ref, vb_ref, o_ref, gext_ref, vext_ref,
                    *, L, from_state, conv_k):
    hist = conv_k - 1
    i = pl.program_id(1)

    def conv(cur_ref, prev_ref, w_ref, b_ref, ext_ref):
        if from_state:
            ext_ref[:, 0:CONV_HALO, :] = jnp.zeros((ext_ref.shape[0], CONV_HALO, ext_ref.shape[2]), F32)
            ext_ref[:, CONV_HALO - hist:CONV_HALO, :] = prev_ref[0]
        else:
            ext_ref[:, 0:CONV_HALO, :] = jnp.where(i == 0, 0.0, prev_ref[...])
        ext_ref[:, CONV_HALO:CONV_HALO + L, :] = cur_ref[...]
        return _causal_conv(ext_ref[...], w_ref, L, 1) + b_ref[...]

    g = conv(g_ref, gp_ref, gw_ref, gb_ref, gext_ref)
    v = conv(v_ref, vp_ref, vw_ref, vb_ref, vext_ref)
    o_ref[...] = (_silu(g) * v).astype(o_ref.dtype)


def ffn_mid(up3, prev_state, cw, cb, *, layer, nb, L, tc, out_dtype):
    b, s, two_f = up3.shape
    f = two_f // 2
    conv_k = cw.shape[0]
    assert f % tc == 0
    nj = f // tc
    from_state = prev_state is not None
    if from_state:
        assert L == s
        prev = prev_state
        gp_spec = pl.BlockSpec((1, nb, conv_k - 1, tc), lambda n, i, j: (layer, n, 0, j))
        vp_spec = pl.BlockSpec((1, nb, conv_k - 1, tc), lambda n, i, j: (layer, n, 0, j + nj))
    else:
        assert L % CONV_HALO == 0
        prev = up3
        r = L // CONV_HALO
        gp_spec = pl.BlockSpec((nb, CONV_HALO, tc), lambda n, i, j: (n, jnp.maximum(i * r - 1, 0), j))
        vp_spec = pl.BlockSpec((nb, CONV_HALO, tc), lambda n, i, j: (n, jnp.maximum(i * r - 1, 0), j + nj))
    cb2 = cb.reshape(1, two_f)
    return pl.pallas_call(
        functools.partial(_ffn_mid_kernel, L=L, from_state=from_state, conv_k=conv_k),
        grid=(b // nb, s // L, nj),
        in_specs=[
            pl.BlockSpec((nb, L, tc), lambda n, i, j: (n, i, j)),
            pl.BlockSpec((nb, L, tc), lambda n, i, j: (n, i, j + nj)),
            gp_spec, vp_spec,
            pl.BlockSpec((conv_k, tc), lambda n, i, j: (0, j)),
            pl.BlockSpec((conv_k, tc), lambda n, i, j: (0, j + nj)),
            pl.BlockSpec((1, tc), lambda n, i, j: (0, j)),
            pl.BlockSpec((1, tc), lambda n, i, j: (0, j + nj)),
        ],
        out_specs=pl.BlockSpec((nb, L, tc), lambda n, i, j: (n, i, j)),
        out_shape=jax.ShapeDtypeStruct((b, s, f), out_dtype),
        scratch_shapes=[pltpu.VMEM((nb, CONV_HALO + L, tc), F32), pltpu.VMEM((nb, CONV_HALO + L, tc), F32)],
        compiler_params=_cparams("parallel", "arbitrary", "arbitrary"),
        name="ffn_mid_state" if from_state else "ffn_mid_scan",
    )(up3, up3, prev, prev, cw, cw, cb2, cb2)


def _ffn_front_kernel(x_ref, g_ref, wg_ref, wv_ref, cwg_ref, cwv_ref, cbg_ref, cbv_ref,
                      act_ref, tail_g_ref, tail_v_ref, xn_ref, halo_ref, *, tiles_per_seq):
    i, j = pl.program_id(0), pl.program_id(1)
    tm = x_ref.shape[0]

    @pl.when(j == 0)
    def _():
        x = x_ref[...]
        ms = jnp.mean(x * x, axis=-1, keepdims=True)
        xn_ref[...] = (x * lax.rsqrt(ms + EPS) * g_ref[...]).astype(BF16)

    xn = xn_ref[...]
    seq_start = (i % tiles_per_seq) == 0

    def half(w_ref, cw_ref, cb_ref, tail_ref, slot):
        up = jnp.dot(xn, w_ref[0], preferred_element_type=F32)
        prev = jnp.where(seq_start, 0.0, halo_ref[slot, j])
        tail = up[tm - CONV_HALO:tm, :]
        halo_ref[slot, j] = tail
        tail_ref[0] = tail
        ext = jnp.concatenate([prev, up], axis=0)
        return _causal_conv(ext, cw_ref, tm, 0) + cb_ref[...]

    g = half(wg_ref, cwg_ref, cbg_ref, tail_g_ref, 0)
    v = half(wv_ref, cwv_ref, cbv_ref, tail_v_ref, 1)
    act_ref[...] = (_silu(g) * v).astype(act_ref.dtype)


def ffn_front(x2, gain, w_up, cw, cb, *, layer, seq_len, tm, tn, out_dtype):
    t, k = x2.shape
    f = w_up.shape[2] // 2
    assert t % tm == 0 and f % tn == 0 and seq_len % tm == 0
    nj = f // tn
    conv_k = cw.shape[0]
    cb2 = cb.reshape(1, 2 * f)
    return pl.pallas_call(
        functools.partial(_ffn_front_kernel, tiles_per_seq=seq_len // tm),
        grid=(t // tm, nj),
        in_specs=[
            pl.BlockSpec((tm, k), lambda i, j: (i, 0)),
            pl.BlockSpec((1, k), lambda i, j: (0, 0)),
            pl.BlockSpec((1, k, tn), lambda i, j: (layer, 0, j)),
            pl.BlockSpec((1, k, tn), lambda i, j: (layer, 0, j + nj)),
            pl.BlockSpec((conv_k, tn), lambda i, j: (0, j)),
            pl.BlockSpec((conv_k, tn), lambda i, j: (0, j + nj)),
            pl.BlockSpec((1, tn), lambda i, j: (0, j)),
            pl.BlockSpec((1, tn), lambda i, j: (0, j + nj)),
        ],
        out_specs=[pl.BlockSpec((tm, tn), lambda i, j: (i, j)),
                   pl.BlockSpec((1, CONV_HALO, tn), lambda i, j: (i, 0, j)),
                   pl.BlockSpec((1, CONV_HALO, tn), lambda i, j: (i, 0, j))],
        out_shape=[jax.ShapeDtypeStruct((t, f), out_dtype),
                   jax.ShapeDtypeStruct((t // tm, CONV_HALO, f), F32),
                   jax.ShapeDtypeStruct((t // tm, CONV_HALO, f), F32)],
        scratch_shapes=[pltpu.VMEM((tm, k), BF16), pltpu.VMEM((2, nj, CONV_HALO, tn), F32)],
        compiler_params=_cparams("arbitrary", "arbitrary"),
        name="ffn_front",
    )(x2, gain.reshape(1, k), w_up, w_up, cw, cw, cb2, cb2)


def _history_tail(prev, cur, hist):
    s = cur.shape[1]
    if s >= hist:
        return cur[:, s - hist:]
    return jnp.concatenate([prev[:, s:], cur], axis=1)


def _layer(x3, wts, lw, dims, states, ssm_out, mem_kv, *, layer, kv_layer, pos0, tiles):
    b, s, d = x3.shape
    t = b * s
    x2 = x3.reshape(t, d)
    G, R, P, N = dims["groups"], dims["heads_per_group"], dims["head_dim"], dims["n_state"]
    inner = G * R * P
    conv_dim = inner + 2 * G * N
    z_col, u_col, xbc_col, gp_col, gs_col = 0, inner, inner + d, inner + d + conv_dim, inner + 2 * d + conv_dim
    n_main = gs_col + d
    tm, tn, tn_k = tiles["tm"], tiles["tn"], tiles["tn_long_k"]
    act_dtype = tiles["act_dtype"]
    hp, hc, hf = POOL_HALO - 1, lw["ssm_conv_w"].shape[0] - 1, lw["ffn_conv_w"].shape[0] - 1

    if states is None:
        pool_prev = conv_prev = ssm_prev = ffn_prev = None
    else:
        pool_prev, conv_prev, ssm_prev, ffn_prev = states

    nu, nz = d // tn, inner // tn
    assert d % tn == 0 and inner % tn == 0
    main, dt = norm_matmul(x2, lw["norm_mix"], [(wts["w_in_uzx"], gp_col), wts["w_in_gates"]], wts["w_dt"], layer=layer,
                           tm=tiles["tm_norm"], tn=tn, name="in_proj",
                           out_block=lambda j: jnp.where(j < nu, j + nz, jnp.where(j < nu + nz, j - nu, j)))
    main3 = main.reshape(b, s, n_main)
    dt3 = dt.reshape(b, s, LANES)

    gated_pool = pool_branch(main3, pool_prev, wts["w_pool_group"], lw["pool_scale"], wts["w_pool_out"], layer=layer,
                             u_col=u_col, gate_col=gp_col, nb=tiles["pool_nb"], L=tiles["pool_L"], pos0=pos0)
    y, ssm_new = ssd_branch(main3, dt3, conv_prev, ssm_prev, ssm_out, lw["ssm_conv_w"], lw["ssm_conv_b"],
                            lw["dt_bias"], lw["a_log"], lw["d_exp"], lw["ssm_norm"], lw["expand"], layer=layer,
                            xbc_col=xbc_col, z_col=z_col, q=tiles["ssd_q"], nb=tiles["ssd_nb"], n_groups=G,
                            heads_per_group=R, head_dim=P, n_state=N, out_dtype=act_dtype)
    merged = matmul(y.reshape(t, inner), wts["w_ssm_out"],
                    [(gated_pool.reshape(t, d), 0), (main, gs_col)], _ep_gate_merge,
                    layer=layer, tm=tm, tn=tn_k, out_dtype=BF16, name="ssm_out_merge")
    x2 = matmul(merged, wts["w_out"], [(x2, 0)], _ep_residual, layer=layer, tm=tm, tn=tn, out_dtype=F32,
                name="mix_out")

    qm = norm_matmul(x2, lw["norm_mem_q"], wts["w_mem_q"], layer=layer, tm=tiles["tm_norm"], tn=tn, out_dtype=act_dtype,
                     name="mem_q")
    k4, v4 = mem_kv
    o = cross_attention(qm.reshape(b, s, d), k4, v4, layer=kv_layer, nb=tiles["attn_nb"], tq=tiles["attn_tq"],
                        n_heads=dims["mem_heads"], out_dtype=act_dtype)
    x2 = matmul(o.reshape(t, d), wts["w_mem_o"], [(x2, 0)], _ep_residual, layer=layer, tm=tm, tn=tn, out_dtype=F32,
                name="mem_o")

    if states is None:
        act2, tail_g, tail_v = ffn_front(x2, lw["norm_ffn"], wts["w_ffn_up"], lw["ffn_conv_w"], lw["ffn_conv_b"],
                                         layer=layer, seq_len=s, tm=tm, tn=tiles["ffn_tc"], out_dtype=act_dtype)
        per_seq = s // tm
        ffn_new = jnp.concatenate([tail_g[per_seq - 1::per_seq, CONV_HALO - hf:],
                                   tail_v[per_seq - 1::per_seq, CONV_HALO - hf:]], axis=-1)
    else:
        up = norm_matmul(x2, lw["norm_ffn"], wts["w_ffn_up"], layer=layer, tm=tiles["tm_norm"], tn=tn, name="ffn_up")
        up3 = up.reshape(b, s, up.shape[1])
        act2 = ffn_mid(up3, ffn_prev, lw["ffn_conv_w"], lw["ffn_conv_b"], layer=layer, nb=tiles["ffn_nb"],
                       L=tiles["ffn_L"], tc=tiles["ffn_tc"], out_dtype=act_dtype).reshape(t, -1)
        ffn_new = _history_tail(ffn_prev[layer], up3, hf)
    x2 = matmul(act2, wts["w_ffn_down"], [(x2, 0)], _ep_residual,
                layer=layer, tm=tiles["tm_down"], tn=tn_k, out_dtype=F32, name="ffn_down")

    u3 = main3[:, :, u_col:u_col + d]
    xbc3 = main3[:, :, xbc_col:xbc_col + conv_dim]
    if states is None:
        pool_new, conv_new = u3[:, s - hp:], xbc3[:, s - hc:]
    else:
        pool_new = _history_tail(pool_prev[layer], u3, hp)
        conv_new = _history_tail(conv_prev[layer], xbc3, hc)
    return x2.reshape(b, s, d), pool_new, conv_new, ssm_new, ffn_new


def kernel(x_prompt, x_sample, state_pool, state_ssm_conv, state_ssm, state_ffn_conv, cache_mem_k, cache_mem_v,
           mem_prompt, norm_mix, w_in, w_pool_group, pool_scale, w_pool_out, ssm_conv_w, ssm_conv_b, ssm_dt_bias,
           ssm_a_log, ssm_d, ssm_norm, w_ssm_out, w_out, norm_mem_q, w_mem_q, w_mem_o, norm_mem_kv, w_mem_k,
           w_mem_v, norm_ffn, w_ffn_up, ffn_conv_w, ffn_conv_b, w_ffn_down, norm_final):
    depth = w_in.shape[0]
    bp, sp, d = x_prompt.shape
    bs, ss, _ = x_sample.shape
    n_heads = ssm_d.shape[1]
    inner = w_ssm_out.shape[1]
    head_dim = inner // n_heads
    n_state = state_ssm.shape[-1]
    conv_dim = ssm_conv_w.shape[2]
    n_groups = (conv_dim - inner) // (2 * n_state)
    mem_heads = cache_mem_k.shape[3]
    mem_len = mem_prompt.shape[1]
    dims = dict(groups=n_groups, heads_per_group=n_heads // n_groups, head_dim=head_dim, n_state=n_state,
                mem_heads=mem_heads)
    assert n_heads <= LANES

    c_u, c_z, c_x, c_dt, c_gp = d, d + inner, d + inner + conv_dim, d + inner + conv_dim + n_heads, 2 * d + inner + conv_dim + n_heads
    head_of_col = jnp.arange(inner, dtype=jnp.int32) // head_dim
    expand = (jnp.arange(LANES, dtype=jnp.int32)[:, None] == head_of_col[None, :]).astype(BF16)

    def pad_heads(v):
        return jnp.pad(v, (0, LANES - n_heads)).reshape(1, LANES)

    tiles_p = dict(tm=1024, tm_norm=1024, tm_down=1024, tn=1024, tn_long_k=512, pool_nb=1, pool_L=512, ssd_q=128,
                   ssd_nb=1, attn_nb=1, attn_tq=512, ffn_nb=1, ffn_L=512, ffn_tc=512, act_dtype=BF16)
    tiles_s = dict(tm=512, tm_norm=1024, tm_down=512, tn=1024, tn_long_k=512, pool_nb=32, pool_L=ss, ssd_q=ss,
                   ssd_nb=2, attn_nb=2, attn_tq=ss, ffn_nb=64, ffn_L=ss, ffn_tc=512, act_dtype=F32)

    ssm_states = state_ssm.reshape(depth, bs, n_groups, inner // n_groups, n_state)
    kv_s = (cache_mem_k, cache_mem_v)
    states_s = (state_pool, state_ssm_conv, ssm_states, state_ffn_conv)
    ssm_s_all = None

    w_in_bf16 = w_in.astype(BF16)
    wts = dict(
        w_in_uzx=w_in_bf16, w_in_gates=w_in_bf16[:, :, c_dt:],
        w_dt=jnp.pad(w_in[:, :, c_x:c_dt], ((0, 0), (0, 0), (0, LANES - n_heads))).astype(BF16),
        w_pool_group=w_pool_group.astype(BF16), w_pool_out=w_pool_out.astype(BF16),
        w_ssm_out=w_ssm_out.astype(BF16), w_out=w_out.astype(BF16),
        w_mem_q=w_mem_q.astype(BF16), w_mem_o=w_mem_o.astype(BF16),
        w_mem_k=w_mem_k.astype(BF16), w_mem_v=w_mem_v.astype(BF16),
        w_ffn_up=w_ffn_up.astype(BF16), w_ffn_down=w_ffn_down.astype(BF16),
    )

    yp, ys = x_prompt, x_sample
    outs = [[] for _ in range(9)]
    for i in range(depth):
        lw = dict(
            norm_mix=norm_mix[i], pool_scale=pool_scale[i],
            ssm_conv_w=ssm_conv_w[i], ssm_conv_b=ssm_conv_b[i],
            dt_bias=pad_heads(ssm_dt_bias[i]), a_log=pad_heads(ssm_a_log[i]),
            d_exp=jnp.repeat(ssm_d[i], head_dim).reshape(1, inner), ssm_norm=ssm_norm[i], expand=expand,
            norm_mem_q=norm_mem_q[i], norm_ffn=norm_ffn[i],
            ffn_conv_w=ffn_conv_w[i], ffn_conv_b=ffn_conv_b[i],
        )
        mem2 = mem_prompt.reshape(bp * mem_len, d)
        k_i = norm_matmul(mem2, norm_mem_kv[i], wts["w_mem_k"], layer=i, tm=512, tn=1024, name="mem_k")
        v_i = norm_matmul(mem2, norm_mem_kv[i], wts["w_mem_v"], layer=i, tm=512, tn=1024, name="mem_v")
        kv_p = (k_i.reshape(1, bp, mem_len, d), v_i.reshape(1, bp, mem_len, d))

        yp, a0, a1, a2, a3 = _layer(yp, wts, lw, dims, None, None, kv_p, layer=i, kv_layer=0, pos0=0, tiles=tiles_p)
        ys, b0, b1, ssm_s_all, b3 = _layer(ys, wts, lw, dims, states_s, ssm_s_all, kv_s, layer=i, kv_layer=i,
                                           pos0=PAST_LEN, tiles=tiles_s)
        for lst, val in zip(outs, (a0, b0, a1, b1, a2.reshape(bp, n_heads, head_dim, n_state), a3, b3,
                                   k_i.reshape(bp, mem_len, mem_heads, d // mem_heads),
                                   v_i.reshape(bp, mem_len, mem_heads, d // mem_heads))):
            lst.append(val)

    y_prompt = rmsnorm(yp.reshape(bp * sp, d), norm_final, tm=512).reshape(bp, sp, d)
    y_sample = rmsnorm(ys.reshape(bs * ss, d), norm_final, tm=512).reshape(bs, ss, d)
    pool_p, pool_s, conv_p, conv_s, ssm_p, ffn_p, ffn_s, mk_p, mv_p = (jnp.stack(lst) for lst in outs)
    ssm_s = ssm_s_all.reshape(depth, bs, n_heads, head_dim, n_state)
    return (y_prompt, y_sample, pool_p, pool_s, conv_p, conv_s, ssm_p, ssm_s, ffn_p, ffn_s, mk_p, mv_p)
```

```python
import functools
import math

import jax
import jax.numpy as jnp
from jax import lax
from jax.experimental import pallas as pl
from jax.experimental.pallas import tpu as pltpu

F32 = jnp.float32
BF16 = jnp.bfloat16
EPS = 1e-6

LANES = 128
SUBLANES = 8
VMEM_LIMIT = 56 * 1024 * 1024

PAST_LEN = 16384
POOL_WINDOWS = (2, 4, 8, 16)
POOL_HALO = 16
CONV_HALO = SUBLANES


def _cparams(*sem):
    return pltpu.CompilerParams(dimension_semantics=sem, vmem_limit_bytes=VMEM_LIMIT)


def _silu(x):
    h = 0.5 * x
    return h + h * jnp.tanh(h)


def _causal_conv(ext, w_ref, rows, axis):
    conv_k = w_ref.shape[0]
    tail = (slice(None),) * axis + (slice(CONV_HALO, CONV_HALO + rows),)
    acc = ext[tail] * w_ref[conv_k - 1:conv_k, :]
    for j in range(conv_k - 1):
        acc = acc + pltpu.roll(ext, conv_k - 1 - j, axis=axis)[tail] * w_ref[j:j + 1, :]
    return acc


def _norm_matmul_kernel(x_ref, g_ref, *rest, bounds, has_aux):
    n_parts = len(bounds) - 1
    w_refs, rest = rest[:n_parts], rest[n_parts:]
    if has_aux:
        wa_ref, o_ref, oa_ref, xn_ref = rest
    else:
        o_ref, xn_ref = rest
    j = pl.program_id(1)

    @pl.when(j == 0)
    def _():
        x = x_ref[...]
        ms = jnp.mean(x * x, axis=-1, keepdims=True)
        xn = (x * lax.rsqrt(ms + EPS) * g_ref[...]).astype(BF16)
        xn_ref[...] = xn
        if has_aux:
            oa_ref[...] = jnp.dot(xn, wa_ref[0], preferred_element_type=F32)

    if n_parts == 1:
        o_ref[...] = jnp.dot(xn_ref[...], w_refs[0][0], preferred_element_type=F32).astype(o_ref.dtype)
    else:
        for p in range(n_parts):
            @pl.when((j >= bounds[p]) & (j < bounds[p + 1]))
            def _(p=p):
                o_ref[...] = jnp.dot(xn_ref[...], w_refs[p][0], preferred_element_type=F32).astype(o_ref.dtype)


def norm_matmul(x, g, w, w_aux=None, *, layer, tm, tn, out_dtype=F32, out_block=None, name):
    t, k = x.shape
    parts, bounds = [], [0]
    for part in (w if isinstance(w, list) else [w]):
        arr, ncols = part if isinstance(part, tuple) else (part, part.shape[2])
        assert ncols % tn == 0 and ncols <= arr.shape[2]
        parts.append(arr)
        bounds.append(bounds[-1] + ncols // tn)
    n = bounds[-1] * tn
    assert t % tm == 0
    has_aux = w_aux is not None
    in_specs = [
        pl.BlockSpec((tm, k), lambda i, j: (i, 0)),
        pl.BlockSpec((1, k), lambda i, j: (0, 0)),
    ]
    for lo, hi in zip(bounds[:-1], bounds[1:]):
        in_specs.append(pl.BlockSpec((1, k, tn), lambda i, j, lo=lo, hi=hi: (layer, 0, jnp.clip(j - lo, 0, hi - lo - 1))))
    args = [x, g.reshape(1, k)] + parts
    out_block = out_block or (lambda j: j)
    out_shape = [jax.ShapeDtypeStruct((t, n), out_dtype)]
    out_specs = [pl.BlockSpec((tm, tn), lambda i, j: (i, out_block(j)))]
    if has_aux:
        na = w_aux.shape[2]
        in_specs.append(pl.BlockSpec((1, k, na), lambda i, j: (layer, 0, 0)))
        args.append(w_aux)
        out_shape.append(jax.ShapeDtypeStruct((t, na), F32))
        out_specs.append(pl.BlockSpec((tm, na), lambda i, j: (i, 0)))
    res = pl.pallas_call(
        functools.partial(_norm_matmul_kernel, bounds=tuple(bounds), has_aux=has_aux),
        grid=(t // tm, n // tn),
        in_specs=in_specs,
        out_specs=out_specs,
        out_shape=out_shape,
        scratch_shapes=[pltpu.VMEM((tm, k), BF16)],
        compiler_params=_cparams("parallel", "arbitrary"),
        name=name,
    )(*args)
    return res if has_aux else res[0]


def _matmul_kernel(a_ref, w_ref, *rest, epilogue):
    *extra, o_ref = rest
    acc = jnp.dot(a_ref[...].astype(BF16), w_ref[0], preferred_element_type=F32)
    o_ref[...] = epilogue(acc, *[e[...] for e in extra]).astype(o_ref.dtype)


def _ep_residual(acc, r):
    return r + acc


def _ep_gate_merge(acc, gated_pool, g_ssm):
    return gated_pool + jax.nn.sigmoid(g_ssm) * acc


def matmul(a, w, extras, epilogue, *, layer, tm, tn, out_dtype, name):
    t, k = a.shape
    n = w.shape[2]
    assert t % tm == 0 and n % tn == 0
    in_specs = [
        pl.BlockSpec((tm, k), lambda i, j: (i, 0)),
        pl.BlockSpec((1, k, tn), lambda i, j: (layer, 0, j)),
    ]
    args = [a, w]
    for arr, off in extras:
        assert off % tn == 0
        ob = off // tn
        in_specs.append(pl.BlockSpec((tm, tn), lambda i, j, ob=ob: (i, j + ob)))
        args.append(arr)
    return pl.pallas_call(
        functools.partial(_matmul_kernel, epilogue=epilogue),
        grid=(t // tm, n // tn),
        in_specs=in_specs,
        out_specs=pl.BlockSpec((tm, tn), lambda i, j: (i, j)),
        out_shape=jax.ShapeDtypeStruct((t, n), out_dtype),
        compiler_params=_cparams("parallel", "arbitrary"),
        name=name,
    )(*args)


def _rmsnorm_kernel(x_ref, g_ref, o_ref):
    x = x_ref[...]
    ms = jnp.mean(x * x, axis=-1, keepdims=True)
    o_ref[...] = x * lax.rsqrt(ms + EPS) * g_ref[...]


def rmsnorm(x, g, *, tm):
    t, k = x.shape
    return pl.pallas_call(
        _rmsnorm_kernel,
        grid=(t // tm,),
        in_specs=[pl.BlockSpec((tm, k), lambda i: (i, 0)), pl.BlockSpec((1, k), lambda i: (0, 0))],
        out_specs=pl.BlockSpec((tm, k), lambda i: (i, 0)),
        out_shape=jax.ShapeDtypeStruct((t, k), F32),
        compiler_params=_cparams("parallel"),
    )(x, g.reshape(1, k))


def _pool_kernel(cur_ref, prev_ref, gate_ref, wg_ref, scale_ref, wo_ref, o_ref, ext_ref, pooled_ref,
                 *, nb, L, from_state, pos0):
    w_ch = cur_ref.shape[-1]
    gdim = w_ch // len(POOL_WINDOWS)
    if from_state:
        ext_ref[:, 1:POOL_HALO, :] = prev_ref[0]
        pos_start = pos0
    else:
        i = pl.program_id(1)
        ext_ref[:, 0:POOL_HALO, :] = jnp.where(i == 0, 0.0, prev_ref[...])
        pos_start = pos0 + i * L
    ext_ref[:, POOL_HALO:POOL_HALO + L, :] = cur_ref[...]

    pos = pos_start + lax.broadcasted_iota(jnp.int32, (1, L, gdim), 1)
    for k, win in enumerate(POOL_WINDOWS):
        cs = slice(k * gdim, (k + 1) * gdim)
        cur = ext_ref[:, POOL_HALO:POOL_HALO + L, cs]
        acc = cur
        for j in range(1, win):
            acc = acc + ext_ref[:, POOL_HALO - j:POOL_HALO - j + L, cs]
        count = jnp.minimum(pos + 1, win).astype(F32)
        diff = (acc / count - cur).reshape(nb * L, gdim).astype(BF16)
        mixed = jnp.dot(diff, wg_ref[0, k], preferred_element_type=F32)
        pooled_ref[:, cs] = (mixed * scale_ref[:, cs]).astype(BF16)
    out_pool = jnp.dot(pooled_ref[...], wo_ref[0], preferred_element_type=F32)
    gate = jax.nn.sigmoid(gate_ref[...].reshape(nb * L, w_ch))
    o_ref[...] = (gate * out_pool).reshape(nb, L, w_ch)


def pool_branch(main3, prev_state, wg, scale, wo, *, layer, u_col, gate_col, nb, L, pos0):
    b, s, _ = main3.shape
    w_ch = wo.shape[1]
    ub, gb = u_col // w_ch, gate_col // w_ch
    from_state = prev_state is not None
    if from_state:
        assert L == s
        prev = prev_state
        prev_spec = pl.BlockSpec((1, nb, POOL_HALO - 1, w_ch), lambda n, i: (layer, n, 0, 0))
    else:
        assert nb == 1 and L % POOL_HALO == 0
        prev = main3
        r = L // POOL_HALO
        prev_spec = pl.BlockSpec((1, POOL_HALO, w_ch), lambda n, i: (n, jnp.maximum(i * r - 1, 0), ub))
    return pl.pallas_call(
        functools.partial(_pool_kernel, nb=nb, L=L, from_state=from_state, pos0=pos0),
        grid=(b // nb, s // L),
        in_specs=[
            pl.BlockSpec((nb, L, w_ch), lambda n, i: (n, i, ub)),
            prev_spec,
            pl.BlockSpec((nb, L, w_ch), lambda n, i: (n, i, gb)),
            pl.BlockSpec((1,) + wg.shape[1:], lambda n, i: (layer, 0, 0, 0)),
            pl.BlockSpec((1, w_ch), lambda n, i: (0, 0)),
            pl.BlockSpec((1,) + wo.shape[1:], lambda n, i: (layer, 0, 0)),
        ],
        out_specs=pl.BlockSpec((nb, L, w_ch), lambda n, i: (n, i, 0)),
        out_shape=jax.ShapeDtypeStruct((b, s, w_ch), F32),
        scratch_shapes=[pltpu.VMEM((nb, POOL_HALO + L, w_ch), F32), pltpu.VMEM((nb * L, w_ch), BF16)],
        compiler_params=_cparams("parallel", "arbitrary"),
        name="pool_state" if from_state else "pool_scan",
    )(main3, prev, main3, wg, scale.reshape(1, w_ch), wo)


def _cumsum_rows(x):
    q = x.shape[0]
    row = lax.broadcasted_iota(jnp.int32, x.shape, 0)
    k = 1
    while k < q:
        x = x + jnp.where(row >= k, pltpu.roll(x, k, axis=0), 0.0)
        k *= 2
    return x


def _expand_heads(vals, e_ref):
    q = vals[0].shape[0]
    pieces = []
    for v in vals:
        hi = v.astype(BF16).astype(F32)
        r1 = v - hi
        mid = r1.astype(BF16).astype(F32)
        pieces += [hi, mid, r1 - mid]
    lhs = jnp.concatenate(pieces, axis=0).astype(BF16)
    out = jnp.dot(lhs, e_ref[...], preferred_element_type=F32)
    return [out[(3 * i) * q:(3 * i + 1) * q] + out[(3 * i + 1) * q:(3 * i + 2) * q] + out[(3 * i + 2) * q:(3 * i + 3) * q]
            for i in range(len(vals))]


def _ssd_kernel(*refs, q, nb, n_groups, heads_per_group, head_dim, n_state, from_state, conv_k):
    if from_state:
        (xbc_ref, prev_ref, z_ref, dt_ref, h0_ref, cw_ref, cb_ref, dtb_ref, alog_ref, dexp_ref, norm_ref, e_ref,
         *_, y_ref, hout_ref, ext_ref, act_ref, yacc_ref) = refs
        params = (dtb_ref, alog_ref, dexp_ref, norm_ref, e_ref)
        hist = conv_k - 1
        for n in range(nb):
            ext = ext_ref.at[n]
            ext[0:CONV_HALO, :] = jnp.zeros((CONV_HALO, ext.shape[1]), F32)
            ext[CONV_HALO - hist:CONV_HALO, :] = prev_ref[0, n]
            ext[CONV_HALO:CONV_HALO + q, :] = xbc_ref[n]
            act_ref[n] = _silu(_causal_conv(ext[...], cw_ref, q, 0) + cb_ref[...])
        for n in range(nb):
            _ssd_chunk(act_ref.at[n], z_ref.at[n], dt_ref.at[n], h0_ref.at[0, n], hout_ref.at[0, n], y_ref.at[n],
                       yacc_ref.at[n], params, q=q, n_groups=n_groups, heads_per_group=heads_per_group,
                       head_dim=head_dim, n_state=n_state)
    else:
        (xbc_ref, prev_ref, z_ref, dt_ref, cw_ref, cb_ref, dtb_ref, alog_ref, dexp_ref, norm_ref, e_ref,
         y_ref, hout_ref, ext_ref, act_ref, h_ref, yacc_ref) = refs
        params = (dtb_ref, alog_ref, dexp_ref, norm_ref, e_ref)
        c = pl.program_id(1)

        @pl.when(c == 0)
        def _():
            h_ref[...] = jnp.zeros_like(h_ref)

        ext = ext_ref.at[0]
        ext[0:CONV_HALO, :] = jnp.where(c == 0, 0.0, prev_ref[0])
        ext[CONV_HALO:CONV_HALO + q, :] = xbc_ref[0]
        act_ref[0] = _silu(_causal_conv(ext[...], cw_ref, q, 0) + cb_ref[...])
        _ssd_chunk(act_ref.at[0], z_ref.at[0], dt_ref.at[0], h_ref, h_ref, y_ref.at[0], yacc_ref.at[0], params,
                   q=q, n_groups=n_groups, heads_per_group=heads_per_group, head_dim=head_dim, n_state=n_state)

        @pl.when(c == pl.num_programs(1) - 1)
        def _():
            hout_ref[0] = h_ref[...]


def _ssd_chunk(act_ref, z_ref, dt_ref, h_in, h_out, y_ref, yacc_ref, params,
               *, q, n_groups, heads_per_group, head_dim, n_state):
    dtb_ref, alog_ref, dexp_ref, norm_ref, e_ref = params
    gw = heads_per_group * head_dim
    inner = n_groups * gw

    dt = jax.nn.softplus(dt_ref[...] + dtb_ref[...])
    la_cs = _cumsum_rows(dt * (-jnp.exp(alog_ref[...])))
    la_cs_t = la_cs.T
    dt_x, la_x = _expand_heads([dt, la_cs], e_ref)
    from_start_x = jnp.exp(la_x)
    to_end_x = jnp.exp(la_x[q - 1:q, :] - la_x)
    chunk_decay = jnp.exp(la_cs[q - 1:q, :])

    xs = act_ref[:, 0:inner]
    xd = xs * dt_x
    xdw = xd * to_end_x
    yacc_ref[...] = dexp_ref[...] * xs

    tri = lax.broadcasted_iota(jnp.int32, (q, q), 0) >= lax.broadcasted_iota(jnp.int32, (q, q), 1)
    assert LANES % head_dim == 0
    hpl = LANES // head_dim
    lane_head = lax.broadcasted_iota(jnp.int32, (q, LANES), 1) // head_dim
    bms, cbs = [], []
    for g in range(n_groups):
        gs = slice(g * gw, (g + 1) * gw)
        bm = act_ref[:, inner + g * n_state:inner + (g + 1) * n_state].astype(BF16)
        cm = act_ref[:, inner + (n_groups + g) * n_state:inner + (n_groups + g + 1) * n_state].astype(BF16)
        bms.append(bm)
        cbs.append(lax.dot_general(cm, bm, (((1,), (1,)), ((), ())), preferred_element_type=F32))
        y_off = lax.dot_general(cm, h_in[g].astype(BF16), (((1,), (1,)), ((), ())),
                                preferred_element_type=F32)
        yacc_ref[:, gs] += y_off * from_start_x[:, gs]
    for g in range(n_groups):
        cb = cbs[g]
        for lt in range(gw // LANES):
            ls = slice(g * gw + lt * LANES, g * gw + (lt + 1) * LANES)
            xd_t = xd[:, ls]
            y_t = None
            for k in range(hpl):
                hd = (g * gw + lt * LANES) // head_dim + k
                seg = la_cs[:, hd:hd + 1] - la_cs_t[hd:hd + 1, :]
                m = (cb * jnp.exp(jnp.where(tri, seg, -jnp.inf))).astype(BF16)
                rhs = jnp.where(lane_head == k, xd_t, 0.0).astype(BF16)
                part = jnp.dot(m, rhs, preferred_element_type=F32)
                y_t = part if y_t is None else y_t + part
            yacc_ref[:, ls] += y_t
    for g in range(n_groups):
        gs = slice(g * gw, (g + 1) * gw)
        s_new = lax.dot_general(xdw[:, gs].astype(BF16), bms[g], (((0,), (0,)), ((), ())),
                                preferred_element_type=F32)
        for r in range(heads_per_group):
            hd = g * heads_per_group + r
            rows = slice(r * head_dim, (r + 1) * head_dim)
            h_out[g, rows, :] = h_in[g, rows, :] * chunk_decay[:, hd:hd + 1] + s_new[rows, :]

    y = yacc_ref[...] * _silu(z_ref[...])
    for g in range(n_groups):
        gs = slice(g * gw, (g + 1) * gw)
        yg = y[:, gs]
        ms = jnp.mean(yg * yg, axis=-1, keepdims=True)
        y_ref[:, gs] = (yg * lax.rsqrt(ms + EPS) * norm_ref[:, gs]).astype(y_ref.dtype)


def ssd_branch(main3, dt3, conv_state, ssm_state, state_out, cw, cb, dt_bias, a_log, d_exp, norm, expand,
               *, layer, xbc_col, z_col, q, nb, n_groups, heads_per_group, head_dim, n_state, out_dtype):
    b, s, _ = main3.shape
    gw = heads_per_group * head_dim
    inner = n_groups * gw
    conv_dim = inner + 2 * n_groups * n_state
    conv_k = cw.shape[0]
    from_state = ssm_state is not None
    xb, zb = xbc_col // conv_dim, z_col // inner
    assert xbc_col % conv_dim == 0 and z_col % inner == 0

    def const(shape):
        nd = len(shape)
        return pl.BlockSpec(shape, lambda n, c: (0,) * nd)

    assert b % nb == 0 and (from_state or nb == 1)
    in_specs = [pl.BlockSpec((nb, q, conv_dim), lambda n, c: (n, c, xb))]
    args = [main3]
    if from_state:
        assert q == s
        in_specs.append(pl.BlockSpec((1, nb, conv_k - 1, conv_dim), lambda n, c: (layer, n, 0, 0)))
        args.append(conv_state)
    else:
        assert q % CONV_HALO == 0
        r = q // CONV_HALO
        in_specs.append(pl.BlockSpec((1, CONV_HALO, conv_dim), lambda n, c: (n, jnp.maximum(c * r - 1, 0), xb)))
        args.append(main3)
    in_specs += [pl.BlockSpec((nb, q, inner), lambda n, c: (n, c, zb)),
                 pl.BlockSpec((nb, q, LANES), lambda n, c: (n, c, 0))]
    args += [main3, dt3]
    if from_state:
        in_specs.append(pl.BlockSpec((1, nb, n_groups, gw, n_state), lambda n, c: (layer, n, 0, 0, 0)))
        args.append(ssm_state)
    small = [cw, cb.reshape(1, conv_dim), dt_bias, a_log, d_exp, norm.reshape(1, inner), expand]
    in_specs += [const(a.shape) for a in small]
    args += small
    aliases = {}
    if from_state:
        state_spec = pl.BlockSpec((1, nb, n_groups, gw, n_state), lambda n, c: (layer, n, 0, 0, 0))
        state_shape = jax.ShapeDtypeStruct(ssm_state.shape, F32)
        if state_out is not None:
            aliases = {len(args): 1}
            in_specs.append(pl.BlockSpec(memory_space=pl.ANY))
            args.append(state_out)
    else:
        state_spec = pl.BlockSpec((1, n_groups, gw, n_state), lambda n, c: (n, 0, 0, 0))
        state_shape = jax.ShapeDtypeStruct((b, n_groups, gw, n_state), F32)
    return pl.pallas_call(
        functools.partial(_ssd_kernel, q=q, nb=nb, n_groups=n_groups, heads_per_group=heads_per_group,
                          head_dim=head_dim, n_state=n_state, from_state=from_state, conv_k=conv_k),
        grid=(b // nb, s // q),
        in_specs=in_specs,
        out_specs=[pl.BlockSpec((nb, q, inner), lambda n, c: (n, c, 0)), state_spec],
        out_shape=[jax.ShapeDtypeStruct((b, s, inner), out_dtype), state_shape],
        input_output_aliases=aliases,
        scratch_shapes=[pltpu.VMEM((nb, CONV_HALO + q, conv_dim), F32),
                        pltpu.VMEM((nb, q, conv_dim), F32)]
                       + ([] if from_state else [pltpu.VMEM((n_groups, gw, n_state), F32)])
                       + [pltpu.VMEM((nb, q, inner), F32)],
        compiler_params=_cparams("parallel", "arbitrary"),
        name="ssd_state" if from_state else "ssd_scan",
    )(*args)


def _kv_head(ref, n, h, d, stage_ref, slot):
    if len(ref.shape) == 5:
        stage_ref[slot] = ref[0, n, :, h, :]
        return stage_ref[slot].astype(BF16)
    return ref[0, n, :, h * d:(h + 1) * d].astype(BF16)


def _attn_kernel(q_ref, k_ref, v_ref, o_ref, *stage, n_heads, nb):
    stage_ref = stage[0] if stage else None
    d = q_ref.shape[-1] // n_heads
    inv = 1.0 / math.sqrt(d)
    units = [(n, h, slice(h * d, (h + 1) * d)) for n in range(nb) for h in range(n_heads)]
    scores = []
    for u, (n, h, hs) in enumerate(units):
        qh = q_ref[n, :, hs].astype(BF16)
        kh = _kv_head(k_ref, n, h, d, stage_ref, 2 * u)
        scores.append(lax.dot_general(qh, kh, (((1,), (1,)), ((), ())), preferred_element_type=F32) * inv)
    probs = []
    for s in scores:
        e = jnp.exp(s - jnp.max(s, axis=-1, keepdims=True))
        probs.append((e / jnp.sum(e, axis=-1, keepdims=True)).astype(BF16))
    for u, ((n, h, hs), p) in enumerate(zip(units, probs)):
        vh = _kv_head(v_ref, n, h, d, stage_ref, 2 * u + 1)
        o_ref[n, :, hs] = jnp.dot(p, vh, preferred_element_type=F32).astype(o_ref.dtype)


def _attn_few_kernel(q_ref, k_ref, v_ref, o_ref, *stage, n_heads, nb):
    stage_ref = stage[0] if stage else None
    tq = q_ref.shape[1]
    d = q_ref.shape[-1] // n_heads
    inv = 1.0 / math.sqrt(d)
    pad = jnp.zeros((LANES - tq, d), F32)
    units = [(n, h, slice(h * d, (h + 1) * d)) for n in range(nb) for h in range(n_heads)]
    scores_t = []
    for u, (n, h, hs) in enumerate(units):
        qh = jnp.concatenate([q_ref[n, :, hs].astype(F32), pad], axis=0).astype(BF16)
        kh = _kv_head(k_ref, n, h, d, stage_ref, 2 * u)
        scores_t.append(lax.dot_general(kh, qh, (((1,), (1,)), ((), ())), preferred_element_type=F32) * inv)
    probs = []
    for st in scores_t:
        e = jnp.exp(st - jnp.max(st, axis=0, keepdims=True))
        pt = e / jnp.sum(e, axis=0, keepdims=True)
        probs.append(pt.T[0:tq, :].astype(BF16))
    for u, ((n, h, hs), p) in enumerate(zip(units, probs)):
        vh = _kv_head(v_ref, n, h, d, stage_ref, 2 * u + 1)
        o_ref[n, :, hs] = jnp.dot(p, vh, preferred_element_type=F32).astype(o_ref.dtype)


def cross_attention(q3, k4, v4, *, layer, nb, tq, n_heads, out_dtype):
    b, s, dm = q3.shape
    m = k4.shape[2]
    body = _attn_few_kernel if tq < LANES else _attn_kernel
    zeros = (0,) * (k4.ndim - 2)
    kv_spec = pl.BlockSpec((1, nb) + k4.shape[2:], lambda n, i: (layer, n) + zeros)
    stage = [pltpu.VMEM((2 * nb * n_heads, m, dm // n_heads), F32)] if k4.ndim == 5 else []
    return pl.pallas_call(
        functools.partial(body, n_heads=n_heads, nb=nb),
        grid=(b // nb, s // tq),
        in_specs=[pl.BlockSpec((nb, tq, dm), lambda n, i: (n, i, 0)), kv_spec, kv_spec],
        out_specs=pl.BlockSpec((nb, tq, dm), lambda n, i: (n, i, 0)),
        out_shape=jax.ShapeDtypeStruct((b, s, dm), out_dtype),
        scratch_shapes=stage,
        compiler_params=_cparams("parallel", "arbitrary"),
        name="cross_attention",
    )(q3, k4, v4)


def _ffn_mid_kernel(g_ref, v_ref, gp_ref, vp_ref, gw_ref, vw_ref, gb_ref, vb_ref, o_ref, gext_ref, vext_ref,
                    *, L, from_state, conv_k):
    hist = conv_k - 1
    i = pl.program_id(1)

    def conv(cur_ref, prev_ref, w_ref, b_ref, ext_ref):
        if from_state:
            ext_ref[:, 0:CONV_HALO, :] = jnp.zeros((ext_ref.shape[0], CONV_HALO, ext_ref.shape[2]), F32)
            ext_ref[:, CONV_HALO - hist:CONV_HALO, :] = prev_ref[0]
        else:
            ext_ref[:, 0:CONV_HALO, :] = jnp.where(i == 0, 0.0, prev_ref[...])
        ext_ref[:, CONV_HALO:CONV_HALO + L, :] = cur_ref[...]
        return _causal_conv(ext_ref[...], w_ref, L, 1) + b_ref[...]

    g = conv(g_ref, gp_ref, gw_ref, gb_ref, gext_ref)
    v = conv(v_ref, vp_ref, vw_ref, vb_ref, vext_ref)
    o_ref[...] = (_silu(g) * v).astype(o_ref.dtype)


def ffn_mid(up3, prev_state, cw, cb, *, layer, nb, L, tc, out_dtype):
    b, s, two_f = up3.shape
    f = two_f // 2
    conv_k = cw.shape[0]
    assert f % tc == 0
    nj = f // tc
    from_state = prev_state is not None
    if from_state:
        assert L == s
        prev = prev_state
        gp_spec = pl.BlockSpec((1, nb, conv_k - 1, tc), lambda n, i, j: (layer, n, 0, j))
        vp_spec = pl.BlockSpec((1, nb, conv_k - 1, tc), lambda n, i, j: (layer, n, 0, j + nj))
    else:
        assert L % CONV_HALO == 0
        prev = up3
        r = L // CONV_HALO
        gp_spec = pl.BlockSpec((nb, CONV_HALO, tc), lambda n, i, j: (n, jnp.maximum(i * r - 1, 0), j))
        vp_spec = pl.BlockSpec((nb, CONV_HALO, tc), lambda n, i, j: (n, jnp.maximum(i * r - 1, 0), j + nj))
    cb2 = cb.reshape(1, two_f)
    return pl.pallas_call(
        functools.partial(_ffn_mid_kernel, L=L, from_state=from_state, conv_k=conv_k),
        grid=(b // nb, s // L, nj),
        in_specs=[
            pl.BlockSpec((nb, L, tc), lambda n, i, j: (n, i, j)),
            pl.BlockSpec((nb, L, tc), lambda n, i, j: (n, i, j + nj)),
            gp_spec, vp_spec,
            pl.BlockSpec((conv_k, tc), lambda n, i, j: (0, j)),
            pl.BlockSpec((conv_k, tc), lambda n, i, j: (0, j + nj)),
            pl.BlockSpec((1, tc), lambda n, i, j: (0, j)),
            pl.BlockSpec((1, tc), lambda n, i, j: (0, j + nj)),
        ],
        out_specs=pl.BlockSpec((nb, L, tc), lambda n, i, j: (n, i, j)),
        out_shape=jax.ShapeDtypeStruct((b, s, f), out_dtype),
        scratch_shapes=[pltpu.VMEM((nb, CONV_HALO + L, tc), F32), pltpu.VMEM((nb, CONV_HALO + L, tc), F32)],
        compiler_params=_cparams("parallel", "arbitrary", "arbitrary"),
        name="ffn_mid_state" if from_state else "ffn_mid_scan",
    )(up3, up3, prev, prev, cw, cw, cb2, cb2)


def _ffn_front_kernel(x_ref, g_ref, wg_ref, wv_ref, cwg_ref, cwv_ref, cbg_ref, cbv_ref,
                      act_ref, tail_g_ref, tail_v_ref, xn_ref, halo_ref, *, tiles_per_seq):
    i, j = pl.program_id(0), pl.program_id(1)
    tm = x_ref.shape[0]

    @pl.when(j == 0)
    def _():
        x = x_ref[...]
        ms = jnp.mean(x * x, axis=-1, keepdims=True)
        xn_ref[...] = (x * lax.rsqrt(ms + EPS) * g_ref[...]).astype(BF16)

    xn = xn_ref[...]
    seq_start = (i % tiles_per_seq) == 0

    def half(w_ref, cw_ref, cb_ref, tail_ref, slot):
        up = jnp.dot(xn, w_ref[0], preferred_element_type=F32)
        prev = jnp.where(seq_start, 0.0, halo_ref[slot, j])
        tail = up[tm - CONV_HALO:tm, :]
        halo_ref[slot, j] = tail
        tail_ref[0] = tail
        ext = jnp.concatenate([prev, up], axis=0)
        return _causal_conv(ext, cw_ref, tm, 0) + cb_ref[...]

    g = half(wg_ref, cwg_ref, cbg_ref, tail_g_ref, 0)
    v = half(wv_ref, cwv_ref, cbv_ref, tail_v_ref, 1)
    act_ref[...] = (_silu(g) * v).astype(act_ref.dtype)


def ffn_front(x2, gain, w_up, cw, cb, *, layer, seq_len, tm, tn, out_dtype):
    t, k = x2.shape
    f = w_up.shape[2] // 2
    assert t % tm == 0 and f % tn == 0 and seq_len % tm == 0
    nj = f // tn
    conv_k = cw.shape[0]
    cb2 = cb.reshape(1, 2 * f)
    return pl.pallas_call(
        functools.partial(_ffn_front_kernel, tiles_per_seq=seq_len // tm),
        grid=(t // tm, nj),
        in_specs=[
            pl.BlockSpec((tm, k), lambda i, j: (i, 0)),
            pl.BlockSpec((1, k), lambda i, j: (0, 0)),
            pl.BlockSpec((1, k, tn), lambda i, j: (layer, 0, j)),
            pl.BlockSpec((1, k, tn), lambda i, j: (layer, 0, j + nj)),
            pl.BlockSpec((conv_k, tn), lambda i, j: (0, j)),
            pl.BlockSpec((conv_k, tn), lambda i, j: (0, j + nj)),
            pl.BlockSpec((1, tn), lambda i, j: (0, j)),
            pl.BlockSpec((1, tn), lambda i, j: (0, j + nj)),
        ],
        out_specs=[pl.BlockSpec((tm, tn), lambda i, j: (i, j)),
                   pl.BlockSpec((1, CONV_HALO, tn), lambda i, j: (i, 0, j)),
                   pl.BlockSpec((1, CONV_HALO, tn), lambda i, j: (i, 0, j))],
        out_shape=[jax.ShapeDtypeStruct((t, f), out_dtype),
                   jax.ShapeDtypeStruct((t // tm, CONV_HALO, f), F32),
                   jax.ShapeDtypeStruct((t // tm, CONV_HALO, f), F32)],
        scratch_shapes=[pltpu.VMEM((tm, k), BF16), pltpu.VMEM((2, nj, CONV_HALO, tn), F32)],
        compiler_params=_cparams("arbitrary", "arbitrary"),
        name="ffn_front",
    )(x2, gain.reshape(1, k), w_up, w_up, cw, cw, cb2, cb2)


def _history_tail(prev, cur, hist):
    s = cur.shape[1]
    if s >= hist:
        return cur[:, s - hist:]
    return jnp.concatenate([prev[:, s:], cur], axis=1)


def _layer(x3, wts, lw, dims, states, ssm_out, mem_kv, *, layer, kv_layer, pos0, tiles):
    b, s, d = x3.shape
    t = b * s
    x2 = x3.reshape(t, d)
    G, R, P, N = dims["groups"], dims["heads_per_group"], dims["head_dim"], dims["n_state"]
    inner = G * R * P
    conv_dim = inner + 2 * G * N
    z_col, u_col, xbc_col, gp_col, gs_col = 0, inner, inner + d, inner + d + conv_dim, inner + 2 * d + conv_dim
    n_main = gs_col + d
    tm, tn, tn_k = tiles["tm"], tiles["tn"], tiles["tn_long_k"]
    act_dtype = tiles["act_dtype"]
    hp, hc, hf = POOL_HALO - 1, lw["ssm_conv_w"].shape[0] - 1, lw["ffn_conv_w"].shape[0] - 1

    if states is None:
        pool_prev = conv_prev = ssm_prev = ffn_prev = None
    else:
        pool_prev, conv_prev, ssm_prev, ffn_prev = states

    nu, nz = d // tn, inner // tn
    assert d % tn == 0 and inner % tn == 0
    main, dt = norm_matmul(x2, lw["norm_mix"], [(wts["w_in_uzx"], gp_col), wts["w_in_gates"]], wts["w_dt"], layer=layer,
                           tm=tiles["tm_norm"], tn=tn, name="in_proj",
                           out_block=lambda j: jnp.where(j < nu, j + nz, jnp.where(j < nu + nz, j - nu, j)))
    main3 = main.reshape(b, s, n_main)
    dt3 = dt.reshape(b, s, LANES)

    gated_pool = pool_branch(main3, pool_prev, wts["w_pool_group"], lw["pool_scale"], wts["w_pool_out"], layer=layer,
                             u_col=u_col, gate_col=gp_col, nb=tiles["pool_nb"], L=tiles["pool_L"], pos0=pos0)
    y, ssm_new = ssd_branch(main3, dt3, conv_prev, ssm_prev, ssm_out, lw["ssm_conv_w"], lw["ssm_conv_b"],
                            lw["dt_bias"], lw["a_log"], lw["d_exp"], lw["ssm_norm"], lw["expand"], layer=layer,
                            xbc_col=xbc_col, z_col=z_col, q=tiles["ssd_q"], nb=tiles["ssd_nb"], n_groups=G,
                            heads_per_group=R, head_dim=P, n_state=N, out_dtype=act_dtype)
    merged = matmul(y.reshape(t, inner), wts["w_ssm_out"],
                    [(gated_pool.reshape(t, d), 0), (main, gs_col)], _ep_gate_merge,
                    layer=layer, tm=tm, tn=tn_k, out_dtype=BF16, name="ssm_out_merge")
    x2 = matmul(merged, wts["w_out"], [(x2, 0)], _ep_residual, layer=layer, tm=tm, tn=tn, out_dtype=F32,
                name="mix_out")

    qm = norm_matmul(x2, lw["norm_mem_q"], wts["w_mem_q"], layer=layer, tm=tiles["tm_norm"], tn=tn, out_dtype=act_dtype,
                     name="mem_q")
    k4, v4 = mem_kv
    o = cross_attention(qm.reshape(b, s, d), k4, v4, layer=kv_layer, nb=tiles["attn_nb"], tq=tiles["attn_tq"],
                        n_heads=dims["mem_heads"], out_dtype=act_dtype)
    x2 = matmul(o.reshape(t, d), wts["w_mem_o"], [(x2, 0)], _ep_residual, layer=layer, tm=tm, tn=tn, out_dtype=F32,
                name="mem_o")

    if states is None:
        act2, tail_g, tail_v = ffn_front(x2, lw["norm_ffn"], wts["w_ffn_up"], lw["ffn_conv_w"], lw["ffn_conv_b"],
                                         layer=layer, seq_len=s, tm=tm, tn=tiles["ffn_tc"], out_dtype=act_dtype)
        per_seq = s // tm
        ffn_new = jnp.concatenate([tail_g[per_seq - 1::per_seq, CONV_HALO - hf:],
                                   tail_v[per_seq - 1::per_seq, CONV_HALO - hf:]], axis=-1)
    else:
        up = norm_matmul(x2, lw["norm_ffn"], wts["w_ffn_up"], layer=layer, tm=tiles["tm_norm"], tn=tn, name="ffn_up")
        up3 = up.reshape(b, s, up.shape[1])
        act2 = ffn_mid(up3, ffn_prev, lw["ffn_conv_w"], lw["ffn_conv_b"], layer=layer, nb=tiles["ffn_nb"],
                       L=tiles["ffn_L"], tc=tiles["ffn_tc"], out_dtype=act_dtype).reshape(t, -1)
        ffn_new = _history_tail(ffn_prev[layer], up3, hf)
    x2 = matmul(act2, wts["w_ffn_down"], [(x2, 0)], _ep_residual,
                layer=layer, tm=tiles["tm_down"], tn=tn_k, out_dtype=F32, name="ffn_down")

    u3 = main3[:, :, u_col:u_col + d]
    xbc3 = main3[:, :, xbc_col:xbc_col + conv_dim]
    if states is None:
        pool_new, conv_new = u3[:, s - hp:], xbc3[:, s - hc:]
    else:
        pool_new = _history_tail(pool_prev[layer], u3, hp)
        conv_new = _history_tail(conv_prev[layer], xbc3, hc)
    return x2.reshape(b, s, d), pool_new, conv_new, ssm_new, ffn_new


def kernel(x_prompt, x_sample, state_pool, state_ssm_conv, state_ssm, state_ffn_conv, cache_mem_k, cache_mem_v,
           mem_prompt, norm_mix, w_in, w_pool_group, pool_scale, w_pool_out, ssm_conv_w, ssm_conv_b, ssm_dt_bias,
           ssm_a_log, ssm_d, ssm_norm, w_ssm_out, w_out, norm_mem_q, w_mem_q, w_mem_o, norm_mem_kv, w_mem_k,
           w_mem_v, norm_ffn, w_ffn_up, ffn_conv_w, ffn_conv_b, w_ffn_down, norm_final):
    depth = w_in.shape[0]
    bp, sp, d = x_prompt.shape
    bs, ss, _ = x_sample.shape
    n_heads = ssm_d.shape[1]
    inner = w_ssm_out.shape[1]
    head_dim = inner // n_heads
    n_state = state_ssm.shape[-1]
    conv_dim = ssm_conv_w.shape[2]
    n_groups = (conv_dim - inner) // (2 * n_state)
    mem_heads = cache_mem_k.shape[3]
    mem_len = mem_prompt.shape[1]
    dims = dict(groups=n_groups, heads_per_group=n_heads // n_groups, head_dim=head_dim, n_state=n_state,
                mem_heads=mem_heads)
    assert n_heads <= LANES

    c_u, c_z, c_x, c_dt, c_gp = d, d + inner, d + inner + conv_dim, d + inner + conv_dim + n_heads, 2 * d + inner + conv_dim + n_heads
    head_of_col = jnp.arange(inner, dtype=jnp.int32) // head_dim
    expand = (jnp.arange(LANES, dtype=jnp.int32)[:, None] == head_of_col[None, :]).astype(BF16)

    def pad_heads(v):
        return jnp.pad(v, (0, LANES - n_heads)).reshape(1, LANES)

    tiles_p = dict(tm=1024, tm_norm=1024, tm_down=1024, tn=1024, tn_long_k=512, pool_nb=1, pool_L=512, ssd_q=128,
                   ssd_nb=1, attn_nb=1, attn_tq=512, ffn_nb=1, ffn_L=512, ffn_tc=512, act_dtype=BF16)
    tiles_s = dict(tm=512, tm_norm=1024, tm_down=512, tn=1024, tn_long_k=512, pool_nb=32, pool_L=ss, ssd_q=ss,
                   ssd_nb=4, attn_nb=2, attn_tq=ss, ffn_nb=64, ffn_L=ss, ffn_tc=512, act_dtype=F32)

    ssm_states = state_ssm.reshape(depth, bs, n_groups, inner // n_groups, n_state)
    kv_s = (cache_mem_k, cache_mem_v)
    states_s = (state_pool, state_ssm_conv, ssm_states, state_ffn_conv)
    ssm_s_all = None

    w_in_bf16 = w_in.astype(BF16)
    wts = dict(
        w_in_uzx=w_in_bf16, w_in_gates=w_in_bf16[:, :, c_dt:],
        w_dt=jnp.pad(w_in[:, :, c_x:c_dt], ((0, 0), (0, 0), (0, LANES - n_heads))).astype(BF16),
        w_pool_group=w_pool_group.astype(BF16), w_pool_out=w_pool_out.astype(BF16),
        w_ssm_out=w_ssm_out.astype(BF16), w_out=w_out.astype(BF16),
        w_mem_q=w_mem_q.astype(BF16), w_mem_o=w_mem_o.astype(BF16),
        w_mem_k=w_mem_k.astype(BF16), w_mem_v=w_mem_v.astype(BF16),
        w_ffn_up=w_ffn_up.astype(BF16), w_ffn_down=w_ffn_down.astype(BF16),
    )

    yp, ys = x_prompt, x_sample
    outs = [[] for _ in range(9)]
    for i in range(depth):
        lw = dict(
            norm_mix=norm_mix[i], pool_scale=pool_scale[i],
            ssm_conv_w=ssm_conv_w[i], ssm_conv_b=ssm_conv_b[i],
            dt_bias=pad_heads(ssm_dt_bias[i]), a_log=pad_heads(ssm_a_log[i]),
            d_exp=jnp.repeat(ssm_d[i], head_dim).reshape(1, inner), ssm_norm=ssm_norm[i], expand=expand,
            norm_mem_q=norm_mem_q[i], norm_ffn=norm_ffn[i],
            ffn_conv_w=ffn_conv_w[i], ffn_conv_b=ffn_conv_b[i],
        )
        mem2 = mem_prompt.reshape(bp * mem_len, d)
        k_i = norm_matmul(mem2, norm_mem_kv[i], wts["w_mem_k"], layer=i, tm=512, tn=1024, name="mem_k")
        v_i = norm_matmul(mem2, norm_mem_kv[i], wts["w_mem_v"], layer=i, tm=512, tn=1024, name="mem_v")
        kv_p = (k_i.reshape(1, bp, mem_len, d), v_i.reshape(1, bp, mem_len, d))

        yp, a0, a1, a2, a3 = _layer(yp, wts, lw, dims, None, None, kv_p, layer=i, kv_layer=0, pos0=0, tiles=tiles_p)
        ys, b0, b1, ssm_s_all, b3 = _layer(ys, wts, lw, dims, states_s, ssm_s_all, kv_s, layer=i, kv_layer=i,
                                           pos0=PAST_LEN, tiles=tiles_s)
        for lst, val in zip(outs, (a0, b0, a1, b1, a2.reshape(bp, n_heads, head_dim, n_state), a3, b3,
                                   k_i.reshape(bp, mem_len, mem_heads, d // mem_heads),
                                   v_i.reshape(bp, mem_len, mem_heads, d // mem_heads))):
            lst.append(val)

    y_prompt = rmsnorm(yp.reshape(bp * sp, d), norm_final, tm=512).reshape(bp, sp, d)
    y_sample = rmsnorm(ys.reshape(bs * ss, d), norm_final, tm=512).reshape(bs, ss, d)
    pool_p, pool_s, conv_p, conv_s, ssm_p, ffn_p, ffn_s, mk_p, mv_p = (jnp.stack(lst) for lst in outs)
    ssm_s = ssm_s_all.reshape(depth, bs, n_heads, head_dim, n_state)
    return (y_prompt, y_sample, pool_p, pool_s, conv_p, conv_s, ssm_p, ssm_s, ffn_p, ffn_s, mk_p, mv_p)
```

```python
import functools
import math

import jax
import jax.numpy as jnp
from jax import lax
from jax.experimental import pallas as pl
from jax.experimental.pallas import tpu as pltpu

F32 = jnp.float32
BF16 = jnp.bfloat16
EPS = 1e-6

LANES = 128
SUBLANES = 8
VMEM_LIMIT = 56 * 1024 * 1024

PAST_LEN = 16384
POOL_WINDOWS = (2, 4, 8, 16)
POOL_HALO = 16
CONV_HALO = SUBLANES


def _cparams(*sem):
    return pltpu.CompilerParams(dimension_semantics=sem, vmem_limit_bytes=VMEM_LIMIT)


def _silu(x):
    h = 0.5 * x
    return h + h * jnp.tanh(h)


def _causal_conv(ext, w_ref, rows, axis):
    conv_k = w_ref.shape[0]
    tail = (slice(None),) * axis + (slice(CONV_HALO, CONV_HALO + rows),)
    acc = ext[tail] * w_ref[conv_k - 1:conv_k, :]
    for j in range(conv_k - 1):
        acc = acc + pltpu.roll(ext, conv_k - 1 - j, axis=axis)[tail] * w_ref[j:j + 1, :]
    return acc


def _norm_matmul_kernel(x_ref, g_ref, *rest, bounds, has_aux):
    n_parts = len(bounds) - 1
    w_refs, rest = rest[:n_parts], rest[n_parts:]
    if has_aux:
        wa_ref, o_ref, oa_ref, xn_ref = rest
    else:
        o_ref, xn_ref = rest
    j = pl.program_id(1)

    @pl.when(j == 0)
    def _():
        x = x_ref[...]
        ms = jnp.mean(x * x, axis=-1, keepdims=True)
        xn = (x * lax.rsqrt(ms + EPS) * g_ref[...]).astype(BF16)
        xn_ref[...] = xn
        if has_aux:
            oa_ref[...] = jnp.dot(xn, wa_ref[0], preferred_element_type=F32)

    if n_parts == 1:
        o_ref[...] = jnp.dot(xn_ref[...], w_refs[0][0], preferred_element_type=F32).astype(o_ref.dtype)
    else:
        for p in range(n_parts):
            @pl.when((j >= bounds[p]) & (j < bounds[p + 1]))
            def _(p=p):
                o_ref[...] = jnp.dot(xn_ref[...], w_refs[p][0], preferred_element_type=F32).astype(o_ref.dtype)


def norm_matmul(x, g, w, w_aux=None, *, layer, tm, tn, out_dtype=F32, out_block=None, name):
    t, k = x.shape
    parts, bounds = [], [0]
    for part in (w if isinstance(w, list) else [w]):
        arr, ncols = part if isinstance(part, tuple) else (part, part.shape[2])
        assert ncols % tn == 0 and ncols <= arr.shape[2]
        parts.append(arr)
        bounds.append(bounds[-1] + ncols // tn)
    n = bounds[-1] * tn
    assert t % tm == 0
    has_aux = w_aux is not None
    in_specs = [
        pl.BlockSpec((tm, k), lambda i, j: (i, 0)),
        pl.BlockSpec((1, k), lambda i, j: (0, 0)),
    ]
    for lo, hi in zip(bounds[:-1], bounds[1:]):
        in_specs.append(pl.BlockSpec((1, k, tn), lambda i, j, lo=lo, hi=hi: (layer, 0, jnp.clip(j - lo, 0, hi - lo - 1))))
    args = [x, g.reshape(1, k)] + parts
    out_block = out_block or (lambda j: j)
    out_shape = [jax.ShapeDtypeStruct((t, n), out_dtype)]
    out_specs = [pl.BlockSpec((tm, tn), lambda i, j: (i, out_block(j)))]
    if has_aux:
        na = w_aux.shape[2]
        in_specs.append(pl.BlockSpec((1, k, na), lambda i, j: (layer, 0, 0)))
        args.append(w_aux)
        out_shape.append(jax.ShapeDtypeStruct((t, na), F32))
        out_specs.append(pl.BlockSpec((tm, na), lambda i, j: (i, 0)))
    res = pl.pallas_call(
        functools.partial(_norm_matmul_kernel, bounds=tuple(bounds), has_aux=has_aux),
        grid=(t // tm, n // tn),
        in_specs=in_specs,
        out_specs=out_specs,
        out_shape=out_shape,
        scratch_shapes=[pltpu.VMEM((tm, k), BF16)],
        compiler_params=_cparams("parallel", "arbitrary"),
        name=name,
    )(*args)
    return res if has_aux else res[0]


def _matmul_kernel(a_ref, w_ref, *rest, epilogue):
    *extra, o_ref = rest
    acc = jnp.dot(a_ref[...].astype(BF16), w_ref[0], preferred_element_type=F32)
    o_ref[...] = epilogue(acc, *[e[...] for e in extra]).astype(o_ref.dtype)


def _ep_residual(acc, r):
    return r + acc


def _ep_gate_merge(acc, gated_pool, g_ssm):
    return gated_pool + jax.nn.sigmoid(g_ssm) * acc


def matmul(a, w, extras, epilogue, *, layer, tm, tn, out_dtype, name):
    t, k = a.shape
    n = w.shape[2]
    assert t % tm == 0 and n % tn == 0
    in_specs = [
        pl.BlockSpec((tm, k), lambda i, j: (i, 0)),
        pl.BlockSpec((1, k, tn), lambda i, j: (layer, 0, j)),
    ]
    args = [a, w]
    for arr, off in extras:
        assert off % tn == 0
        ob = off // tn
        in_specs.append(pl.BlockSpec((tm, tn), lambda i, j, ob=ob: (i, j + ob)))
        args.append(arr)
    return pl.pallas_call(
        functools.partial(_matmul_kernel, epilogue=epilogue),
        grid=(t // tm, n // tn),
        in_specs=in_specs,
        out_specs=pl.BlockSpec((tm, tn), lambda i, j: (i, j)),
        out_shape=jax.ShapeDtypeStruct((t, n), out_dtype),
        compiler_params=_cparams("parallel", "arbitrary"),
        name=name,
    )(*args)


def _rmsnorm_kernel(x_ref, g_ref, o_ref):
    x = x_ref[...]
    ms = jnp.mean(x * x, axis=-1, keepdims=True)
    o_ref[...] = x * lax.rsqrt(ms + EPS) * g_ref[...]


def rmsnorm(x, g, *, tm):
    t, k = x.shape
    return pl.pallas_call(
        _rmsnorm_kernel,
        grid=(t // tm,),
        in_specs=[pl.BlockSpec((tm, k), lambda i: (i, 0)), pl.BlockSpec((1, k), lambda i: (0, 0))],
        out_specs=pl.BlockSpec((tm, k), lambda i: (i, 0)),
        out_shape=jax.ShapeDtypeStruct((t, k), F32),
        compiler_params=_cparams("parallel"),
    )(x, g.reshape(1, k))


def _pool_kernel(cur_ref, prev_ref, gate_ref, wg_ref, scale_ref, wo_ref, o_ref, ext_ref, pooled_ref,
                 *, nb, L, from_state, pos0):
    w_ch = cur_ref.shape[-1]
    gdim = w_ch // len(POOL_WINDOWS)
    if from_state:
        ext_ref[:, 1:POOL_HALO, :] = prev_ref[0]
        pos_start = pos0
    else:
        i = pl.program_id(1)
        ext_ref[:, 0:POOL_HALO, :] = jnp.where(i == 0, 0.0, prev_ref[...])
        pos_start = pos0 + i * L
    ext_ref[:, POOL_HALO:POOL_HALO + L, :] = cur_ref[...]

    pos = pos_start + lax.broadcasted_iota(jnp.int32, (1, L, gdim), 1)
    for k, win in enumerate(POOL_WINDOWS):
        cs = slice(k * gdim, (k + 1) * gdim)
        cur = ext_ref[:, POOL_HALO:POOL_HALO + L, cs]
        acc = cur
        for j in range(1, win):
            acc = acc + ext_ref[:, POOL_HALO - j:POOL_HALO - j + L, cs]
        count = jnp.minimum(pos + 1, win).astype(F32)
        diff = (acc / count - cur).reshape(nb * L, gdim).astype(BF16)
        mixed = jnp.dot(diff, wg_ref[0, k], preferred_element_type=F32)
        pooled_ref[:, cs] = (mixed * scale_ref[:, cs]).astype(BF16)
    out_pool = jnp.dot(pooled_ref[...], wo_ref[0], preferred_element_type=F32)
    gate = jax.nn.sigmoid(gate_ref[...].reshape(nb * L, w_ch))
    o_ref[...] = (gate * out_pool).reshape(nb, L, w_ch)


def pool_branch(main3, prev_state, wg, scale, wo, *, layer, u_col, gate_col, nb, L, pos0):
    b, s, _ = main3.shape
    w_ch = wo.shape[1]
    ub, gb = u_col // w_ch, gate_col // w_ch
    from_state = prev_state is not None
    if from_state:
        assert L == s
        prev = prev_state
        prev_spec = pl.BlockSpec((1, nb, POOL_HALO - 1, w_ch), lambda n, i: (layer, n, 0, 0))
    else:
        assert nb == 1 and L % POOL_HALO == 0
        prev = main3
        r = L // POOL_HALO
        prev_spec = pl.BlockSpec((1, POOL_HALO, w_ch), lambda n, i: (n, jnp.maximum(i * r - 1, 0), ub))
    return pl.pallas_call(
        functools.partial(_pool_kernel, nb=nb, L=L, from_state=from_state, pos0=pos0),
        grid=(b // nb, s // L),
        in_specs=[
            pl.BlockSpec((nb, L, w_ch), lambda n, i: (n, i, ub)),
            prev_spec,
            pl.BlockSpec((nb, L, w_ch), lambda n, i: (n, i, gb)),
            pl.BlockSpec((1,) + wg.shape[1:], lambda n, i: (layer, 0, 0, 0)),
            pl.BlockSpec((1, w_ch), lambda n, i: (0, 0)),
            pl.BlockSpec((1,) + wo.shape[1:], lambda n, i: (layer, 0, 0)),
        ],
        out_specs=pl.BlockSpec((nb, L, w_ch), lambda n, i: (n, i, 0)),
        out_shape=jax.ShapeDtypeStruct((b, s, w_ch), F32),
        scratch_shapes=[pltpu.VMEM((nb, POOL_HALO + L, w_ch), F32), pltpu.VMEM((nb * L, w_ch), BF16)],
        compiler_params=_cparams("parallel", "arbitrary"),
        name="pool_state" if from_state else "pool_scan",
    )(main3, prev, main3, wg, scale.reshape(1, w_ch), wo)


def _cumsum_rows(x):
    q = x.shape[0]
    row = lax.broadcasted_iota(jnp.int32, x.shape, 0)
    k = 1
    while k < q:
        x = x + jnp.where(row >= k, pltpu.roll(x, k, axis=0), 0.0)
        k *= 2
    return x


def _expand_heads(vals, e_ref):
    q = vals[0].shape[0]
    pieces = []
    for v in vals:
        hi = v.astype(BF16).astype(F32)
        r1 = v - hi
        mid = r1.astype(BF16).astype(F32)
        pieces += [hi, mid, r1 - mid]
    lhs = jnp.concatenate(pieces, axis=0).astype(BF16)
    out = jnp.dot(lhs, e_ref[...], preferred_element_type=F32)
    return [out[(3 * i) * q:(3 * i + 1) * q] + out[(3 * i + 1) * q:(3 * i + 2) * q] + out[(3 * i + 2) * q:(3 * i + 3) * q]
            for i in range(len(vals))]


def _ssd_kernel(*refs, q, nb, n_groups, heads_per_group, head_dim, n_state, from_state, conv_k):
    if from_state:
        (xbc_ref, prev_ref, z_ref, dt_ref, h0_ref, cw_ref, cb_ref, dtb_ref, alog_ref, dexp_ref, norm_ref, e_ref,
         *_, y_ref, hout_ref, ext_ref, act_ref, yacc_ref) = refs
        params = (dtb_ref, alog_ref, dexp_ref, norm_ref, e_ref)
        hist = conv_k - 1
        for n in range(nb):
            ext = ext_ref.at[n]
            ext[0:CONV_HALO, :] = jnp.zeros((CONV_HALO, ext.shape[1]), F32)
            ext[CONV_HALO - hist:CONV_HALO, :] = prev_ref[0, n]
            ext[CONV_HALO:CONV_HALO + q, :] = xbc_ref[n]
            act_ref[n] = _silu(_causal_conv(ext[...], cw_ref, q, 0) + cb_ref[...])
        for n in range(nb):
            _ssd_chunk(act_ref.at[n], z_ref.at[n], dt_ref.at[n], h0_ref.at[0, n], hout_ref.at[0, n], y_ref.at[n],
                       yacc_ref.at[n], params, q=q, n_groups=n_groups, heads_per_group=heads_per_group,
                       head_dim=head_dim, n_state=n_state)
    else:
        (xbc_ref, prev_ref, z_ref, dt_ref, cw_ref, cb_ref, dtb_ref, alog_ref, dexp_ref, norm_ref, e_ref,
         y_ref, hout_ref, ext_ref, act_ref, h_ref, yacc_ref) = refs
        params = (dtb_ref, alog_ref, dexp_ref, norm_ref, e_ref)
        c = pl.program_id(1)

        @pl.when(c == 0)
        def _():
            h_ref[...] = jnp.zeros_like(h_ref)

        ext = ext_ref.at[0]
        ext[0:CONV_HALO, :] = jnp.where(c == 0, 0.0, prev_ref[0])
        ext[CONV_HALO:CONV_HALO + q, :] = xbc_ref[0]
        act_ref[0] = _silu(_causal_conv(ext[...], cw_ref, q, 0) + cb_ref[...])
        _ssd_chunk(act_ref.at[0], z_ref.at[0], dt_ref.at[0], h_ref, h_ref, y_ref.at[0], yacc_ref.at[0], params,
                   q=q, n_groups=n_groups, heads_per_group=heads_per_group, head_dim=head_dim, n_state=n_state)

        @pl.when(c == pl.num_programs(1) - 1)
        def _():
            hout_ref[0] = h_ref[...]


def _ssd_chunk(act_ref, z_ref, dt_ref, h_in, h_out, y_ref, yacc_ref, params,
               *, q, n_groups, heads_per_group, head_dim, n_state):
    dtb_ref, alog_ref, dexp_ref, norm_ref, e_ref = params
    gw = heads_per_group * head_dim
    inner = n_groups * gw

    dt = jax.nn.softplus(dt_ref[...] + dtb_ref[...])
    la_cs = _cumsum_rows(dt * (-jnp.exp(alog_ref[...])))
    la_cs_t = la_cs.T
    dt_x, la_x = _expand_heads([dt, la_cs], e_ref)
    from_start_x = jnp.exp(la_x)
    to_end_x = jnp.exp(la_x[q - 1:q, :] - la_x)
    chunk_decay = jnp.exp(la_cs[q - 1:q, :])

    xs = act_ref[:, 0:inner]
    xd = xs * dt_x
    xdw = xd * to_end_x
    yacc_ref[...] = dexp_ref[...] * xs

    tri = lax.broadcasted_iota(jnp.int32, (q, q), 0) >= lax.broadcasted_iota(jnp.int32, (q, q), 1)
    assert LANES % head_dim == 0
    hpl = LANES // head_dim
    lane_head = lax.broadcasted_iota(jnp.int32, (q, LANES), 1) // head_dim
    bms, cbs = [], []
    for g in range(n_groups):
        gs = slice(g * gw, (g + 1) * gw)
        bm = act_ref[:, inner + g * n_state:inner + (g + 1) * n_state].astype(BF16)
        cm = act_ref[:, inner + (n_groups + g) * n_state:inner + (n_groups + g + 1) * n_state].astype(BF16)
        bms.append(bm)
        cbs.append(lax.dot_general(cm, bm, (((1,), (1,)), ((), ())), preferred_element_type=F32))
        y_off = lax.dot_general(cm, h_in[g].astype(BF16), (((1,), (1,)), ((), ())),
                                preferred_element_type=F32)
        yacc_ref[:, gs] += y_off * from_start_x[:, gs]
    for g in range(n_groups):
        cb = cbs[g]
        for lt in range(gw // LANES):
            ls = slice(g * gw + lt * LANES, g * gw + (lt + 1) * LANES)
            xd_t = xd[:, ls]
            y_t = None
            for k in range(hpl):
                hd = (g * gw + lt * LANES) // head_dim + k
                seg = la_cs[:, hd:hd + 1] - la_cs_t[hd:hd + 1, :]
                m = (cb * jnp.exp(jnp.where(tri, seg, -jnp.inf))).astype(BF16)
                rhs = jnp.where(lane_head == k, xd_t, 0.0).astype(BF16)
                part = jnp.dot(m, rhs, preferred_element_type=F32)
                y_t = part if y_t is None else y_t + part
            yacc_ref[:, ls] += y_t
    for g in range(n_groups):
        gs = slice(g * gw, (g + 1) * gw)
        s_new = lax.dot_general(xdw[:, gs].astype(BF16), bms[g], (((0,), (0,)), ((), ())),
                                preferred_element_type=F32)
        for r in range(heads_per_group):
            hd = g * heads_per_group + r
            rows = slice(r * head_dim, (r + 1) * head_dim)
            h_out[g, rows, :] = h_in[g, rows, :] * chunk_decay[:, hd:hd + 1] + s_new[rows, :]

    y = yacc_ref[...] * _silu(z_ref[...])
    for g in range(n_groups):
        gs = slice(g * gw, (g + 1) * gw)
        yg = y[:, gs]
        ms = jnp.mean(yg * yg, axis=-1, keepdims=True)
        y_ref[:, gs] = (yg * lax.rsqrt(ms + EPS) * norm_ref[:, gs]).astype(y_ref.dtype)


def ssd_branch(main3, dt3, conv_state, ssm_state, state_out, cw, cb, dt_bias, a_log, d_exp, norm, expand,
               *, layer, xbc_col, z_col, q, nb, n_groups, heads_per_group, head_dim, n_state, out_dtype):
    b, s, _ = main3.shape
    gw = heads_per_group * head_dim
    inner = n_groups * gw
    conv_dim = inner + 2 * n_groups * n_state
    conv_k = cw.shape[0]
    from_state = ssm_state is not None
    xb, zb = xbc_col // conv_dim, z_col // inner
    assert xbc_col % conv_dim == 0 and z_col % inner == 0

    def const(shape):
        nd = len(shape)
        return pl.BlockSpec(shape, lambda n, c: (0,) * nd)

    assert b % nb == 0 and (from_state or nb == 1)
    in_specs = [pl.BlockSpec((nb, q, conv_dim), lambda n, c: (n, c, xb))]
    args = [main3]
    if from_state:
        assert q == s
        in_specs.append(pl.BlockSpec((1, nb, conv_k - 1, conv_dim), lambda n, c: (layer, n, 0, 0)))
        args.append(conv_state)
    else:
        assert q % CONV_HALO == 0
        r = q // CONV_HALO
        in_specs.append(pl.BlockSpec((1, CONV_HALO, conv_dim), lambda n, c: (n, jnp.maximum(c * r - 1, 0), xb)))
        args.append(main3)
    in_specs += [pl.BlockSpec((nb, q, inner), lambda n, c: (n, c, zb)),
                 pl.BlockSpec((nb, q, LANES), lambda n, c: (n, c, 0))]
    args += [main3, dt3]
    if from_state:
        in_specs.append(pl.BlockSpec((1, nb, n_groups, gw, n_state), lambda n, c: (layer, n, 0, 0, 0)))
        args.append(ssm_state)
    small = [cw, cb.reshape(1, conv_dim), dt_bias, a_log, d_exp, norm.reshape(1, inner), expand]
    in_specs += [const(a.shape) for a in small]
    args += small
    aliases = {}
    if from_state:
        state_spec = pl.BlockSpec((1, nb, n_groups, gw, n_state), lambda n, c: (layer, n, 0, 0, 0))
        state_shape = jax.ShapeDtypeStruct(ssm_state.shape, F32)
        if state_out is not None:
            aliases = {len(args): 1}
            in_specs.append(pl.BlockSpec(memory_space=pl.ANY))
            args.append(state_out)
    else:
        state_spec = pl.BlockSpec((1, n_groups, gw, n_state), lambda n, c: (n, 0, 0, 0))
        state_shape = jax.ShapeDtypeStruct((b, n_groups, gw, n_state), F32)
    return pl.pallas_call(
        functools.partial(_ssd_kernel, q=q, nb=nb, n_groups=n_groups, heads_per_group=heads_per_group,
                          head_dim=head_dim, n_state=n_state, from_state=from_state, conv_k=conv_k),
        grid=(b // nb, s // q),
        in_specs=in_specs,
        out_specs=[pl.BlockSpec((nb, q, inner), lambda n, c: (n, c, 0)), state_spec],
        out_shape=[jax.ShapeDtypeStruct((b, s, inner), out_dtype), state_shape],
        input_output_aliases=aliases,
        scratch_shapes=[pltpu.VMEM((nb, CONV_HALO + q, conv_dim), F32),
                        pltpu.VMEM((nb, q, conv_dim), F32)]
                       + ([] if from_state else [pltpu.VMEM((n_groups, gw, n_state), F32)])
                       + [pltpu.VMEM((nb, q, inner), F32)],
        compiler_params=_cparams("parallel", "arbitrary"),
        name="ssd_state" if from_state else "ssd_scan",
    )(*args)


def _kv_head(ref, n, h, d, stage_ref, slot):
    if len(ref.shape) == 5:
        stage_ref[slot] = ref[0, n, :, h, :]
        return stage_ref[slot].astype(BF16)
    return ref[0, n, :, h * d:(h + 1) * d].astype(BF16)


def _attn_kernel(q_ref, k_ref, v_ref, o_ref, *stage, n_heads, nb):
    stage_ref = stage[0] if stage else None
    d = q_ref.shape[-1] // n_heads
    inv = 1.0 / math.sqrt(d)
    units = [(n, h, slice(h * d, (h + 1) * d)) for n in range(nb) for h in range(n_heads)]
    scores = []
    for u, (n, h, hs) in enumerate(units):
        qh = q_ref[n, :, hs].astype(BF16)
        kh = _kv_head(k_ref, n, h, d, stage_ref, 2 * u)
        scores.append(lax.dot_general(qh, kh, (((1,), (1,)), ((), ())), preferred_element_type=F32) * inv)
    probs = []
    for s in scores:
        e = jnp.exp(s - jnp.max(s, axis=-1, keepdims=True))
        probs.append((e / jnp.sum(e, axis=-1, keepdims=True)).astype(BF16))
    for u, ((n, h, hs), p) in enumerate(zip(units, probs)):
        vh = _kv_head(v_ref, n, h, d, stage_ref, 2 * u + 1)
        o_ref[n, :, hs] = jnp.dot(p, vh, preferred_element_type=F32).astype(o_ref.dtype)


def _attn_few_kernel(q_ref, k_ref, v_ref, o_ref, *stage, n_heads, nb):
    stage_ref = stage[0] if stage else None
    tq = q_ref.shape[1]
    d = q_ref.shape[-1] // n_heads
    inv = 1.0 / math.sqrt(d)
    pad = jnp.zeros((LANES - tq, d), F32)
    units = [(n, h, slice(h * d, (h + 1) * d)) for n in range(nb) for h in range(n_heads)]
    scores_t = []
    for u, (n, h, hs) in enumerate(units):
        qh = jnp.concatenate([q_ref[n, :, hs].astype(F32), pad], axis=0).astype(BF16)
        kh = _kv_head(k_ref, n, h, d, stage_ref, 2 * u)
        scores_t.append(lax.dot_general(kh, qh, (((1,), (1,)), ((), ())), preferred_element_type=F32) * inv)
    probs = []
    for st in scores_t:
        e = jnp.exp(st - jnp.max(st, axis=0, keepdims=True))
        pt = e / jnp.sum(e, axis=0, keepdims=True)
        probs.append(pt.T[0:tq, :].astype(BF16))
    for u, ((n, h, hs), p) in enumerate(zip(units, probs)):
        vh = _kv_head(v_ref, n, h, d, stage_ref, 2 * u + 1)
        o_ref[n, :, hs] = jnp.dot(p, vh, preferred_element_type=F32).astype(o_ref.dtype)


def cross_attention(q3, k4, v4, *, layer, nb, tq, n_heads, out_dtype):
    b, s, dm = q3.shape
    m = k4.shape[2]
    body = _attn_few_kernel if tq < LANES else _attn_kernel
    zeros = (0,) * (k4.ndim - 2)
    kv_spec = pl.BlockSpec((1, nb) + k4.shape[2:], lambda n, i: (layer, n) + zeros)
    stage = [pltpu.VMEM((2 * nb * n_heads, m, dm // n_heads), F32)] if k4.ndim == 5 else []
    return pl.pallas_call(
        functools.partial(body, n_heads=n_heads, nb=nb),
        grid=(b // nb, s // tq),
        in_specs=[pl.BlockSpec((nb, tq, dm), lambda n, i: (n, i, 0)), kv_spec, kv_spec],
        out_specs=pl.BlockSpec((nb, tq, dm), lambda n, i: (n, i, 0)),
        out_shape=jax.ShapeDtypeStruct((b, s, dm), out_dtype),
        scratch_shapes=stage,
        compiler_params=_cparams("parallel", "arbitrary"),
        name="cross_attention",
    )(q3, k4, v4)


def _ffn_mid_kernel(g_ref, v_ref, gp_ref, vp_ref, gw_ref, vw_ref, gb_ref, vb_ref, o_ref, gext_ref, vext_ref,
                    *, L, from_state, conv_k):
    hist = conv_k - 1
    i = pl.program_id(1)

    def conv(cur_ref, prev_ref, w_ref, b_ref, ext_ref):
        if from_state:
            ext_ref[:, 0:CONV_HALO, :] = jnp.zeros((ext_ref.shape[0], CONV_HALO, ext_ref.shape[2]), F32)
            ext_ref[:, CONV_HALO - hist:CONV_HALO, :] = prev_ref[0]
        else:
            ext_ref[:, 0:CONV_HALO, :] = jnp.where(i == 0, 0.0, prev_ref[...])
        ext_ref[:, CONV_HALO:CONV_HALO + L, :] = cur_ref[...]
        return _causal_conv(ext_ref[...], w_ref, L, 1) + b_ref[...]

    g = conv(g_ref, gp_ref, gw_ref, gb_ref, gext_ref)
    v = conv(v_ref, vp_ref, vw_ref, vb_ref, vext_ref)
    o_ref[...] = (_silu(g) * v).astype(o_ref.dtype)


def ffn_mid(up3, prev_state, cw, cb, *, layer, nb, L, tc, out_dtype):
    b, s, two_f = up3.shape
    f = two_f // 2
    conv_k = cw.shape[0]
    assert f % tc == 0
    nj = f // tc
    from_state = prev_state is not None
    if from_state:
        assert L == s
        prev = prev_state
        gp_spec = pl.BlockSpec((1, nb, conv_k - 1, tc), lambda n, i, j: (layer, n, 0, j))
        vp_spec = pl.BlockSpec((1, nb, conv_k - 1, tc), lambda n, i, j: (layer, n, 0, j + nj))
    else:
        assert L % CONV_HALO == 0
        prev = up3
        r = L // CONV_HALO
        gp_spec = pl.BlockSpec((nb, CONV_HALO, tc), lambda n, i, j: (n, jnp.maximum(i * r - 1, 0), j))
        vp_spec = pl.BlockSpec((nb, CONV_HALO, tc), lambda n, i, j: (n, jnp.maximum(i * r - 1, 0), j + nj))
    cb2 = cb.reshape(1, two_f)
    return pl.pallas_call(
        functools.partial(_ffn_mid_kernel, L=L, from_state=from_state, conv_k=conv_k),
        grid=(b // nb, s // L, nj),
        in_specs=[
            pl.BlockSpec((nb, L, tc), lambda n, i, j: (n, i, j)),
            pl.BlockSpec((nb, L, tc), lambda n, i, j: (n, i, j + nj)),
            gp_spec, vp_spec,
            pl.BlockSpec((conv_k, tc), lambda n, i, j: (0, j)),
            pl.BlockSpec((conv_k, tc), lambda n, i, j: (0, j + nj)),
            pl.BlockSpec((1, tc), lambda n, i, j: (0, j)),
            pl.BlockSpec((1, tc), lambda n, i, j: (0, j + nj)),
        ],
        out_specs=pl.BlockSpec((nb, L, tc), lambda n, i, j: (n, i, j)),
        out_shape=jax.ShapeDtypeStruct((b, s, f), out_dtype),
        scratch_shapes=[pltpu.VMEM((nb, CONV_HALO + L, tc), F32), pltpu.VMEM((nb, CONV_HALO + L, tc), F32)],
        compiler_params=_cparams("parallel", "arbitrary", "arbitrary"),
        name="ffn_mid_state" if from_state else "ffn_mid_scan",
    )(up3, up3, prev, prev, cw, cw, cb2, cb2)


def _ffn_front_kernel(x_ref, g_ref, wg_ref, wv_ref, cwg_ref, cwv_ref, cbg_ref, cbv_ref,
                      act_ref, tail_g_ref, tail_v_ref, xn_ref, halo_ref, *, tiles_per_seq):
    i, j = pl.program_id(0), pl.program_id(1)
    tm = x_ref.shape[0]

    @pl.when(j == 0)
    def _():
        x = x_ref[...]
        ms = jnp.mean(x * x, axis=-1, keepdims=True)
        xn_ref[...] = (x * lax.rsqrt(ms + EPS) * g_ref[...]).astype(BF16)

    xn = xn_ref[...]
    seq_start = (i % tiles_per_seq) == 0

    def half(w_ref, cw_ref, cb_ref, tail_ref, slot):
        up = jnp.dot(xn, w_ref[0], preferred_element_type=F32)
        prev = jnp.where(seq_start, 0.0, halo_ref[slot, j])
        tail = up[tm - CONV_HALO:tm, :]
        halo_ref[slot, j] = tail
        tail_ref[0] = tail
        ext = jnp.concatenate([prev, up], axis=0)
        return _causal_conv(ext, cw_ref, tm, 0) + cb_ref[...]

    g = half(wg_ref, cwg_ref, cbg_ref, tail_g_ref, 0)
    v = half(wv_ref, cwv_ref, cbv_ref, tail_v_ref, 1)
    act_ref[...] = (_silu(g) * v).astype(act_ref.dtype)


def ffn_front(x2, gain, w_up, cw, cb, *, layer, seq_len, tm, tn, out_dtype):
    t, k = x2.shape
    f = w_up.shape[2] // 2
    assert t % tm == 0 and f % tn == 0 and seq_len % tm == 0
    nj = f // tn
    conv_k = cw.shape[0]
    cb2 = cb.reshape(1, 2 * f)
    return pl.pallas_call(
        functools.partial(_ffn_front_kernel, tiles_per_seq=seq_len // tm),
        grid=(t // tm, nj),
        in_specs=[
            pl.BlockSpec((tm, k), lambda i, j: (i, 0)),
            pl.BlockSpec((1, k), lambda i, j: (0, 0)),
            pl.BlockSpec((1, k, tn), lambda i, j: (layer, 0, j)),
            pl.BlockSpec((1, k, tn), lambda i, j: (layer, 0, j + nj)),
            pl.BlockSpec((conv_k, tn), lambda i, j: (0, j)),
            pl.BlockSpec((conv_k, tn), lambda i, j: (0, j + nj)),
            pl.BlockSpec((1, tn), lambda i, j: (0, j)),
            pl.BlockSpec((1, tn), lambda i, j: (0, j + nj)),
        ],
        out_specs=[pl.BlockSpec((tm, tn), lambda i, j: (i, j)),
                   pl.BlockSpec((1, CONV_HALO, tn), lambda i, j: (i, 0, j)),
                   pl.BlockSpec((1, CONV_HALO, tn), lambda i, j: (i, 0, j))],
        out_shape=[jax.ShapeDtypeStruct((t, f), out_dtype),
                   jax.ShapeDtypeStruct((t // tm, CONV_HALO, f), F32),
                   jax.ShapeDtypeStruct((t // tm, CONV_HALO, f), F32)],
        scratch_shapes=[pltpu.VMEM((tm, k), BF16), pltpu.VMEM((2, nj, CONV_HALO, tn), F32)],
        compiler_params=_cparams("arbitrary", "arbitrary"),
        name="ffn_front",
    )(x2, gain.reshape(1, k), w_up, w_up, cw, cw, cb2, cb2)


def _history_tail(prev, cur, hist):
    s = cur.shape[1]
    if s >= hist:
        return cur[:, s - hist:]
    return jnp.concatenate([prev[:, s:], cur], axis=1)


def _layer(x3, wts, lw, dims, states, ssm_out, mem_kv, *, layer, kv_layer, pos0, tiles):
    b, s, d = x3.shape
    t = b * s
    x2 = x3.reshape(t, d)
    G, R, P, N = dims["groups"], dims["heads_per_group"], dims["head_dim"], dims["n_state"]
    inner = G * R * P
    conv_dim = inner + 2 * G * N
    z_col, u_col, xbc_col, gp_col, gs_col = 0, inner, inner + d, inner + d + conv_dim, inner + 2 * d + conv_dim
    n_main = gs_col + d
    tm, tn, tn_k = tiles["tm"], tiles["tn"], tiles["tn_long_k"]
    act_dtype = tiles["act_dtype"]
    hp, hc, hf = POOL_HALO - 1, lw["ssm_conv_w"].shape[0] - 1, lw["ffn_conv_w"].shape[0] - 1

    if states is None:
        pool_prev = conv_prev = ssm_prev = ffn_prev = None
    else:
        pool_prev, conv_prev, ssm_prev, ffn_prev = states

    nu, nz = d // tn, inner // tn
    assert d % tn == 0 and inner % tn == 0
    main, dt = norm_matmul(x2, lw["norm_mix"], [(wts["w_in_uzx"], gp_col), wts["w_in_gates"]], wts["w_dt"], layer=layer,
                           tm=tiles["tm_norm"], tn=tn, name="in_proj",
                           out_block=lambda j: jnp.where(j < nu, j + nz, jnp.where(j < nu + nz, j - nu, j)))
    main3 = main.reshape(b, s, n_main)
    dt3 = dt.reshape(b, s, LANES)

    gated_pool = pool_branch(main3, pool_prev, wts["w_pool_group"], lw["pool_scale"], wts["w_pool_out"], layer=layer,
                             u_col=u_col, gate_col=gp_col, nb=tiles["pool_nb"], L=tiles["pool_L"], pos0=pos0)
    y, ssm_new = ssd_branch(main3, dt3, conv_prev, ssm_prev, ssm_out, lw["ssm_conv_w"], lw["ssm_conv_b"],
                            lw["dt_bias"], lw["a_log"], lw["d_exp"], lw["ssm_norm"], lw["expand"], layer=layer,
                            xbc_col=xbc_col, z_col=z_col, q=tiles["ssd_q"], nb=tiles["ssd_nb"], n_groups=G,
                            heads_per_group=R, head_dim=P, n_state=N, out_dtype=act_dtype)
    merged = matmul(y.reshape(t, inner), wts["w_ssm_out"],
                    [(gated_pool.reshape(t, d), 0), (main, gs_col)], _ep_gate_merge,
                    layer=layer, tm=tm, tn=tn_k, out_dtype=BF16, name="ssm_out_merge")
    x2 = matmul(merged, wts["w_out"], [(x2, 0)], _ep_residual, layer=layer, tm=tm, tn=tn, out_dtype=F32,
                name="mix_out")

    qm = norm_matmul(x2, lw["norm_mem_q"], wts["w_mem_q"], layer=layer, tm=tiles["tm_norm"], tn=tn, out_dtype=act_dtype,
                     name="mem_q")
    k4, v4 = mem_kv
    o = cross_attention(qm.reshape(b, s, d), k4, v4, layer=kv_layer, nb=tiles["attn_nb"], tq=tiles["attn_tq"],
                        n_heads=dims["mem_heads"], out_dtype=act_dtype)
    x2 = matmul(o.reshape(t, d), wts["w_mem_o"], [(x2, 0)], _ep_residual, layer=layer, tm=tm, tn=tn, out_dtype=F32,
                name="mem_o")

    if states is None:
        act2, tail_g, tail_v = ffn_front(x2, lw["norm_ffn"], wts["w_ffn_up"], lw["ffn_conv_w"], lw["ffn_conv_b"],
                                         layer=layer, seq_len=s, tm=tm, tn=tiles["ffn_tc"], out_dtype=act_dtype)
        per_seq = s // tm
        ffn_new = jnp.concatenate([tail_g[per_seq - 1::per_seq, CONV_HALO - hf:],
                                   tail_v[per_seq - 1::per_seq, CONV_HALO - hf:]], axis=-1)
    else:
        up = norm_matmul(x2, lw["norm_ffn"], wts["w_ffn_up"], layer=layer, tm=tiles["tm_norm"], tn=tn, name="ffn_up")
        up3 = up.reshape(b, s, up.shape[1])
        act2 = ffn_mid(up3, ffn_prev, lw["ffn_conv_w"], lw["ffn_conv_b"], layer=layer, nb=tiles["ffn_nb"],
                       L=tiles["ffn_L"], tc=tiles["ffn_tc"], out_dtype=act_dtype).reshape(t, -1)
        ffn_new = _history_tail(ffn_prev[layer], up3, hf)
    x2 = matmul(act2, wts["w_ffn_down"], [(x2, 0)], _ep_residual,
                layer=layer, tm=tiles["tm_down"], tn=tn_k, out_dtype=F32, name="ffn_down")

    u3 = main3[:, :, u_col:u_col + d]
    xbc3 = main3[:, :, xbc_col:xbc_col + conv_dim]
    if states is None:
        pool_new, conv_new = u3[:, s - hp:], xbc3[:, s - hc:]
    else:
        pool_new = _history_tail(pool_prev[layer], u3, hp)
        conv_new = _history_tail(conv_prev[layer], xbc3, hc)
    return x2.reshape(b, s, d), pool_new, conv_new, ssm_new, ffn_new


def kernel(x_prompt, x_sample, state_pool, state_ssm_conv, state_ssm, state_ffn_conv, cache_mem_k, cache_mem_v,
           mem_prompt, norm_mix, w_in, w_pool_group, pool_scale, w_pool_out, ssm_conv_w, ssm_conv_b, ssm_dt_bias,
           ssm_a_log, ssm_d, ssm_norm, w_ssm_out, w_out, norm_mem_q, w_mem_q, w_mem_o, norm_mem_kv, w_mem_k,
           w_mem_v, norm_ffn, w_ffn_up, ffn_conv_w, ffn_conv_b, w_ffn_down, norm_final):
    depth = w_in.shape[0]
    bp, sp, d = x_prompt.shape
    bs, ss, _ = x_sample.shape
    n_heads = ssm_d.shape[1]
    inner = w_ssm_out.shape[1]
    head_dim = inner // n_heads
    n_state = state_ssm.shape[-1]
    conv_dim = ssm_conv_w.shape[2]
    n_groups = (conv_dim - inner) // (2 * n_state)
    mem_heads = cache_mem_k.shape[3]
    mem_len = mem_prompt.shape[1]
    dims = dict(groups=n_groups, heads_per_group=n_heads // n_groups, head_dim=head_dim, n_state=n_state,
                mem_heads=mem_heads)
    assert n_heads <= LANES

    c_u, c_z, c_x, c_dt, c_gp = d, d + inner, d + inner + conv_dim, d + inner + conv_dim + n_heads, 2 * d + inner + conv_dim + n_heads
    head_of_col = jnp.arange(inner, dtype=jnp.int32) // head_dim
    expand = (jnp.arange(LANES, dtype=jnp.int32)[:, None] == head_of_col[None, :]).astype(BF16)

    def pad_heads(v):
        return jnp.pad(v, (0, LANES - n_heads)).reshape(1, LANES)

    tiles_p = dict(tm=1024, tm_norm=1024, tm_down=1024, tn=1024, tn_long_k=512, pool_nb=1, pool_L=512, ssd_q=128,
                   ssd_nb=1, attn_nb=1, attn_tq=512, ffn_nb=1, ffn_L=512, ffn_tc=512, act_dtype=BF16)
    tiles_s = dict(tm=512, tm_norm=1024, tm_down=512, tn=1024, tn_long_k=512, pool_nb=32, pool_L=ss, ssd_q=ss,
                   ssd_nb=4, attn_nb=4, attn_tq=ss, ffn_nb=64, ffn_L=ss, ffn_tc=512, act_dtype=F32)

    ssm_states = state_ssm.reshape(depth, bs, n_groups, inner // n_groups, n_state)
    kv_s = (cache_mem_k, cache_mem_v)
    states_s = (state_pool, state_ssm_conv, ssm_states, state_ffn_conv)
    ssm_s_all = None

    w_in_bf16 = w_in.astype(BF16)
    wts = dict(
        w_in_uzx=w_in_bf16, w_in_gates=w_in_bf16[:, :, c_dt:],
        w_dt=jnp.pad(w_in[:, :, c_x:c_dt], ((0, 0), (0, 0), (0, LANES - n_heads))).astype(BF16),
        w_pool_group=w_pool_group.astype(BF16), w_pool_out=w_pool_out.astype(BF16),
        w_ssm_out=w_ssm_out.astype(BF16), w_out=w_out.astype(BF16),
        w_mem_q=w_mem_q.astype(BF16), w_mem_o=w_mem_o.astype(BF16),
        w_mem_k=w_mem_k.astype(BF16), w_mem_v=w_mem_v.astype(BF16),
        w_ffn_up=w_ffn_up.astype(BF16), w_ffn_down=w_ffn_down.astype(BF16),
    )

    yp, ys = x_prompt, x_sample
    outs = [[] for _ in range(9)]
    for i in range(depth):
        lw = dict(
            norm_mix=norm_mix[i], pool_scale=pool_scale[i],
            ssm_conv_w=ssm_conv_w[i], ssm_conv_b=ssm_conv_b[i],
            dt_bias=pad_heads(ssm_dt_bias[i]), a_log=pad_heads(ssm_a_log[i]),
            d_exp=jnp.repeat(ssm_d[i], head_dim).reshape(1, inner), ssm_norm=ssm_norm[i], expand=expand,
            norm_mem_q=norm_mem_q[i], norm_ffn=norm_ffn[i],
            ffn_conv_w=ffn_conv_w[i], ffn_conv_b=ffn_conv_b[i],
        )
        mem2 = mem_prompt.reshape(bp * mem_len, d)
        k_i = norm_matmul(mem2, norm_mem_kv[i], wts["w_mem_k"], layer=i, tm=512, tn=1024, name="mem_k")
        v_i = norm_matmul(mem2, norm_mem_kv[i], wts["w_mem_v"], layer=i, tm=512, tn=1024, name="mem_v")
        kv_p = (k_i.reshape(1, bp, mem_len, d), v_i.reshape(1, bp, mem_len, d))

        yp, a0, a1, a2, a3 = _layer(yp, wts, lw, dims, None, None, kv_p, layer=i, kv_layer=0, pos0=0, tiles=tiles_p)
        ys, b0, b1, ssm_s_all, b3 = _layer(ys, wts, lw, dims, states_s, ssm_s_all, kv_s, layer=i, kv_layer=i,
                                           pos0=PAST_LEN, tiles=tiles_s)
        for lst, val in zip(outs, (a0, b0, a1, b1, a2.reshape(bp, n_heads, head_dim, n_state), a3, b3,
                                   k_i.reshape(bp, mem_len, mem_heads, d // mem_heads),
                                   v_i.reshape(bp, mem_len, mem_heads, d // mem_heads))):
            lst.append(val)

    y_prompt = rmsnorm(yp.reshape(bp * sp, d), norm_final, tm=512).reshape(bp, sp, d)
    y_sample = rmsnorm(ys.reshape(bs * ss, d), norm_final, tm=512).reshape(bs, ss, d)
    pool_p, pool_s, conv_p, conv_s, ssm_p, ffn_p, ffn_s, mk_p, mv_p = (jnp.stack(lst) for lst in outs)
    ssm_s = ssm_s_all.reshape(depth, bs, n_heads, head_dim, n_state)
    return (y_prompt, y_sample, pool_p, pool_s, conv_p, conv_s, ssm_p, ssm_s, ffn_p, ffn_s, mk_p, mv_p)
```
